```python
import math
import jax, jax.numpy as jnp
from jax import lax
import numpy as np

D_MODEL = 2048
BATCH = 4
SEQ = 2048
DEPTH = 1

CHUNK = 64
Q_BLOCK = 128

N_HEADS = 16
QK_NOPE = 128
QK_ROPE = 64
V_HEAD = 128
Q_LORA = 512
KV_LORA = 512
ROPE_THETA = 10000.0
ATTN_SCALE = (QK_NOPE + QK_ROPE) ** -0.5

CONV_WIDTH = D_MODEL
CONV_K = 3

D_FF = ((8 * D_MODEL + 3 * 256 - 1) // (3 * 256)) * 256

COL_Q_A = Q_LORA
COL_KV_A = KV_LORA
COL_K_ROPE = QK_ROPE
COL_CONV_B = CONV_WIDTH
COL_CONV_C = CONV_WIDTH
COL_CONV_X = CONV_WIDTH
COL_GATE_A = D_MODEL
COL_GATE_B = D_MODEL
D_IN_ALL = COL_Q_A + COL_KV_A + COL_K_ROPE + COL_CONV_B + COL_CONV_C + COL_CONV_X + COL_GATE_A + COL_GATE_B
SPLIT_POINTS = (
    COL_Q_A,
    COL_Q_A + COL_KV_A,
    COL_Q_A + COL_KV_A + COL_K_ROPE,
    COL_Q_A + COL_KV_A + COL_K_ROPE + COL_CONV_B,
    COL_Q_A + COL_KV_A + COL_K_ROPE + COL_CONV_B + COL_CONV_C,
    COL_Q_A + COL_KV_A + COL_K_ROPE + COL_CONV_B + COL_CONV_C + COL_CONV_X,
    COL_Q_A + COL_KV_A + COL_K_ROPE + COL_CONV_B + COL_CONV_C + COL_CONV_X + COL_GATE_A,
)

DEEPNORM_ALPHA = (2.0 * DEPTH) ** 0.25
DEEPNORM_BETA = (8.0 * DEPTH) ** -0.25
LN_EPS = 1e-5
RMS_EPS = 1e-6

kernel_name = "hybrid_mla_shortconv_swiglu_deepnorm_adaln"


def layer_norm(x, g, b):
    xf = x.astype(jnp.float32)
    mu = jnp.mean(xf, axis=-1, keepdims=True)
    var = jnp.mean(jnp.square(xf - mu), axis=-1, keepdims=True)
    y = (xf - mu) * lax.rsqrt(var + LN_EPS)
    return (y * g.astype(jnp.float32) + b.astype(jnp.float32)).astype(x.dtype)


def rms_norm(x, g):
    xf = x.astype(jnp.float32)
    y = xf * lax.rsqrt(jnp.mean(jnp.square(xf), axis=-1, keepdims=True) + RMS_EPS)
    return (y * g.astype(jnp.float32)).astype(x.dtype)


def rope_tables(positions):
    inv_freq = 1.0 / (ROPE_THETA ** (jnp.arange(0, QK_ROPE, 2, dtype=jnp.float32) / QK_ROPE))
    ang = positions.astype(jnp.float32)[..., None] * inv_freq
    return jnp.cos(ang), jnp.sin(ang)


def apply_rope(x, cos, sin):
    xf = x.astype(jnp.float32)
    x1, x2 = jnp.split(xf, 2, axis=-1)
    out = jnp.concatenate([x1 * cos - x2 * sin, x2 * cos + x1 * sin], axis=-1)
    return out.astype(x.dtype)


def chunk_causal_mla_attention(q_nope, q_rope, k_nope, k_rope, v):
    S = q_nope.shape[1]
    outs = []
    for i in range(S // Q_BLOCK):
        q0 = i * Q_BLOCK
        L = q0 + Q_BLOCK
        s = (jnp.einsum('bqhd,bkhd->bhqk', q_nope[:, q0:L], k_nope[:, :L])
             + jnp.einsum('bqhr,bkr->bhqk', q_rope[:, q0:L], k_rope[:, :L]))
        s = s.astype(jnp.float32) * ATTN_SCALE
        q_chunk = (q0 + jnp.arange(Q_BLOCK)) // CHUNK
        k_chunk = jnp.arange(L) // CHUNK
        allowed = k_chunk[None, :] <= q_chunk[:, None]
        s = jnp.where(allowed[None, None], s, jnp.float32(-1e30))
        p = jax.nn.softmax(s, axis=-1).astype(v.dtype)
        outs.append(jnp.einsum('bhqk,bkhd->bqhd', p, v[:, :L]))
    return jnp.concatenate(outs, axis=1)


def causal_depthwise_conv(z, w_conv):
    S = z.shape[1]
    zp = jnp.pad(z, ((0, 0), (CONV_K - 1, 0), (0, 0)))
    out = zp[:, 0:S] * w_conv[0]
    for k in range(1, CONV_K):
        out = out + zp[:, k:k + S] * w_conv[k]
    return out


def setup_inputs(seed: int = 0) -> dict:
    key = jax.random.key(seed)
    ks = jax.random.split(key, 24)
    f32 = jnp.float32

    def nrm(k, shape, scale):
        return jax.random.normal(k, shape, f32) * scale

    x = nrm(ks[0], (BATCH, SEQ, D_MODEL), 1.0)
    c = nrm(ks[1], (BATCH, D_MODEL), 1.0)
    offsets = jax.random.randint(ks[2], (BATCH, 1), 0, 64, dtype=jnp.int32) * CHUNK
    positions = (offsets + jnp.arange(SEQ, dtype=jnp.int32)[None, :]).astype(jnp.int32)

    inputs = {
        "x": x,
        "c": c,
        "positions": positions,
        "w_ada": nrm(ks[3], (DEPTH, D_MODEL, 6 * D_MODEL), 0.5 * D_MODEL ** -0.5),
        "b_ada": nrm(ks[4], (DEPTH, 6 * D_MODEL), 0.01),
        "w_in": nrm(ks[5], (DEPTH, D_MODEL, D_IN_ALL), D_MODEL ** -0.5),
        "g_q_a": 1.0 + nrm(ks[6], (DEPTH, Q_LORA), 0.02),
        "w_q_b": nrm(ks[7], (DEPTH, Q_LORA, N_HEADS * (QK_NOPE + QK_ROPE)), Q_LORA ** -0.5),
        "g_kv_a": 1.0 + nrm(ks[8], (DEPTH, KV_LORA), 0.02),
        "w_kv_b": nrm(ks[9], (DEPTH, KV_LORA, N_HEADS * (QK_NOPE + V_HEAD)), KV_LORA ** -0.5),
        "w_o_a": nrm(ks[10], (DEPTH, N_HEADS * V_HEAD, D_MODEL), (N_HEADS * V_HEAD) ** -0.5 * DEEPNORM_BETA),
        "w_conv": nrm(ks[11], (DEPTH, CONV_K, CONV_WIDTH), CONV_K ** -0.5),
        "w_o_b": nrm(ks[12], (DEPTH, CONV_WIDTH, D_MODEL), CONV_WIDTH ** -0.5 * DEEPNORM_BETA),
        "w_o": nrm(ks[13], (DEPTH, D_MODEL, D_MODEL), D_MODEL ** -0.5 * DEEPNORM_BETA),
        "ln1_g": 1.0 + nrm(ks[14], (DEPTH, D_MODEL), 0.02),
        "ln1_b": nrm(ks[15], (DEPTH, D_MODEL), 0.02),
        "w_ffn_in": nrm(ks[16], (DEPTH, D_MODEL, 2 * D_FF), D_MODEL ** -0.5),
        "w_ffn_out": nrm(ks[17], (DEPTH, D_FF, D_MODEL), D_FF ** -0.5 * DEEPNORM_BETA),
        "ln2_g": 1.0 + nrm(ks[18], (DEPTH, D_MODEL), 0.02),
        "ln2_b": nrm(ks[19], (DEPTH, D_MODEL), 0.02),
    }
    return inputs


def reference(x, c, positions, w_ada, b_ada, w_in, g_q_a, w_q_b, g_kv_a, w_kv_b, w_o_a,
              w_conv, w_o_b, w_o, ln1_g, ln1_b, w_ffn_in, w_ffn_out, ln2_g, ln2_b):
    B, S, D = x.shape
    cos, sin = rope_tables(positions)
    cos_q, sin_q = cos[:, :, None, :], sin[:, :, None, :]
    c_act = jax.nn.silu(c)

    for l in range(DEPTH):
        mod = c_act @ w_ada[l] + b_ada[l]
        shift1, scale1, gate1, shift2, scale2, gate2 = [m[:, None, :] for m in jnp.split(mod, 6, axis=-1)]

        u = x * (1.0 + scale1) + shift1
        proj = u @ w_in[l]
        q_a, kv_a, k_rope, conv_b, conv_c, conv_x, gate_a, gate_b = jnp.split(proj, SPLIT_POINTS, axis=-1)

        q = (rms_norm(q_a, g_q_a[l]) @ w_q_b[l]).reshape(B, S, N_HEADS, QK_NOPE + QK_ROPE)
        q_nope, q_rope = q[..., :QK_NOPE], apply_rope(q[..., QK_NOPE:], cos_q, sin_q)
        kv = (rms_norm(kv_a, g_kv_a[l]) @ w_kv_b[l]).reshape(B, S, N_HEADS, QK_NOPE + V_HEAD)
        k_nope, v = kv[..., :QK_NOPE], kv[..., QK_NOPE:]
        k_rope = apply_rope(k_rope, cos, sin)
        attn = chunk_causal_mla_attention(q_nope, q_rope, k_nope, k_rope, v)
        y_a = attn.reshape(B, S, N_HEADS * V_HEAD) @ w_o_a[l]

        z = conv_c * conv_x
        y_b = (conv_b * causal_depthwise_conv(z, w_conv[l])) @ w_o_b[l]

        merged = jax.nn.sigmoid(gate_a) * y_a + jax.nn.sigmoid(gate_b) * y_b
        mix_out = merged @ w_o[l]
        x = layer_norm(DEEPNORM_ALPHA * x + gate1 * mix_out, ln1_g[l], ln1_b[l])

        u2 = x * (1.0 + scale2) + shift2
        h_gate, h_up = jnp.split(u2 @ w_ffn_in[l], 2, axis=-1)
        ffn_out = (jax.nn.silu(h_gate) * h_up) @ w_ffn_out[l]
        x = layer_norm(DEEPNORM_ALPHA * x + gate2 * ffn_out, ln2_g[l], ln2_b[l])

    return x
```

```python
import functools

import jax
import jax.numpy as jnp
from jax import lax
from jax.experimental import pallas as pl
from jax.experimental.pallas import tpu as pltpu

F32 = jnp.float32
BF16 = jnp.bfloat16

D_MODEL = 2048
SEQ = 2048
CHUNK = 64
Q_BLOCK = 128
N_HEADS = 16
QK_NOPE = 128
QK_ROPE = 64
V_HEAD = 128
Q_LORA = 512
KV_LORA = 512
ROPE_THETA = 10000.0
ATTN_SCALE = (QK_NOPE + QK_ROPE) ** -0.5
CONV_K = 3
D_FF = 5632
DEPTH = 1
DEEPNORM_ALPHA = (2.0 * DEPTH) ** 0.25
LN_EPS = 1e-5
RMS_EPS = 1e-6

LANES = 128
SUBLANES = 8
HALF = QK_ROPE // 2
HEAD_PAD = N_HEADS * LANES

VMEM_LIMIT = 52 * 1024 * 1024


def _params(n_axes, vmem=VMEM_LIMIT):
    return pltpu.CompilerParams(
        dimension_semantics=("arbitrary",) * n_axes, vmem_limit_bytes=vmem)


def _dot(a, b):
    return jnp.dot(a, b, preferred_element_type=F32)


def _modulate(x, scale, shift):
    return (x * (1.0 + scale) + shift).astype(BF16)


def _layer_norm(y, g, b):
    mu = jnp.mean(y, axis=-1, keepdims=True)
    yc = y - mu
    var = jnp.mean(yc * yc, axis=-1, keepdims=True)
    return yc * lax.rsqrt(var + LN_EPS) * g + b


def _rms_norm(y, g):
    return y * lax.rsqrt(jnp.mean(y * y, axis=-1, keepdims=True) + RMS_EPS) * g


def _ada_kernel(c_ref, w_ref, b_ref, o_ref):
    c = c_ref[...]
    c_act = (c * jax.nn.sigmoid(c)).astype(BF16)
    o_ref[...] = _dot(c_act, w_ref[...].astype(BF16)) + b_ref[...]


def _ada(c_pad, w_ada, b_ada, tn=1024):
    rows, d = c_pad.shape
    n = w_ada.shape[1]
    return pl.pallas_call(
        _ada_kernel,
        grid=(n // tn,),
        in_specs=[pl.BlockSpec((rows, d), lambda j: (0, 0)),
                  pl.BlockSpec((d, tn), lambda j: (0, j)),
                  pl.BlockSpec((1, tn), lambda j: (0, j))],
        out_specs=pl.BlockSpec((rows, tn), lambda j: (0, j)),
        out_shape=jax.ShapeDtypeStruct((rows, n), F32),
        compiler_params=_params(1),
        name="ada_mod",
    )(c_pad, w_ada, b_ada)


def _latent_kernel(x_ref, sc_ref, sh_ref, w_ref, wkr_ref, gq_ref, gkv_ref,
                   qn_ref, kvn_ref, kr_ref):
    u = _modulate(x_ref[...], sc_ref[0], sh_ref[0])
    p = _dot(u, w_ref[...])
    qn_ref[...] = _rms_norm(p[:, :Q_LORA], gq_ref[...]).astype(BF16)
    kvn_ref[...] = _rms_norm(p[:, Q_LORA:], gkv_ref[...]).astype(BF16)
    kr_ref[...] = _dot(u, wkr_ref[...])


def _latent(x2, scale, shift, w_lat, w_kr, g_q, g_kv, bm=512):
    m, d = x2.shape
    per_seq = SEQ // bm
    n_lat = w_lat.shape[1]
    mod_spec = pl.BlockSpec((1, 1, d), lambda i: (i // per_seq, 0, 0))
    return pl.pallas_call(
        _latent_kernel,
        grid=(m // bm,),
        in_specs=[pl.BlockSpec((bm, d), lambda i: (i, 0)),
                  mod_spec, mod_spec,
                  pl.BlockSpec((d, n_lat), lambda i: (0, 0)),
                  pl.BlockSpec((d, 2 * LANES), lambda i: (0, 0)),
                  pl.BlockSpec((1, Q_LORA), lambda i: (0, 0)),
                  pl.BlockSpec((1, KV_LORA), lambda i: (0, 0))],
        out_specs=[pl.BlockSpec((bm, Q_LORA), lambda i: (i, 0)),
                   pl.BlockSpec((bm, KV_LORA), lambda i: (i, 0)),
                   pl.BlockSpec((bm, 2 * LANES), lambda i: (i, 0))],
        out_shape=[jax.ShapeDtypeStruct((m, Q_LORA), BF16),
                   jax.ShapeDtypeStruct((m, KV_LORA), BF16),
                   jax.ShapeDtypeStruct((m, 2 * LANES), F32)],
        compiler_params=_params(1),
        name="latent_proj",
    )(x2, scale, shift, w_lat, w_kr, g_q, g_kv)


def _qkv_up_kernel(qn_ref, kvn_ref, krr_ref, pos_ref, inv_ref, wq_ref, wkv_ref,
                   qnope_ref, qrope_ref, knope_ref, v_ref, kr_ref):
    ang = pos_ref[...].astype(F32) * inv_ref[...]
    cos = jnp.cos(ang)
    sin = jnp.sin(ang)
    q = _dot(qn_ref[...], wq_ref[...])
    qnope_ref[...] = (q[:, :HEAD_PAD] * ATTN_SCALE).astype(BF16)
    for h in range(N_HEADS):
        lo = h * LANES
        r = q[:, HEAD_PAD + lo:HEAD_PAD + lo + LANES]
        rr = q[:, 2 * HEAD_PAD + lo:2 * HEAD_PAD + lo + LANES]
        qrope_ref[:, lo:lo + LANES] = ((r * cos + rr * sin) * ATTN_SCALE).astype(BF16)
    kv = _dot(kvn_ref[...], wkv_ref[...])
    knope_ref[...] = kv[:, :HEAD_PAD].astype(BF16)
    v_ref[...] = kv[:, HEAD_PAD:].astype(BF16)
    krr = krr_ref[...]
    kr_ref[...] = (krr[:, :LANES] * cos + krr[:, LANES:] * sin).astype(BF16)


def _qkv_up(qn, kvn, kr_raw, pos_b, inv_row, w_q, w_kv, bm=256):
    m = qn.shape[0]
    row = lambda n: pl.BlockSpec((bm, n), lambda i: (i, 0))
    const = lambda a: pl.BlockSpec(a.shape, lambda i: (0, 0))
    return pl.pallas_call(
        _qkv_up_kernel,
        grid=(m // bm,),
        in_specs=[row(Q_LORA), row(KV_LORA), row(2 * LANES), row(LANES),
                  const(inv_row), const(w_q), const(w_kv)],
        out_specs=[row(HEAD_PAD), row(HEAD_PAD), row(HEAD_PAD), row(HEAD_PAD), row(LANES)],
        out_shape=[jax.ShapeDtypeStruct((m, HEAD_PAD), BF16)] * 4
        + [jax.ShapeDtypeStruct((m, LANES), BF16)],
        compiler_params=_params(1),
        name="qkv_up",
    )(qn, kvn, kr_raw, pos_b, inv_row, w_q, w_kv)


def _conv_kernel(x_ref, sc_ref, sh_ref, wb_ref, wc_ref, wx_ref, wconv_ref, o_ref,
                 u_ref, halo_ref, *, tiles_per_seq):
    i = pl.program_id(0)
    j = pl.program_id(1)

    @pl.when(j == 0)
    def _():
        u_ref[...] = _modulate(x_ref[...], sc_ref[0], sh_ref[0])

    u = u_ref[...]
    cb = _dot(u, wb_ref[...])
    z = _dot(u, wc_ref[...]) * _dot(u, wx_ref[...])
    bm = z.shape[0]
    @pl.when(i % tiles_per_seq == 0)
    def _():
        halo_ref[j] = jnp.zeros(halo_ref.shape[1:], F32)

    prev = halo_ref[j]
    rows = lax.broadcasted_iota(jnp.int32, z.shape, 0)
    z1 = jnp.where(rows == 0, prev[SUBLANES - 1:SUBLANES, :], pltpu.roll(z, 1, 0))
    z2 = jnp.where(rows == 0, prev[SUBLANES - 2:SUBLANES - 1, :],
                   jnp.where(rows == 1, prev[SUBLANES - 1:SUBLANES, :], pltpu.roll(z, 2, 0)))
    halo_ref[j] = z[bm - SUBLANES:, :]
    w = wconv_ref[...]
    o_ref[...] = (cb * (w[0:1, :] * z2 + w[1:2, :] * z1 + w[2:3, :] * z)).astype(BF16)


def _conv_branch(x2, scale, shift, w_b, w_c, w_x, w_conv, bm=512, tn=512):
    m, d = x2.shape
    n = w_b.shape[1]
    per_seq = SEQ // bm
    mod_spec = pl.BlockSpec((1, 1, d), lambda i, j: (i // per_seq, 0, 0))
    w_spec = pl.BlockSpec((d, tn), lambda i, j: (0, j))
    return pl.pallas_call(
        functools.partial(_conv_kernel, tiles_per_seq=per_seq),
        grid=(m // bm, n // tn),
        in_specs=[pl.BlockSpec((bm, d), lambda i, j: (i, 0)), mod_spec, mod_spec,
                  w_spec, w_spec, w_spec,
                  pl.BlockSpec((CONV_K, tn), lambda i, j: (0, j))],
        out_specs=pl.BlockSpec((bm, tn), lambda i, j: (i, j)),
        out_shape=jax.ShapeDtypeStruct((m, n), BF16),
        scratch_shapes=[pltpu.VMEM((bm, d), BF16),
                        pltpu.VMEM((n // tn, SUBLANES, tn), F32)],
        compiler_params=_params(2),
        name="conv_branch",
    )(x2, scale, shift, w_b, w_c, w_x, w_conv)


def _gate_kernel(x_ref, sc_ref, sh_ref, w_ref, o_ref, u_ref):
    @pl.when(pl.program_id(1) == 0)
    def _():
        u_ref[...] = _modulate(x_ref[...], sc_ref[0], sh_ref[0])

    o_ref[...] = jax.nn.sigmoid(_dot(u_ref[...], w_ref[...])).astype(BF16)


def _gates(x2, scale, shift, w_g, bm=512, tn=1024):
    m, d = x2.shape
    n = w_g.shape[1]
    per_seq = SEQ // bm
    mod_spec = pl.BlockSpec((1, 1, d), lambda i, j: (i // per_seq, 0, 0))
    return pl.pallas_call(
        _gate_kernel,
        grid=(m // bm, n // tn),
        in_specs=[pl.BlockSpec((bm, d), lambda i, j: (i, 0)), mod_spec, mod_spec,
                  pl.BlockSpec((d, tn), lambda i, j: (0, j))],
        out_specs=pl.BlockSpec((bm, tn), lambda i, j: (i, j)),
        out_shape=jax.ShapeDtypeStruct((m, n), BF16),
        scratch_shapes=[pltpu.VMEM((bm, d), BF16)],
        compiler_params=_params(2),
        name="merge_gates",
    )(x2, scale, shift, w_g)


def _attn_kernel(qn_ref, qr_ref, kn_ref, kr_ref, v_ref, o_ref):
    k = jnp.concatenate([kn_ref[...], kr_ref[...]], axis=1)
    v = v_ref[...]
    r = lax.broadcasted_iota(jnp.int32, (Q_BLOCK, Q_BLOCK), 0)
    c = lax.broadcasted_iota(jnp.int32, (Q_BLOCK, Q_BLOCK), 1)
    allowed = (c // CHUNK) <= (r // CHUNK)
    for i in range(SEQ // Q_BLOCK):
        q0 = i * Q_BLOCK
        L = q0 + Q_BLOCK
        q = jnp.concatenate([qn_ref[q0:L, :], qr_ref[q0:L, :]], axis=1)
        s = lax.dot_general(q, k[:L], (((1,), (1,)), ((), ())), preferred_element_type=F32)
        s_diag = jnp.where(allowed, s[:, q0:], -1e30)
        s = s_diag if i == 0 else jnp.concatenate([s[:, :q0], s_diag], axis=1)
        e = jnp.exp(s - jnp.max(s, axis=-1, keepdims=True))
        denom = jnp.sum(e, axis=-1, keepdims=True)
        o = _dot(e.astype(BF16), v[:L])
        o_ref[q0:L, :] = (o / denom).astype(BF16)


def _attention(q_nope, q_rope, k_nope, k_rope, v):
    m = q_nope.shape[0]
    head = pl.BlockSpec((SEQ, LANES), lambda b, h: (b, h))
    return pl.pallas_call(
        _attn_kernel,
        grid=(m // SEQ, N_HEADS),
        in_specs=[head, head, head, pl.BlockSpec((SEQ, LANES), lambda b, h: (b, 0)), head],
        out_specs=head,
        out_shape=jax.ShapeDtypeStruct((m, HEAD_PAD), BF16),
        compiler_params=_params(2),
        name="mla_attention",
    )(q_nope, q_rope, k_nope, k_rope, v)


def _merge_kernel(a_ref, c_ref, wa_ref, wb_ref, ga_ref, gb_ref, o_ref):
    y_a = _dot(a_ref[...], wa_ref[...])
    y_b = _dot(c_ref[...], wb_ref[...])
    o_ref[...] = (ga_ref[...].astype(F32) * y_a + gb_ref[...].astype(F32) * y_b).astype(BF16)


def _merge(attn, cm, w_oa, w_ob, gates, bm=1024, tn=512):
    m, d = attn.shape
    n = w_oa.shape[1]
    nb = n // tn
    lhs = pl.BlockSpec((bm, d), lambda i, j: (i, 0))
    w_spec = pl.BlockSpec((d, tn), lambda i, j: (0, j))
    return pl.pallas_call(
        _merge_kernel,
        grid=(m // bm, nb),
        in_specs=[lhs, lhs, w_spec, w_spec,
                  pl.BlockSpec((bm, tn), lambda i, j: (i, j)),
                  pl.BlockSpec((bm, tn), lambda i, j: (i, j + nb))],
        out_specs=pl.BlockSpec((bm, tn), lambda i, j: (i, j)),
        out_shape=jax.ShapeDtypeStruct((m, n), BF16),
        compiler_params=_params(2),
        name="branch_merge",
    )(attn, cm, w_oa, w_ob, gates, gates)


def _mix_out_kernel(mg_ref, w_ref, x_ref, g1_ref, lg_ref, lb_ref, o_ref):
    mix = _dot(mg_ref[...], w_ref[...])
    y = DEEPNORM_ALPHA * x_ref[...] + g1_ref[0] * mix
    o_ref[...] = _layer_norm(y, lg_ref[...], lb_ref[...])


def _mix_out(merged, w_o, x2, gate1, ln_g, ln_b, bm=512):
    m, d = x2.shape
    per_seq = SEQ // bm
    row = pl.BlockSpec((bm, d), lambda i: (i, 0))
    vec = pl.BlockSpec((1, d), lambda i: (0, 0))
    return pl.pallas_call(
        _mix_out_kernel,
        grid=(m // bm,),
        in_specs=[row, pl.BlockSpec((d, d), lambda i: (0, 0)), row,
                  pl.BlockSpec((1, 1, d), lambda i: (i // per_seq, 0, 0)), vec, vec],
        out_specs=row,
        out_shape=jax.ShapeDtypeStruct((m, d), F32),
        compiler_params=_params(1),
        name="mix_out_ln",
    )(merged, w_o, x2, gate1, ln_g, ln_b)


def _ffn_kernel(x_ref, sc_ref, sh_ref, g2_ref, wg_ref, wu_ref, wo_ref, lg_ref, lb_ref, o_ref,
                u_ref, acc_ref):
    j = pl.program_id(1)

    @pl.when(j == 0)
    def _():
        u_ref[...] = _modulate(x_ref[...], sc_ref[0], sh_ref[0])
        acc_ref[...] = jnp.zeros_like(acc_ref)

    u = u_ref[...]
    hg = _dot(u, wg_ref[...])
    h = (hg * jax.nn.sigmoid(hg) * _dot(u, wu_ref[...])).astype(BF16)
    acc_ref[...] += _dot(h, wo_ref[...])

    @pl.when(j == pl.num_programs(1) - 1)
    def _():
        y = DEEPNORM_ALPHA * x_ref[...] + g2_ref[0] * acc_ref[...]
        o_ref[...] = _layer_norm(y, lg_ref[...], lb_ref[...])


def _ffn(x1, scale, shift, gate, w_in, w_out, ln_g, ln_b, bm=512, tf=512):
    m, d = x1.shape
    nf = D_FF // tf
    per_seq = SEQ // bm
    row = pl.BlockSpec((bm, d), lambda i, j: (i, 0))
    mod_spec = pl.BlockSpec((1, 1, d), lambda i, j: (i // per_seq, 0, 0))
    vec = pl.BlockSpec((1, d), lambda i, j: (0, 0))
    return pl.pallas_call(
        _ffn_kernel,
        grid=(m // bm, nf),
        in_specs=[row, mod_spec, mod_spec, mod_spec,
                  pl.BlockSpec((d, tf), lambda i, j: (0, j)),
                  pl.BlockSpec((d, tf), lambda i, j: (0, j + nf)),
                  pl.BlockSpec((tf, d), lambda i, j: (j, 0)), vec, vec],
        out_specs=row,
        out_shape=jax.ShapeDtypeStruct((m, d), F32),
        scratch_shapes=[pltpu.VMEM((bm, d), BF16), pltpu.VMEM((bm, d), F32)],
        compiler_params=_params(2),
        name="swiglu_ffn_ln",
    )(x1, scale, shift, gate, w_in, w_in, w_out, ln_g, ln_b)


def _rot_cols(w):
    return jnp.concatenate([-w[..., HALF:], w[..., :HALF]], axis=-1)


def _pad_lanes(w):
    return jnp.pad(w, [(0, 0)] * (w.ndim - 1) + [(0, LANES - w.shape[-1])])


def kernel(x, c, positions, w_ada, b_ada, w_in, g_q_a, w_q_b, g_kv_a, w_kv_b, w_o_a,
           w_conv, w_o_b, w_o, ln1_g, ln1_b, w_ffn_in, w_ffn_out, ln2_g, ln2_b):
    bsz, seq, d = x.shape
    assert (seq, d) == (SEQ, D_MODEL) and w_ada.shape[0] == DEPTH
    m = bsz * seq
    x_cur = x.reshape(m, d)

    inv_freq = 1.0 / (ROPE_THETA ** (jnp.arange(0, QK_ROPE, 2, dtype=F32) / QK_ROPE))
    inv_row = jnp.tile(inv_freq, LANES // HALF).reshape(1, LANES)
    pos_b = jnp.broadcast_to(positions.reshape(m, 1), (m, LANES))
    c_pad = jnp.pad(c, ((0, SUBLANES - bsz), (0, 0)))

    for l in range(DEPTH):
        mod = _ada(c_pad, w_ada[l], b_ada[l].reshape(1, -1))[:bsz]
        shift1, scale1, gate1, shift2, scale2, gate2 = [
            t.reshape(bsz, 1, d) for t in jnp.split(mod, 6, axis=-1)]

        wi = w_in[l]
        o_kr = Q_LORA + KV_LORA
        o_cb = o_kr + QK_ROPE
        w_lat = wi[:, :o_kr].astype(BF16)
        w_kr = wi[:, o_kr:o_cb]
        w_kr2 = jnp.concatenate([_pad_lanes(w_kr), _pad_lanes(_rot_cols(w_kr))], axis=1).astype(BF16)
        w_cb = wi[:, o_cb:o_cb + d].astype(BF16)
        w_cc = wi[:, o_cb + d:o_cb + 2 * d].astype(BF16)
        w_cx = wi[:, o_cb + 2 * d:o_cb + 3 * d].astype(BF16)
        w_g = wi[:, o_cb + 3 * d:].astype(BF16)

        wq = w_q_b[l].reshape(Q_LORA, N_HEADS, QK_NOPE + QK_ROPE)
        wq_rope = wq[:, :, QK_NOPE:]
        w_q = jnp.concatenate([
            wq[:, :, :QK_NOPE].reshape(Q_LORA, HEAD_PAD),
            _pad_lanes(wq_rope).reshape(Q_LORA, HEAD_PAD),
            _pad_lanes(_rot_cols(wq_rope)).reshape(Q_LORA, HEAD_PAD)], axis=1).astype(BF16)
        wkv = w_kv_b[l].reshape(KV_LORA, N_HEADS, QK_NOPE + V_HEAD)
        w_kv = jnp.concatenate([wkv[:, :, :QK_NOPE].reshape(KV_LORA, HEAD_PAD),
                                wkv[:, :, QK_NOPE:].reshape(KV_LORA, HEAD_PAD)], axis=1).astype(BF16)

        qn, kvn, kr_raw = _latent(x_cur, scale1, shift1, w_lat, w_kr2,
                                  g_q_a[l].reshape(1, -1), g_kv_a[l].reshape(1, -1))
        q_nope, q_rope, k_nope, v, k_rope = _qkv_up(qn, kvn, kr_raw, pos_b, inv_row, w_q, w_kv)
        attn = _attention(q_nope, q_rope, k_nope, k_rope, v)
        cm = _conv_branch(x_cur, scale1, shift1, w_cb, w_cc, w_cx, w_conv[l])
        gates = _gates(x_cur, scale1, shift1, w_g)
        merged = _merge(attn, cm, w_o_a[l].astype(BF16), w_o_b[l].astype(BF16), gates)
        x_cur = _mix_out(merged, w_o[l].astype(BF16), x_cur, gate1,
                         ln1_g[l].reshape(1, -1), ln1_b[l].reshape(1, -1))

        x_cur = _ffn(x_cur, scale2, shift2, gate2, w_ffn_in[l].astype(BF16),
                     w_ffn_out[l].astype(BF16), ln2_g[l].reshape(1, -1), ln2_b[l].reshape(1, -1))

    return x_cur.reshape(bsz, seq, d)
```

```python
import functools
import math

import jax
import jax.numpy as jnp
from jax import lax
from jax.experimental import pallas as pl
from jax.experimental.pallas import tpu as pltpu

F32 = jnp.float32
BF16 = jnp.bfloat16

D_MODEL = 2048
SEQ = 2048
CHUNK = 64
N_HEADS = 16
QK_NOPE = 128
QK_ROPE = 64
V_HEAD = 128
Q_LORA = 512
KV_LORA = 512
ROPE_THETA = 10000.0
ATTN_SCALE = (QK_NOPE + QK_ROPE) ** -0.5
CONV_K = 3
D_FF = 5632
DEPTH = 1
DEEPNORM_ALPHA = (2.0 * DEPTH) ** 0.25
LN_EPS = 1e-5
RMS_EPS = 1e-6

LANES = 128
SUBLANES = 8
HALF = QK_ROPE // 2
HEAD_PAD = N_HEADS * LANES
W_ROW_ALIGN = QK_ROPE
ATTN_TQ = 256
Q_PRESCALE = ATTN_SCALE * math.log2(math.e)

VMEM_LIMIT = 52 * 1024 * 1024


def _params(n_axes, vmem=VMEM_LIMIT):
    return pltpu.CompilerParams(
        dimension_semantics=("arbitrary",) * n_axes, vmem_limit_bytes=vmem)


def _dot(a, b):
    return jnp.dot(a, b, preferred_element_type=F32)


def _dot_nt(a, b_t):
    return lax.dot_general(a, b_t, (((1,), (1,)), ((), ())), preferred_element_type=F32)


def _modulate(x, scale, shift):
    return (x * (1.0 + scale) + shift).astype(BF16)


def _layer_norm(y, g, b):
    mu = jnp.mean(y, axis=-1, keepdims=True)
    yc = y - mu
    var = jnp.mean(yc * yc, axis=-1, keepdims=True)
    return yc * lax.rsqrt(var + LN_EPS) * g + b


def _rms_norm(y, g):
    return y * lax.rsqrt(jnp.mean(y * y, axis=-1, keepdims=True) + RMS_EPS) * g


def _ada_kernel(c_ref, w_ref, b_ref, o_ref):
    c = c_ref[...]
    c_act = (c * jax.nn.sigmoid(c)).astype(BF16)
    o_ref[...] = _dot(c_act, w_ref[...].astype(BF16)) + b_ref[...]


def _ada(c_pad, w_ada, b_ada, tn=1024):
    rows, d = c_pad.shape
    n = w_ada.shape[1]
    return pl.pallas_call(
        _ada_kernel,
        grid=(n // tn,),
        in_specs=[pl.BlockSpec((rows, d), lambda j: (0, 0)),
                  pl.BlockSpec((d, tn), lambda j: (0, j)),
                  pl.BlockSpec((1, tn), lambda j: (0, j))],
        out_specs=pl.BlockSpec((rows, tn), lambda j: (0, j)),
        out_shape=jax.ShapeDtypeStruct((rows, n), F32),
        compiler_params=_params(1),
        name="ada_mod",
    )(c_pad, w_ada, b_ada)


def _latent_kernel(x_ref, sc_ref, sh_ref, w_ref, wkr_ref, gq_ref, gkv_ref,
                   qn_ref, kvn_ref, kr_ref):
    u = _modulate(x_ref[...], sc_ref[0], sh_ref[0])
    p = _dot_nt(u, w_ref[...])
    qn_ref[...] = _rms_norm(p[:, :Q_LORA], gq_ref[...]).astype(BF16)
    kvn_ref[...] = _rms_norm(p[:, Q_LORA:], gkv_ref[...]).astype(BF16)
    kr_ref[...] = _dot_nt(u, wkr_ref[...])


def _latent(x2, scale, shift, w_t, w_kr_t, g_q, g_kv, bm=512):
    m, d = x2.shape
    per_seq = SEQ // bm
    n_lat = Q_LORA + KV_LORA
    mod_spec = pl.BlockSpec((1, 1, d), lambda i: (i // per_seq, 0, 0))
    return pl.pallas_call(
        _latent_kernel,
        grid=(m // bm,),
        in_specs=[pl.BlockSpec((bm, d), lambda i: (i, 0)),
                  mod_spec, mod_spec,
                  pl.BlockSpec((n_lat, d), lambda i: (0, 0)),
                  pl.BlockSpec((2 * LANES, d), lambda i: (0, 0)),
                  pl.BlockSpec((1, Q_LORA), lambda i: (0, 0)),
                  pl.BlockSpec((1, KV_LORA), lambda i: (0, 0))],
        out_specs=[pl.BlockSpec((bm, Q_LORA), lambda i: (i, 0)),
                   pl.BlockSpec((bm, KV_LORA), lambda i: (i, 0)),
                   pl.BlockSpec((bm, 2 * LANES), lambda i: (i, 0))],
        out_shape=[jax.ShapeDtypeStruct((m, Q_LORA), BF16),
                   jax.ShapeDtypeStruct((m, KV_LORA), BF16),
                   jax.ShapeDtypeStruct((m, 2 * LANES), F32)],
        compiler_params=_params(1),
        name="latent_proj",
    )(x2, scale, shift, w_t, w_kr_t, g_q, g_kv)


def _qkv_up_kernel(qn_ref, kvn_ref, krr_ref, pos_ref, inv_ref, wq_ref, wk_ref, wvt_ref,
                   qnope_ref, qrope_ref, knope_ref, vt_ref, kr_ref):
    ang = pos_ref[...].astype(F32) * inv_ref[...]
    cos = jnp.cos(ang)
    sin = jnp.sin(ang)
    q = _dot(qn_ref[...], wq_ref[...])
    qnope_ref[...] = (q[:, :HEAD_PAD] * Q_PRESCALE).astype(BF16)
    for h in range(N_HEADS):
        lo = h * LANES
        r = q[:, HEAD_PAD + lo:HEAD_PAD + lo + LANES]
        rr = q[:, 2 * HEAD_PAD + lo:2 * HEAD_PAD + lo + LANES]
        qrope_ref[:, lo:lo + LANES] = ((r * cos + rr * sin) * Q_PRESCALE).astype(BF16)
    kvn = kvn_ref[...]
    knope_ref[...] = _dot(kvn, wk_ref[...]).astype(BF16)
    vt_ref[...] = _dot_nt(wvt_ref[...], kvn).astype(BF16)
    krr = krr_ref[...]
    kr_ref[...] = (krr[:, :LANES] * cos + krr[:, LANES:] * sin).astype(BF16)


def _qkv_up(qn, kvn, kr_raw, pos_b, inv_row, w_q, w_k, w_vt, bm=256):
    m = qn.shape[0]
    row = lambda n: pl.BlockSpec((bm, n), lambda i: (i, 0))
    const = lambda a: pl.BlockSpec(a.shape, lambda i: (0, 0))
    return pl.pallas_call(
        _qkv_up_kernel,
        grid=(m // bm,),
        in_specs=[row(Q_LORA), row(KV_LORA), row(2 * LANES), row(LANES),
                  const(inv_row), const(w_q), const(w_k), const(w_vt)],
        out_specs=[row(HEAD_PAD), row(HEAD_PAD), row(HEAD_PAD),
                   pl.BlockSpec((HEAD_PAD, bm), lambda i: (0, i)), row(LANES)],
        out_shape=[jax.ShapeDtypeStruct((m, HEAD_PAD), BF16)] * 3
        + [jax.ShapeDtypeStruct((HEAD_PAD, m), BF16), jax.ShapeDtypeStruct((m, LANES), BF16)],
        compiler_params=_params(1),
        name="qkv_up",
    )(qn, kvn, kr_raw, pos_b, inv_row, w_q, w_k, w_vt)


def _conv_kernel(x_ref, sc_ref, sh_ref, wb_ref, wc_ref, wx_ref, wconv_ref, o_ref,
                 u_ref, halo_ref, *, tiles_per_seq):
    i = pl.program_id(0)
    j = pl.program_id(1)

    @pl.when(j == 0)
    def _():
        u_ref[...] = _modulate(x_ref[...], sc_ref[0], sh_ref[0])

    u = u_ref[...]
    cb = _dot_nt(u, wb_ref[...])
    z = _dot_nt(u, wc_ref[...]) * _dot_nt(u, wx_ref[...])
    bm = z.shape[0]

    @pl.when(i % tiles_per_seq == 0)
    def _():
        halo_ref[j] = jnp.zeros(halo_ref.shape[1:], F32)

    prev = halo_ref[j]
    rows = lax.broadcasted_iota(jnp.int32, z.shape, 0)
    z1 = jnp.where(rows == 0, prev[SUBLANES - 1:SUBLANES, :], pltpu.roll(z, 1, 0))
    z2 = jnp.where(rows == 0, prev[SUBLANES - 2:SUBLANES - 1, :],
                   jnp.where(rows == 1, prev[SUBLANES - 1:SUBLANES, :], pltpu.roll(z, 2, 0)))
    halo_ref[j] = z[bm - SUBLANES:, :]
    w = wconv_ref[...]
    o_ref[...] = (cb * (w[0:1, :] * z2 + w[1:2, :] * z1 + w[2:3, :] * z)).astype(BF16)


def _conv_branch(x2, scale, shift, w_t, row0, w_conv, bm=512, tn=512):
    m, d = x2.shape
    n = w_conv.shape[1]
    per_seq = SEQ // bm
    mod_spec = pl.BlockSpec((1, 1, d), lambda i, j: (i // per_seq, 0, 0))

    def w_spec(first_row):
        return pl.BlockSpec((pl.Element(tn), pl.Element(d)),
                            lambda i, j: (pl.multiple_of(first_row + j * tn, W_ROW_ALIGN), 0))

    return pl.pallas_call(
        functools.partial(_conv_kernel, tiles_per_seq=per_seq),
        grid=(m // bm, n // tn),
        in_specs=[pl.BlockSpec((bm, d), lambda i, j: (i, 0)), mod_spec, mod_spec,
                  w_spec(row0), w_spec(row0 + n), w_spec(row0 + 2 * n),
                  pl.BlockSpec((CONV_K, tn), lambda i, j: (0, j))],
        out_specs=pl.BlockSpec((bm, tn), lambda i, j: (i, j)),
        out_shape=jax.ShapeDtypeStruct((m, n), BF16),
        scratch_shapes=[pltpu.VMEM((bm, d), BF16),
                        pltpu.VMEM((n // tn, SUBLANES, tn), F32)],
        compiler_params=_params(2),
        name="conv_branch",
    )(x2, scale, shift, w_t, w_t, w_t, w_conv)


def _gate_kernel(x_ref, sc_ref, sh_ref, w_ref, o_ref, u_ref):
    @pl.when(pl.program_id(1) == 0)
    def _():
        u_ref[...] = _modulate(x_ref[...], sc_ref[0], sh_ref[0])

    o_ref[...] = jax.nn.sigmoid(_dot_nt(u_ref[...], w_ref[...])).astype(BF16)


def _gates(x2, scale, shift, w_t, row0, n, bm=512, tn=1024):
    m, d = x2.shape
    per_seq = SEQ // bm
    mod_spec = pl.BlockSpec((1, 1, d), lambda i, j: (i // per_seq, 0, 0))
    return pl.pallas_call(
        _gate_kernel,
        grid=(m // bm, n // tn),
        in_specs=[pl.BlockSpec((bm, d), lambda i, j: (i, 0)), mod_spec, mod_spec,
                  pl.BlockSpec((pl.Element(tn), pl.Element(d)),
                               lambda i, j: (pl.multiple_of(row0 + j * tn, W_ROW_ALIGN), 0))],
        out_specs=pl.BlockSpec((bm, tn), lambda i, j: (i, j)),
        out_shape=jax.ShapeDtypeStruct((m, n), BF16),
        scratch_shapes=[pltpu.VMEM((bm, d), BF16)],
        compiler_params=_params(2),
        name="merge_gates",
    )(x2, scale, shift, w_t)


def _attn_kernel(qn_ref, qr_ref, kn_ref, kr_ref, vt_ref, o_ref):
    k = jnp.concatenate([kn_ref[...], kr_ref[...]], axis=1)
    key = lax.broadcasted_iota(jnp.int32, (ATTN_TQ, ATTN_TQ), 0)
    qry = lax.broadcasted_iota(jnp.int32, (ATTN_TQ, ATTN_TQ), 1)
    allowed = (key // CHUNK) <= (qry // CHUNK)
    n_blk = SEQ // ATTN_TQ

    def scores(i):
        q0 = i * ATTN_TQ
        L = q0 + ATTN_TQ
        q = jnp.concatenate([qn_ref[q0:L, :], qr_ref[q0:L, :]], axis=1)
        return _dot_nt(k[:L], q)

    s_next = scores(0)
    for i in range(n_blk):
        q0 = i * ATTN_TQ
        L = q0 + ATTN_TQ
        s_t = s_next
        if i + 1 < n_blk:
            s_next = scores(i + 1)
        s_diag = jnp.where(allowed, s_t[q0:, :], -1e30)
        s_t = s_diag if i == 0 else jnp.concatenate([s_t[:q0, :], s_diag], axis=0)
        e = jnp.exp2(s_t - jnp.max(s_t, axis=0, keepdims=True))
        denom = jnp.sum(e, axis=0, keepdims=True)
        o_t = _dot(vt_ref[:, :L], e.astype(BF16))
        o_ref[q0:L, :] = (o_t / denom).T.astype(BF16)


def _attention(q_nope, q_rope, k_nope, k_rope, v_t):
    m = q_nope.shape[0]
    head = pl.BlockSpec((SEQ, LANES), lambda b, h: (b, h))
    return pl.pallas_call(
        _attn_kernel,
        grid=(m // SEQ, N_HEADS),
        in_specs=[head, head, head, pl.BlockSpec((SEQ, LANES), lambda b, h: (b, 0)),
                  pl.BlockSpec((V_HEAD, SEQ), lambda b, h: (h, b))],
        out_specs=head,
        out_shape=jax.ShapeDtypeStruct((m, HEAD_PAD), BF16),
        compiler_params=_params(2),
        name="mla_attention",
    )(q_nope, q_rope, k_nope, k_rope, v_t)


def _merge_kernel(a_ref, c_ref, wa_ref, wb_ref, ga_ref, gb_ref, o_ref):
    y_a = _dot(a_ref[...], wa_ref[...])
    y_b = _dot(c_ref[...], wb_ref[...])
    o_ref[...] = (ga_ref[...].astype(F32) * y_a + gb_ref[...].astype(F32) * y_b).astype(BF16)


def _merge(attn, cm, w_oa, w_ob, gates, bm=1024, tn=512):
    m, d = attn.shape
    n = w_oa.shape[1]
    nb = n // tn
    lhs = pl.BlockSpec((bm, d), lambda i, j: (i, 0))
    w_spec = pl.BlockSpec((d, tn), lambda i, j: (0, j))
    return pl.pallas_call(
        _merge_kernel,
        grid=(m // bm, nb),
        in_specs=[lhs, lhs, w_spec, w_spec,
                  pl.BlockSpec((bm, tn), lambda i, j: (i, j)),
                  pl.BlockSpec((bm, tn), lambda i, j: (i, j + nb))],
        out_specs=pl.BlockSpec((bm, tn), lambda i, j: (i, j)),
        out_shape=jax.ShapeDtypeStruct((m, n), BF16),
        compiler_params=_params(2),
        name="branch_merge",
    )(attn, cm, w_oa, w_ob, gates, gates)


def _mix_out_kernel(mg_ref, w_ref, x_ref, g1_ref, lg_ref, lb_ref, o_ref):
    mix = _dot(mg_ref[...], w_ref[...])
    y = DEEPNORM_ALPHA * x_ref[...] + g1_ref[0] * mix
    o_ref[...] = _layer_norm(y, lg_ref[...], lb_ref[...])


def _mix_out(merged, w_o, x2, gate1, ln_g, ln_b, bm=512):
    m, d = x2.shape
    per_seq = SEQ // bm
    row = pl.BlockSpec((bm, d), lambda i: (i, 0))
    vec = pl.BlockSpec((1, d), lambda i: (0, 0))
    return pl.pallas_call(
        _mix_out_kernel,
        grid=(m // bm,),
        in_specs=[row, pl.BlockSpec((d, d), lambda i: (0, 0)), row,
                  pl.BlockSpec((1, 1, d), lambda i: (i // per_seq, 0, 0)), vec, vec],
        out_specs=row,
        out_shape=jax.ShapeDtypeStruct((m, d), F32),
        compiler_params=_params(1),
        name="mix_out_ln",
    )(merged, w_o, x2, gate1, ln_g, ln_b)


def _ffn_kernel(x_ref, sc_ref, sh_ref, g2_ref, wg_ref, wu_ref, wo_ref, lg_ref, lb_ref, o_ref,
                u_ref, acc_ref):
    j = pl.program_id(1)

    @pl.when(j == 0)
    def _():
        u_ref[...] = _modulate(x_ref[...], sc_ref[0], sh_ref[0])
        acc_ref[...] = jnp.zeros_like(acc_ref)

    u = u_ref[...]
    hg = _dot(u, wg_ref[...])
    h = (hg * jax.nn.sigmoid(hg) * _dot(u, wu_ref[...])).astype(BF16)
    acc_ref[...] += _dot(h, wo_ref[...])

    @pl.when(j == pl.num_programs(1) - 1)
    def _():
        y = DEEPNORM_ALPHA * x_ref[...] + g2_ref[0] * acc_ref[...]
        o_ref[...] = _layer_norm(y, lg_ref[...], lb_ref[...])


def _ffn(x1, scale, shift, gate, w_in, w_out, ln_g, ln_b, bm=512, tf=512):
    m, d = x1.shape
    nf = D_FF // tf
    per_seq = SEQ // bm
    row = pl.BlockSpec((bm, d), lambda i, j: (i, 0))
    mod_spec = pl.BlockSpec((1, 1, d), lambda i, j: (i // per_seq, 0, 0))
    vec = pl.BlockSpec((1, d), lambda i, j: (0, 0))
    return pl.pallas_call(
        _ffn_kernel,
        grid=(m // bm, nf),
        in_specs=[row, mod_spec, mod_spec, mod_spec,
                  pl.BlockSpec((d, tf), lambda i, j: (0, j)),
                  pl.BlockSpec((d, tf), lambda i, j: (0, j + nf)),
                  pl.BlockSpec((tf, d), lambda i, j: (j, 0)), vec, vec],
        out_specs=row,
        out_shape=jax.ShapeDtypeStruct((m, d), F32),
        scratch_shapes=[pltpu.VMEM((bm, d), BF16), pltpu.VMEM((bm, d), F32)],
        compiler_params=_params(2),
        name="swiglu_ffn_ln",
    )(x1, scale, shift, gate, w_in, w_in, w_out, ln_g, ln_b)


def _rot_last(w):
    return jnp.concatenate([-w[..., HALF:], w[..., :HALF]], axis=-1)


def _pad_lanes(w):
    return jnp.pad(w, [(0, 0)] * (w.ndim - 1) + [(0, LANES - w.shape[-1])])


def kernel(x, c, positions, w_ada, b_ada, w_in, g_q_a, w_q_b, g_kv_a, w_kv_b, w_o_a,
           w_conv, w_o_b, w_o, ln1_g, ln1_b, w_ffn_in, w_ffn_out, ln2_g, ln2_b):
    bsz, seq, d = x.shape
    assert (seq, d) == (SEQ, D_MODEL) and w_ada.shape[0] == DEPTH
    m = bsz * seq
    x_cur = x.reshape(m, d)

    inv_freq = 1.0 / (ROPE_THETA ** (jnp.arange(0, QK_ROPE, 2, dtype=F32) / QK_ROPE))
    inv_row = jnp.tile(inv_freq, LANES // HALF).reshape(1, LANES)
    pos_b = jnp.broadcast_to(positions.reshape(m, 1), (m, LANES))
    c_pad = jnp.pad(c, ((0, SUBLANES - bsz), (0, 0)))

    for l in range(DEPTH):
        mod = _ada(c_pad, w_ada[l], b_ada[l].reshape(1, -1))[:bsz]
        shift1, scale1, gate1, shift2, scale2, gate2 = [
            t.reshape(bsz, 1, d) for t in jnp.split(mod, 6, axis=-1)]

        o_kr = Q_LORA + KV_LORA
        o_cb = o_kr + QK_ROPE
        w_t = jnp.swapaxes(w_in[l], 0, 1).astype(BF16)
        w_kr = w_in[l][:, o_kr:o_cb]
        w_kr_t = jnp.swapaxes(jnp.concatenate(
            [_pad_lanes(w_kr), _pad_lanes(_rot_last(w_kr))], axis=1), 0, 1).astype(BF16)

        wq = w_q_b[l].reshape(Q_LORA, N_HEADS, QK_NOPE + QK_ROPE)
        wq_rope = wq[:, :, QK_NOPE:]
        w_q = jnp.concatenate([
            wq[:, :, :QK_NOPE].reshape(Q_LORA, HEAD_PAD),
            _pad_lanes(wq_rope).reshape(Q_LORA, HEAD_PAD),
            _pad_lanes(_rot_last(wq_rope)).reshape(Q_LORA, HEAD_PAD)], axis=1).astype(BF16)
        wkv = w_kv_b[l].reshape(KV_LORA, N_HEADS, QK_NOPE + V_HEAD)
        w_k = wkv[:, :, :QK_NOPE].reshape(KV_LORA, HEAD_PAD).astype(BF16)
        w_vt = jnp.swapaxes(wkv[:, :, QK_NOPE:].reshape(KV_LORA, HEAD_PAD), 0, 1).astype(BF16)

        qn, kvn, kr_raw = _latent(x_cur, scale1, shift1, w_t, w_kr_t,
                                  g_q_a[l].reshape(1, -1), g_kv_a[l].reshape(1, -1))
        q_nope, q_rope, k_nope, v_t, k_rope = _qkv_up(qn, kvn, kr_raw, pos_b, inv_row,
                                                      w_q, w_k, w_vt)
        attn = _attention(q_nope, q_rope, k_nope, k_rope, v_t)
        cm = _conv_branch(x_cur, scale1, shift1, w_t, o_cb, w_conv[l])
        gates = _gates(x_cur, scale1, shift1, w_t, o_cb + 3 * d, 2 * d)
        merged = _merge(attn, cm, w_o_a[l].astype(BF16), w_o_b[l].astype(BF16), gates)
        x_cur = _mix_out(merged, w_o[l].astype(BF16), x_cur, gate1,
                         ln1_g[l].reshape(1, -1), ln1_b[l].reshape(1, -1))

        x_cur = _ffn(x_cur, scale2, shift2, gate2, w_ffn_in[l].astype(BF16),
                     w_ffn_out[l].astype(BF16), ln2_g[l].reshape(1, -1), ln2_b[l].reshape(1, -1))

    return x_cur.reshape(bsz, seq, d)
```

```python
import functools
import math

import jax
import jax.numpy as jnp
from jax import lax
from jax.experimental import pallas as pl
from jax.experimental.pallas import tpu as pltpu

F32 = jnp.float32
BF16 = jnp.bfloat16

D_MODEL = 2048
SEQ = 2048
CHUNK = 64
N_HEADS = 16
QK_NOPE = 128
QK_ROPE = 64
V_HEAD = 128
Q_LORA = 512
KV_LORA = 512
ROPE_THETA = 10000.0
ATTN_SCALE = (QK_NOPE + QK_ROPE) ** -0.5
CONV_K = 3
D_FF = 5632
DEPTH = 1
DEEPNORM_ALPHA = (2.0 * DEPTH) ** 0.25
LN_EPS = 1e-5
RMS_EPS = 1e-6

LANES = 128
SUBLANES = 8
HALF = QK_ROPE // 2
HEAD_PAD = N_HEADS * LANES
W_ROW_ALIGN = QK_ROPE
ATTN_TQ = 256
UP_HEAD_GROUP = 4
ROW_CHUNK = 256
Q_PRESCALE = ATTN_SCALE * math.log2(math.e)

VMEM_LIMIT = 52 * 1024 * 1024


def _params(n_axes, vmem=VMEM_LIMIT):
    return pltpu.CompilerParams(
        dimension_semantics=("arbitrary",) * n_axes, vmem_limit_bytes=vmem)


def _dot(a, b):
    return jnp.dot(a, b, preferred_element_type=F32)


def _dot_nt(a, b_t):
    return lax.dot_general(a, b_t, (((1,), (1,)), ((), ())), preferred_element_type=F32)


def _modulate(x, scale, shift):
    return (x * (1.0 + scale) + shift).astype(BF16)


def _layer_norm(y, g, b):
    mu = jnp.mean(y, axis=-1, keepdims=True)
    yc = y - mu
    var = jnp.mean(yc * yc, axis=-1, keepdims=True)
    return yc * lax.rsqrt(var + LN_EPS) * g + b


def _rms_norm(y, g):
    return y * lax.rsqrt(jnp.mean(y * y, axis=-1, keepdims=True) + RMS_EPS) * g


def _ada_kernel(c_ref, w_ref, b_ref, o_ref):
    c = c_ref[...]
    c_act = (c * jax.nn.sigmoid(c)).astype(BF16)
    o_ref[...] = _dot(c_act, w_ref[...].astype(BF16)) + b_ref[...]


def _ada(c_pad, w_ada, b_ada, tn=1024):
    rows, d = c_pad.shape
    n = w_ada.shape[1]
    return pl.pallas_call(
        _ada_kernel,
        grid=(n // tn,),
        in_specs=[pl.BlockSpec((rows, d), lambda j: (0, 0)),
                  pl.BlockSpec((d, tn), lambda j: (0, j)),
                  pl.BlockSpec((1, tn), lambda j: (0, j))],
        out_specs=pl.BlockSpec((rows, tn), lambda j: (0, j)),
        out_shape=jax.ShapeDtypeStruct((rows, n), F32),
        compiler_params=_params(1),
        name="ada_mod",
    )(c_pad, w_ada, b_ada)


def _latent_kernel(x_ref, sc_ref, sh_ref, w_ref, wkr_ref, gq_ref, gkv_ref,
                   qn_ref, kvn_ref, kr_ref):
    u = _modulate(x_ref[...], sc_ref[0], sh_ref[0])
    p = _dot_nt(u, w_ref[...])
    qn_ref[...] = _rms_norm(p[:, :Q_LORA], gq_ref[...]).astype(BF16)
    kvn_ref[...] = _rms_norm(p[:, Q_LORA:], gkv_ref[...]).astype(BF16)
    kr_ref[...] = _dot_nt(u, wkr_ref[...])


def _latent(x2, scale, shift, w_t, w_kr_t, g_q, g_kv, bm=512):
    m, d = x2.shape
    per_seq = SEQ // bm
    n_lat = Q_LORA + KV_LORA
    mod_spec = pl.BlockSpec((1, 1, d), lambda i: (i // per_seq, 0, 0))
    return pl.pallas_call(
        _latent_kernel,
        grid=(m // bm,),
        in_specs=[pl.BlockSpec((bm, d), lambda i: (i, 0)),
                  mod_spec, mod_spec,
                  pl.BlockSpec((n_lat, d), lambda i: (0, 0)),
                  pl.BlockSpec((2 * LANES, d), lambda i: (0, 0)),
                  pl.BlockSpec((1, Q_LORA), lambda i: (0, 0)),
                  pl.BlockSpec((1, KV_LORA), lambda i: (0, 0))],
        out_specs=[pl.BlockSpec((bm, Q_LORA), lambda i: (i, 0)),
                   pl.BlockSpec((bm, KV_LORA), lambda i: (i, 0)),
                   pl.BlockSpec((bm, 2 * LANES), lambda i: (i, 0))],
        out_shape=[jax.ShapeDtypeStruct((m, Q_LORA), BF16),
                   jax.ShapeDtypeStruct((m, KV_LORA), BF16),
                   jax.ShapeDtypeStruct((m, 2 * LANES), F32)],
        compiler_params=_params(1),
        name="latent_proj",
    )(x2, scale, shift, w_t, w_kr_t, g_q, g_kv)


def _qkv_up_kernel(qn_ref, kvn_ref, krr_ref, pos_ref, inv_ref, wq_ref, wk_ref, wvt_ref,
                   qnope_ref, qrope_ref, knope_ref, vt_ref, kr_ref):
    ang = pos_ref[...].astype(F32) * inv_ref[...]
    cos = jnp.cos(ang)
    sin = jnp.sin(ang)
    qn = qn_ref[...]
    kvn = kvn_ref[...]
    grp = UP_HEAD_GROUP * LANES
    cos_g = jnp.concatenate([cos * Q_PRESCALE] * UP_HEAD_GROUP, axis=1)
    sin_g = jnp.concatenate([sin * Q_PRESCALE] * UP_HEAD_GROUP, axis=1)
    for g in range(N_HEADS // UP_HEAD_GROUP):
        lo, hi = g * grp, (g + 1) * grp
        qnope_ref[:, lo:hi] = (_dot(qn, wq_ref[:, lo:hi]) * Q_PRESCALE).astype(BF16)
        r = _dot(qn, wq_ref[:, HEAD_PAD + lo:HEAD_PAD + hi])
        rr = _dot(qn, wq_ref[:, 2 * HEAD_PAD + lo:2 * HEAD_PAD + hi])
        qrope_ref[:, lo:hi] = (r * cos_g + rr * sin_g).astype(BF16)
        knope_ref[:, lo:hi] = _dot(kvn, wk_ref[:, lo:hi]).astype(BF16)
        vt_ref[lo:hi, :] = _dot_nt(wvt_ref[lo:hi, :], kvn).astype(BF16)
    krr = krr_ref[...]
    kr_ref[...] = (krr[:, :LANES] * cos + krr[:, LANES:] * sin).astype(BF16)


def _qkv_up(qn, kvn, kr_raw, pos_b, inv_row, w_q, w_k, w_vt, bm=512):
    m = qn.shape[0]
    row = lambda n: pl.BlockSpec((bm, n), lambda i: (i, 0))
    const = lambda a: pl.BlockSpec(a.shape, lambda i: (0, 0))
    return pl.pallas_call(
        _qkv_up_kernel,
        grid=(m // bm,),
        in_specs=[row(Q_LORA), row(KV_LORA), row(2 * LANES), row(LANES),
                  const(inv_row), const(w_q), const(w_k), const(w_vt)],
        out_specs=[row(HEAD_PAD), row(HEAD_PAD), row(HEAD_PAD),
                   pl.BlockSpec((HEAD_PAD, bm), lambda i: (0, i)), row(LANES)],
        out_shape=[jax.ShapeDtypeStruct((m, HEAD_PAD), BF16)] * 3
        + [jax.ShapeDtypeStruct((HEAD_PAD, m), BF16), jax.ShapeDtypeStruct((m, LANES), BF16)],
        compiler_params=_params(1),
        name="qkv_up",
    )(qn, kvn, kr_raw, pos_b, inv_row, w_q, w_k, w_vt)


def _conv_kernel(x_ref, sc_ref, sh_ref, wb_ref, wc_ref, wx_ref, wconv_ref, o_ref,
                 u_ref, halo_ref, *, tiles_per_seq):
    i = pl.program_id(0)
    j = pl.program_id(1)

    @pl.when(j == 0)
    def _():
        u_ref[...] = _modulate(x_ref[...], sc_ref[0], sh_ref[0])

    u = u_ref[...]
    cb = _dot_nt(u, wb_ref[...])
    z = _dot_nt(u, wc_ref[...]) * _dot_nt(u, wx_ref[...])
    bm = z.shape[0]

    @pl.when(i % tiles_per_seq == 0)
    def _():
        halo_ref[j] = jnp.zeros(halo_ref.shape[1:], F32)

    prev = halo_ref[j]
    rows = lax.broadcasted_iota(jnp.int32, z.shape, 0)
    z1 = jnp.where(rows == 0, prev[SUBLANES - 1:SUBLANES, :], pltpu.roll(z, 1, 0))
    z2 = jnp.where(rows == 0, prev[SUBLANES - 2:SUBLANES - 1, :],
                   jnp.where(rows == 1, prev[SUBLANES - 1:SUBLANES, :], pltpu.roll(z, 2, 0)))
    halo_ref[j] = z[bm - SUBLANES:, :]
    w = wconv_ref[...]
    o_ref[...] = (cb * (w[0:1, :] * z2 + w[1:2, :] * z1 + w[2:3, :] * z)).astype(BF16)


def _conv_branch(x2, scale, shift, w_t, row0, w_conv, bm=1024, tn=512):
    m, d = x2.shape
    n = w_conv.shape[1]
    per_seq = SEQ // bm
    mod_spec = pl.BlockSpec((1, 1, d), lambda i, j: (i // per_seq, 0, 0))

    def w_spec(first_row):
        return pl.BlockSpec((pl.Element(tn), pl.Element(d)),
                            lambda i, j: (pl.multiple_of(first_row + j * tn, W_ROW_ALIGN), 0))

    return pl.pallas_call(
        functools.partial(_conv_kernel, tiles_per_seq=per_seq),
        grid=(m // bm, n // tn),
        in_specs=[pl.BlockSpec((bm, d), lambda i, j: (i, 0)), mod_spec, mod_spec,
                  w_spec(row0), w_spec(row0 + n), w_spec(row0 + 2 * n),
                  pl.BlockSpec((CONV_K, tn), lambda i, j: (0, j))],
        out_specs=pl.BlockSpec((bm, tn), lambda i, j: (i, j)),
        out_shape=jax.ShapeDtypeStruct((m, n), BF16),
        scratch_shapes=[pltpu.VMEM((bm, d), BF16),
                        pltpu.VMEM((n // tn, SUBLANES, tn), F32)],
        compiler_params=_params(2),
        name="conv_branch",
    )(x2, scale, shift, w_t, w_t, w_t, w_conv)


def _gate_kernel(x_ref, sc_ref, sh_ref, w_ref, o_ref, u_ref):
    @pl.when(pl.program_id(1) == 0)
    def _():
        u_ref[...] = _modulate(x_ref[...], sc_ref[0], sh_ref[0])

    o_ref[...] = jax.nn.sigmoid(_dot_nt(u_ref[...], w_ref[...])).astype(BF16)


def _gates(x2, scale, shift, w_t, row0, n, bm=1024, tn=1024):
    m, d = x2.shape
    per_seq = SEQ // bm
    mod_spec = pl.BlockSpec((1, 1, d), lambda i, j: (i // per_seq, 0, 0))
    return pl.pallas_call(
        _gate_kernel,
        grid=(m // bm, n // tn),
        in_specs=[pl.BlockSpec((bm, d), lambda i, j: (i, 0)), mod_spec, mod_spec,
                  pl.BlockSpec((pl.Element(tn), pl.Element(d)),
                               lambda i, j: (pl.multiple_of(row0 + j * tn, W_ROW_ALIGN), 0))],
        out_specs=pl.BlockSpec((bm, tn), lambda i, j: (i, j)),
        out_shape=jax.ShapeDtypeStruct((m, n), BF16),
        scratch_shapes=[pltpu.VMEM((bm, d), BF16)],
        compiler_params=_params(2),
        name="merge_gates",
    )(x2, scale, shift, w_t)


def _attn_kernel(qn_ref, qr_ref, kn_ref, kr_ref, vt_ref, o_ref):
    k = jnp.concatenate([kn_ref[...], kr_ref[...]], axis=1)
    key = lax.broadcasted_iota(jnp.int32, (ATTN_TQ, ATTN_TQ), 0)
    qry = lax.broadcasted_iota(jnp.int32, (ATTN_TQ, ATTN_TQ), 1)
    allowed = (key // CHUNK) <= (qry // CHUNK)
    n_blk = SEQ // ATTN_TQ

    def scores(i):
        q0 = i * ATTN_TQ
        L = q0 + ATTN_TQ
        q = jnp.concatenate([qn_ref[q0:L, :], qr_ref[q0:L, :]], axis=1)
        return _dot_nt(k[:L], q)

    s_next = scores(0)
    for i in range(n_blk):
        q0 = i * ATTN_TQ
        L = q0 + ATTN_TQ
        s_t = s_next
        if i + 1 < n_blk:
            s_next = scores(i + 1)
        s_diag = jnp.where(allowed, s_t[q0:, :], -1e30)
        s_t = s_diag if i == 0 else jnp.concatenate([s_t[:q0, :], s_diag], axis=0)
        e = jnp.exp2(s_t - jnp.max(s_t, axis=0, keepdims=True))
        denom = jnp.sum(e, axis=0, keepdims=True)
        o_t = _dot(vt_ref[:, :L], e.astype(BF16))
        o_ref[q0:L, :] = (o_t / denom).T.astype(BF16)


def _attention(q_nope, q_rope, k_nope, k_rope, v_t):
    m = q_nope.shape[0]
    head = pl.BlockSpec((SEQ, LANES), lambda b, h: (b, h))
    return pl.pallas_call(
        _attn_kernel,
        grid=(m // SEQ, N_HEADS),
        in_specs=[head, head, head, pl.BlockSpec((SEQ, LANES), lambda b, h: (b, 0)),
                  pl.BlockSpec((V_HEAD, SEQ), lambda b, h: (h, b))],
        out_specs=head,
        out_shape=jax.ShapeDtypeStruct((m, HEAD_PAD), BF16),
        compiler_params=_params(2),
        name="mla_attention",
    )(q_nope, q_rope, k_nope, k_rope, v_t)


def _merge_kernel(a_ref, c_ref, wa_ref, wb_ref, ga_ref, gb_ref, o_ref):
    y_a = _dot(a_ref[...], wa_ref[...])
    y_b = _dot(c_ref[...], wb_ref[...])
    o_ref[...] = (ga_ref[...].astype(F32) * y_a + gb_ref[...].astype(F32) * y_b).astype(BF16)


def _merge(attn, cm, w_oa, w_ob, gates, bm=1024, tn=512):
    m, d = attn.shape
    n = w_oa.shape[1]
    nb = n // tn
    lhs = pl.BlockSpec((bm, d), lambda i, j: (i, 0))
    w_spec = pl.BlockSpec((d, tn), lambda i, j: (0, j))
    return pl.pallas_call(
        _merge_kernel,
        grid=(m // bm, nb),
        in_specs=[lhs, lhs, w_spec, w_spec,
                  pl.BlockSpec((bm, tn), lambda i, j: (i, j)),
                  pl.BlockSpec((bm, tn), lambda i, j: (i, j + nb))],
        out_specs=pl.BlockSpec((bm, tn), lambda i, j: (i, j)),
        out_shape=jax.ShapeDtypeStruct((m, n), BF16),
        compiler_params=_params(2),
        name="branch_merge",
    )(attn, cm, w_oa, w_ob, gates, gates)


def _mix_out_kernel(mg_ref, w_ref, x_ref, g1_ref, lg_ref, lb_ref, o_ref):
    n_chunks = mg_ref.shape[0] // ROW_CHUNK
    mix = _dot(mg_ref[0:ROW_CHUNK, :], w_ref[...])
    for r in range(n_chunks):
        rows = slice(r * ROW_CHUNK, (r + 1) * ROW_CHUNK)
        cur = mix
        if r + 1 < n_chunks:
            mix = _dot(mg_ref[(r + 1) * ROW_CHUNK:(r + 2) * ROW_CHUNK, :], w_ref[...])
        y = DEEPNORM_ALPHA * x_ref[rows, :] + g1_ref[0] * cur
        o_ref[rows, :] = _layer_norm(y, lg_ref[...], lb_ref[...])


def _mix_out(merged, w_o, x2, gate1, ln_g, ln_b, bm=512):
    m, d = x2.shape
    per_seq = SEQ // bm
    row = pl.BlockSpec((bm, d), lambda i: (i, 0))
    vec = pl.BlockSpec((1, d), lambda i: (0, 0))
    return pl.pallas_call(
        _mix_out_kernel,
        grid=(m // bm,),
        in_specs=[row, pl.BlockSpec((d, d), lambda i: (0, 0)), row,
                  pl.BlockSpec((1, 1, d), lambda i: (i // per_seq, 0, 0)), vec, vec],
        out_specs=row,
        out_shape=jax.ShapeDtypeStruct((m, d), F32),
        compiler_params=_params(1),
        name="mix_out_ln",
    )(merged, w_o, x2, gate1, ln_g, ln_b)


def _ffn_kernel(x_ref, sc_ref, sh_ref, g2_ref, wg_ref, wu_ref, wo_ref, lg_ref, lb_ref, o_ref,
                u_ref, acc_ref):
    j = pl.program_id(1)

    @pl.when(j == 0)
    def _():
        u_ref[...] = _modulate(x_ref[...], sc_ref[0], sh_ref[0])
        acc_ref[...] = jnp.zeros_like(acc_ref)

    u = u_ref[...]
    hg = _dot(u, wg_ref[...])
    h = (hg * jax.nn.sigmoid(hg) * _dot(u, wu_ref[...])).astype(BF16)
    acc_ref[...] += _dot(h, wo_ref[...])

    @pl.when(j == pl.num_programs(1) - 1)
    def _():
        y = DEEPNORM_ALPHA * x_ref[...] + g2_ref[0] * acc_ref[...]
        o_ref[...] = _layer_norm(y, lg_ref[...], lb_ref[...])


def _ffn(x1, scale, shift, gate, w_in, w_out, ln_g, ln_b, bm=512, tf=512):
    m, d = x1.shape
    nf = D_FF // tf
    per_seq = SEQ // bm
    row = pl.BlockSpec((bm, d), lambda i, j: (i, 0))
    mod_spec = pl.BlockSpec((1, 1, d), lambda i, j: (i // per_seq, 0, 0))
    vec = pl.BlockSpec((1, d), lambda i, j: (0, 0))
    return pl.pallas_call(
        _ffn_kernel,
        grid=(m // bm, nf),
        in_specs=[row, mod_spec, mod_spec, mod_spec,
                  pl.BlockSpec((d, tf), lambda i, j: (0, j)),
                  pl.BlockSpec((d, tf), lambda i, j: (0, j + nf)),
                  pl.BlockSpec((tf, d), lambda i, j: (j, 0)), vec, vec],
        out_specs=row,
        out_shape=jax.ShapeDtypeStruct((m, d), F32),
        scratch_shapes=[pltpu.VMEM((bm, d), BF16), pltpu.VMEM((bm, d), F32)],
        compiler_params=_params(2),
        name="swiglu_ffn_ln",
    )(x1, scale, shift, gate, w_in, w_in, w_out, ln_g, ln_b)


def _rot_last(w):
    return jnp.concatenate([-w[..., HALF:], w[..., :HALF]], axis=-1)


def _pad_lanes(w):
    return jnp.pad(w, [(0, 0)] * (w.ndim - 1) + [(0, LANES - w.shape[-1])])


def kernel(x, c, positions, w_ada, b_ada, w_in, g_q_a, w_q_b, g_kv_a, w_kv_b, w_o_a,
           w_conv, w_o_b, w_o, ln1_g, ln1_b, w_ffn_in, w_ffn_out, ln2_g, ln2_b):
    bsz, seq, d = x.shape
    assert (seq, d) == (SEQ, D_MODEL) and w_ada.shape[0] == DEPTH
    m = bsz * seq
    x_cur = x.reshape(m, d)

    inv_freq = 1.0 / (ROPE_THETA ** (jnp.arange(0, QK_ROPE, 2, dtype=F32) / QK_ROPE))
    inv_row = jnp.tile(inv_freq, LANES // HALF).reshape(1, LANES)
    pos_b = jnp.broadcast_to(positions.reshape(m, 1), (m, LANES))
    c_pad = jnp.pad(c, ((0, SUBLANES - bsz), (0, 0)))

    for l in range(DEPTH):
        mod = _ada(c_pad, w_ada[l], b_ada[l].reshape(1, -1))[:bsz]
        shift1, scale1, gate1, shift2, scale2, gate2 = [
            t.reshape(bsz, 1, d) for t in jnp.split(mod, 6, axis=-1)]

        o_kr = Q_LORA + KV_LORA
        o_cb = o_kr + QK_ROPE
        w_t = jnp.swapaxes(w_in[l], 0, 1).astype(BF16)
        w_kr = w_in[l][:, o_kr:o_cb]
        w_kr_t = jnp.swapaxes(jnp.concatenate(
            [_pad_lanes(w_kr), _pad_lanes(_rot_last(w_kr))], axis=1), 0, 1).astype(BF16)

        wq = w_q_b[l].reshape(Q_LORA, N_HEADS, QK_NOPE + QK_ROPE)
        wq_rope = wq[:, :, QK_NOPE:]
        w_q = jnp.concatenate([
            wq[:, :, :QK_NOPE].reshape(Q_LORA, HEAD_PAD),
            _pad_lanes(wq_rope).reshape(Q_LORA, HEAD_PAD),
            _pad_lanes(_rot_last(wq_rope)).reshape(Q_LORA, HEAD_PAD)], axis=1).astype(BF16)
        wkv = w_kv_b[l].reshape(KV_LORA, N_HEADS, QK_NOPE + V_HEAD)
        w_k = wkv[:, :, :QK_NOPE].reshape(KV_LORA, HEAD_PAD).astype(BF16)
        w_vt = jnp.swapaxes(wkv[:, :, QK_NOPE:].reshape(KV_LORA, HEAD_PAD), 0, 1).astype(BF16)

        qn, kvn, kr_raw = _latent(x_cur, scale1, shift1, w_t, w_kr_t,
                                  g_q_a[l].reshape(1, -1), g_kv_a[l].reshape(1, -1))
        q_nope, q_rope, k_nope, v_t, k_rope = _qkv_up(qn, kvn, kr_raw, pos_b, inv_row,
                                                      w_q, w_k, w_vt)
        attn = _attention(q_nope, q_rope, k_nope, k_rope, v_t)
        cm = _conv_branch(x_cur, scale1, shift1, w_t, o_cb, w_conv[l])
        gates = _gates(x_cur, scale1, shift1, w_t, o_cb + 3 * d, 2 * d)
        merged = _merge(attn, cm, w_o_a[l].astype(BF16), w_o_b[l].astype(BF16), gates)
        x_cur = _mix_out(merged, w_o[l].astype(BF16), x_cur, gate1,
                         ln1_g[l].reshape(1, -1), ln1_b[l].reshape(1, -1))

        x_cur = _ffn(x_cur, scale2, shift2, gate2, w_ffn_in[l].astype(BF16),
                     w_ffn_out[l].astype(BF16), ln2_g[l].reshape(1, -1), ln2_b[l].reshape(1, -1))

    return x_cur.reshape(bsz, seq, d)
```

```python
import functools
import math

import jax
import jax.numpy as jnp
from jax import lax
from jax.experimental import pallas as pl
from jax.experimental.pallas import tpu as pltpu

F32 = jnp.float32
BF16 = jnp.bfloat16

D_MODEL = 2048
SEQ = 2048
CHUNK = 64
N_HEADS = 16
QK_NOPE = 128
QK_ROPE = 64
V_HEAD = 128
Q_LORA = 512
KV_LORA = 512
ROPE_THETA = 10000.0
ATTN_SCALE = (QK_NOPE + QK_ROPE) ** -0.5
CONV_K = 3
D_FF = 5632
DEPTH = 1
DEEPNORM_ALPHA = (2.0 * DEPTH) ** 0.25
LN_EPS = 1e-5
RMS_EPS = 1e-6

LANES = 128
SUBLANES = 8
HALF = QK_ROPE // 2
HEAD_PAD = N_HEADS * LANES
ATTN_TQ = 256
UP_HEAD_GROUP = 4
ROW_CHUNK = 256
Q_PRESCALE = ATTN_SCALE * math.log2(math.e)

VMEM_LIMIT = 52 * 1024 * 1024


def _params(n_axes, vmem=VMEM_LIMIT):
    return pltpu.CompilerParams(
        dimension_semantics=("arbitrary",) * n_axes, vmem_limit_bytes=vmem)


def _dot(a, b):
    return jnp.dot(a, b, preferred_element_type=F32)


def _dot_nt(a, b_t):
    return lax.dot_general(a, b_t, (((1,), (1,)), ((), ())), preferred_element_type=F32)


def _modulate(x, scale, shift):
    return (x * (1.0 + scale) + shift).astype(BF16)


def _layer_norm(y, g, b):
    mu = jnp.mean(y, axis=-1, keepdims=True)
    yc = y - mu
    var = jnp.mean(yc * yc, axis=-1, keepdims=True)
    return yc * lax.rsqrt(var + LN_EPS) * g + b


def _rms_norm(y, g):
    return y * lax.rsqrt(jnp.mean(y * y, axis=-1, keepdims=True) + RMS_EPS) * g


def _ada_kernel(c_ref, w_ref, b_ref, o_ref):
    c = c_ref[...]
    c_act = (c * jax.nn.sigmoid(c)).astype(BF16)
    o_ref[...] = _dot(c_act, w_ref[...].astype(BF16)) + b_ref[...]


def _ada(c_pad, w_ada, b_ada, tn=1024):
    rows, d = c_pad.shape
    n = w_ada.shape[1]
    return pl.pallas_call(
        _ada_kernel,
        grid=(n // tn,),
        in_specs=[pl.BlockSpec((rows, d), lambda j: (0, 0)),
                  pl.BlockSpec((d, tn), lambda j: (0, j)),
                  pl.BlockSpec((1, tn), lambda j: (0, j))],
        out_specs=pl.BlockSpec((rows, tn), lambda j: (0, j)),
        out_shape=jax.ShapeDtypeStruct((rows, n), F32),
        compiler_params=_params(1),
        name="ada_mod",
    )(c_pad, w_ada, b_ada)


def _latent_kernel(x_ref, sc_ref, sh_ref, w_ref, wkr_ref, gq_ref, gkv_ref,
                   qn_ref, kvn_ref, kr_ref):
    u = _modulate(x_ref[...], sc_ref[0], sh_ref[0])
    p = _dot_nt(u, w_ref[...])
    qn_ref[...] = _rms_norm(p[:, :Q_LORA], gq_ref[...]).astype(BF16)
    kvn_ref[...] = _rms_norm(p[:, Q_LORA:], gkv_ref[...]).astype(BF16)
    kr_ref[...] = _dot_nt(u, wkr_ref[...])


def _latent(x2, scale, shift, w_t, w_kr_t, g_q, g_kv, bm=512):
    m, d = x2.shape
    per_seq = SEQ // bm
    n_lat = Q_LORA + KV_LORA
    mod_spec = pl.BlockSpec((1, 1, d), lambda i: (i // per_seq, 0, 0))
    return pl.pallas_call(
        _latent_kernel,
        grid=(m // bm,),
        in_specs=[pl.BlockSpec((bm, d), lambda i: (i, 0)),
                  mod_spec, mod_spec,
                  pl.BlockSpec((n_lat, d), lambda i: (0, 0)),
                  pl.BlockSpec((2 * LANES, d), lambda i: (0, 0)),
                  pl.BlockSpec((1, Q_LORA), lambda i: (0, 0)),
                  pl.BlockSpec((1, KV_LORA), lambda i: (0, 0))],
        out_specs=[pl.BlockSpec((bm, Q_LORA), lambda i: (i, 0)),
                   pl.BlockSpec((bm, KV_LORA), lambda i: (i, 0)),
                   pl.BlockSpec((bm, 2 * LANES), lambda i: (i, 0))],
        out_shape=[jax.ShapeDtypeStruct((m, Q_LORA), BF16),
                   jax.ShapeDtypeStruct((m, KV_LORA), BF16),
                   jax.ShapeDtypeStruct((m, 2 * LANES), F32)],
        compiler_params=_params(1),
        name="latent_proj",
    )(x2, scale, shift, w_t, w_kr_t, g_q, g_kv)


def _qkv_up_kernel(qn_ref, kvn_ref, krr_ref, pos_ref, inv_ref, wq_ref, wk_ref, wvt_ref,
                   qnope_ref, qrope_ref, knope_ref, vt_ref, kr_ref):
    ang = pos_ref[...].astype(F32) * inv_ref[...]
    cos = jnp.cos(ang)
    sin = jnp.sin(ang)
    qn = qn_ref[...]
    kvn = kvn_ref[...]
    grp = UP_HEAD_GROUP * LANES
    cos_g = jnp.concatenate([cos * Q_PRESCALE] * UP_HEAD_GROUP, axis=1)
    sin_g = jnp.concatenate([sin * Q_PRESCALE] * UP_HEAD_GROUP, axis=1)
    for g in range(N_HEADS // UP_HEAD_GROUP):
        lo, hi = g * grp, (g + 1) * grp
        qnope_ref[:, lo:hi] = (_dot(qn, wq_ref[:, lo:hi]) * Q_PRESCALE).astype(BF16)
        r = _dot(qn, wq_ref[:, HEAD_PAD + lo:HEAD_PAD + hi])
        rr = _dot(qn, wq_ref[:, 2 * HEAD_PAD + lo:2 * HEAD_PAD + hi])
        qrope_ref[:, lo:hi] = (r * cos_g + rr * sin_g).astype(BF16)
        knope_ref[:, lo:hi] = _dot(kvn, wk_ref[:, lo:hi]).astype(BF16)
        vt_ref[lo:hi, :] = _dot_nt(wvt_ref[lo:hi, :], kvn).astype(BF16)
    krr = krr_ref[...]
    kr_ref[...] = (krr[:, :LANES] * cos + krr[:, LANES:] * sin).astype(BF16)


def _qkv_up(qn, kvn, kr_raw, pos_b, inv_row, w_q, w_k, w_vt, bm=512):
    m = qn.shape[0]
    row = lambda n: pl.BlockSpec((bm, n), lambda i: (i, 0))
    const = lambda a: pl.BlockSpec(a.shape, lambda i: (0, 0))
    return pl.pallas_call(
        _qkv_up_kernel,
        grid=(m // bm,),
        in_specs=[row(Q_LORA), row(KV_LORA), row(2 * LANES), row(LANES),
                  const(inv_row), const(w_q), const(w_k), const(w_vt)],
        out_specs=[row(HEAD_PAD), row(HEAD_PAD), row(HEAD_PAD),
                   pl.BlockSpec((HEAD_PAD, bm), lambda i: (0, i)), row(LANES)],
        out_shape=[jax.ShapeDtypeStruct((m, HEAD_PAD), BF16)] * 3
        + [jax.ShapeDtypeStruct((HEAD_PAD, m), BF16), jax.ShapeDtypeStruct((m, LANES), BF16)],
        compiler_params=_params(1),
        name="qkv_up",
    )(qn, kvn, kr_raw, pos_b, inv_row, w_q, w_k, w_vt)


def _conv_kernel(x_ref, sc_ref, sh_ref, wb_ref, wc_ref, wx_ref, wconv_ref, o_ref,
                 u_ref, halo_ref, *, tiles_per_seq):
    i = pl.program_id(0)
    j = pl.program_id(1)

    @pl.when(j == 0)
    def _():
        u_ref[...] = _modulate(x_ref[...], sc_ref[0], sh_ref[0])

    u = u_ref[...]
    cb = _dot_nt(u, wb_ref[...])
    z = _dot_nt(u, wc_ref[...]) * _dot_nt(u, wx_ref[...])
    bm = z.shape[0]

    @pl.when(i % tiles_per_seq == 0)
    def _():
        halo_ref[j] = jnp.zeros(halo_ref.shape[1:], F32)

    prev = halo_ref[j]
    rows = lax.broadcasted_iota(jnp.int32, z.shape, 0)
    z1 = jnp.where(rows == 0, prev[SUBLANES - 1:SUBLANES, :], pltpu.roll(z, 1, 0))
    z2 = jnp.where(rows == 0, prev[SUBLANES - 2:SUBLANES - 1, :],
                   jnp.where(rows == 1, prev[SUBLANES - 1:SUBLANES, :], pltpu.roll(z, 2, 0)))
    halo_ref[j] = z[bm - SUBLANES:, :]
    w = wconv_ref[...]
    o_ref[...] = (cb * (w[0:1, :] * z2 + w[1:2, :] * z1 + w[2:3, :] * z)).astype(BF16)


def _conv_branch(x2, scale, shift, w_t, row0, w_conv, bm=1024, tn=512):
    m, d = x2.shape
    n = w_conv.shape[1]
    per_seq = SEQ // bm
    mod_spec = pl.BlockSpec((1, 1, d), lambda i, j: (i // per_seq, 0, 0))

    def w_spec(first_row):
        return pl.BlockSpec((tn, d), lambda i, j: (first_row // tn + j, 0))

    return pl.pallas_call(
        functools.partial(_conv_kernel, tiles_per_seq=per_seq),
        grid=(m // bm, n // tn),
        in_specs=[pl.BlockSpec((bm, d), lambda i, j: (i, 0)), mod_spec, mod_spec,
                  w_spec(row0), w_spec(row0 + n), w_spec(row0 + 2 * n),
                  pl.BlockSpec((CONV_K, tn), lambda i, j: (0, j))],
        out_specs=pl.BlockSpec((bm, tn), lambda i, j: (i, j)),
        out_shape=jax.ShapeDtypeStruct((m, n), BF16),
        scratch_shapes=[pltpu.VMEM((bm, d), BF16),
                        pltpu.VMEM((n // tn, SUBLANES, tn), F32)],
        compiler_params=_params(2),
        name="conv_branch",
    )(x2, scale, shift, w_t, w_t, w_t, w_conv)


def _gate_kernel(x_ref, sc_ref, sh_ref, w_ref, o_ref, u_ref):
    @pl.when(pl.program_id(1) == 0)
    def _():
        u_ref[...] = _modulate(x_ref[...], sc_ref[0], sh_ref[0])

    o_ref[...] = jax.nn.sigmoid(_dot_nt(u_ref[...], w_ref[...])).astype(BF16)


def _gates(x2, scale, shift, w_t, row0, n, bm=1024, tn=1024):
    m, d = x2.shape
    per_seq = SEQ // bm
    mod_spec = pl.BlockSpec((1, 1, d), lambda i, j: (i // per_seq, 0, 0))
    return pl.pallas_call(
        _gate_kernel,
        grid=(m // bm, n // tn),
        in_specs=[pl.BlockSpec((bm, d), lambda i, j: (i, 0)), mod_spec, mod_spec,
                  pl.BlockSpec((tn, d), lambda i, j: (row0 // tn + j, 0))],
        out_specs=pl.BlockSpec((bm, tn), lambda i, j: (i, j)),
        out_shape=jax.ShapeDtypeStruct((m, n), BF16),
        scratch_shapes=[pltpu.VMEM((bm, d), BF16)],
        compiler_params=_params(2),
        name="merge_gates",
    )(x2, scale, shift, w_t)


def _attn_kernel(*refs, n_cast):
    qn_ref, qr_ref, kn_ref, kr_ref, vt_ref = refs[:5]
    cast_src = refs[5:5 + n_cast]
    o_ref = refs[5 + n_cast]
    cast_dst = refs[6 + n_cast:]
    for src, dst in zip(cast_src, cast_dst, strict=True):
        dst[...] = src[...].astype(BF16)

    k = jnp.concatenate([kn_ref[...], kr_ref[...]], axis=1)
    key = lax.broadcasted_iota(jnp.int32, (ATTN_TQ, ATTN_TQ), 0)
    qry = lax.broadcasted_iota(jnp.int32, (ATTN_TQ, ATTN_TQ), 1)
    allowed = (key // CHUNK) <= (qry // CHUNK)
    n_blk = SEQ // ATTN_TQ

    def scores(i):
        q0 = i * ATTN_TQ
        L = q0 + ATTN_TQ
        q = jnp.concatenate([qn_ref[q0:L, :], qr_ref[q0:L, :]], axis=1)
        return _dot_nt(k[:L], q)

    s_next = scores(0)
    for i in range(n_blk):
        q0 = i * ATTN_TQ
        L = q0 + ATTN_TQ
        s_t = s_next
        if i + 1 < n_blk:
            s_next = scores(i + 1)
        s_diag = jnp.where(allowed, s_t[q0:, :], -1e30)
        s_t = s_diag if i == 0 else jnp.concatenate([s_t[:q0, :], s_diag], axis=0)
        e = jnp.exp2(s_t - jnp.max(s_t, axis=0, keepdims=True))
        denom = jnp.sum(e, axis=0, keepdims=True)
        o_t = _dot(vt_ref[:, :L], e.astype(BF16))
        o_ref[q0:L, :] = (o_t / denom).T.astype(BF16)


def _slab_specs(w, n_steps, row0=0, n_rows=None):
    n_rows = w.shape[0] - row0 if n_rows is None else n_rows
    cols = w.shape[1]
    bf16_rows = 2 * SUBLANES
    step = lambda b, h: b * N_HEADS + h
    if n_rows % (n_steps * bf16_rows) == 0:
        rb = n_rows // n_steps
        in_spec = pl.BlockSpec(
            (pl.Element(rb), pl.Element(cols)),
            lambda b, h: (pl.multiple_of(row0 + step(b, h) * rb, SUBLANES), 0))
        out_spec = pl.BlockSpec((rb, cols), lambda b, h: (step(b, h), 0))
    else:
        assert row0 == 0 and n_rows % (n_steps // 2 * bf16_rows) == 0
        rb = n_rows // (n_steps // 2)
        in_spec = out_spec = pl.BlockSpec(
            (rb, cols // 2), lambda b, h: (step(b, h) // 2, step(b, h) % 2))
    return in_spec, out_spec, jax.ShapeDtypeStruct((n_rows, cols), BF16)


def _attention(q_nope, q_rope, k_nope, k_rope, v_t, cast_jobs):
    m = q_nope.shape[0]
    grid = (m // SEQ, N_HEADS)
    head = pl.BlockSpec((SEQ, LANES), lambda b, h: (b, h))
    specs = [_slab_specs(w, grid[0] * grid[1], row0, n_rows) for w, row0, n_rows in cast_jobs]
    outs = pl.pallas_call(
        functools.partial(_attn_kernel, n_cast=len(cast_jobs)),
        grid=grid,
        in_specs=[head, head, head, pl.BlockSpec((SEQ, LANES), lambda b, h: (b, 0)),
                  pl.BlockSpec((V_HEAD, SEQ), lambda b, h: (h, b))] + [s[0] for s in specs],
        out_specs=[head] + [s[1] for s in specs],
        out_shape=[jax.ShapeDtypeStruct((m, HEAD_PAD), BF16)] + [s[2] for s in specs],
        compiler_params=_params(2),
        name="mla_attention",
    )(q_nope, q_rope, k_nope, k_rope, v_t, *[w for w, _, _ in cast_jobs])
    return outs[0], outs[1:]


def _merge_kernel(a_ref, c_ref, wa_ref, wb_ref, ga_ref, gb_ref, o_ref):
    y_a = _dot(a_ref[...], wa_ref[...])
    y_b = _dot(c_ref[...], wb_ref[...])
    o_ref[...] = (ga_ref[...].astype(F32) * y_a + gb_ref[...].astype(F32) * y_b).astype(BF16)


def _merge(attn, cm, w_oa, w_ob, gates, bm=1024, tn=512):
    m, d = attn.shape
    n = w_oa.shape[1]
    nb = n // tn
    lhs = pl.BlockSpec((bm, d), lambda i, j: (i, 0))
    w_spec = pl.BlockSpec((d, tn), lambda i, j: (0, j))
    return pl.pallas_call(
        _merge_kernel,
        grid=(m // bm, nb),
        in_specs=[lhs, lhs, w_spec, w_spec,
                  pl.BlockSpec((bm, tn), lambda i, j: (i, j)),
                  pl.BlockSpec((bm, tn), lambda i, j: (i, j + nb))],
        out_specs=pl.BlockSpec((bm, tn), lambda i, j: (i, j)),
        out_shape=jax.ShapeDtypeStruct((m, n), BF16),
        compiler_params=_params(2),
        name="branch_merge",
    )(attn, cm, w_oa, w_ob, gates, gates)


def _mix_out_kernel(mg_ref, w_ref, x_ref, g1_ref, lg_ref, lb_ref, o_ref):
    n_chunks = mg_ref.shape[0] // ROW_CHUNK
    mix = _dot(mg_ref[0:ROW_CHUNK, :], w_ref[...])
    for r in range(n_chunks):
        rows = slice(r * ROW_CHUNK, (r + 1) * ROW_CHUNK)
        cur = mix
        if r + 1 < n_chunks:
            mix = _dot(mg_ref[(r + 1) * ROW_CHUNK:(r + 2) * ROW_CHUNK, :], w_ref[...])
        y = DEEPNORM_ALPHA * x_ref[rows, :] + g1_ref[0] * cur
        o_ref[rows, :] = _layer_norm(y, lg_ref[...], lb_ref[...])


def _mix_out(merged, w_o, x2, gate1, ln_g, ln_b, bm=512):
    m, d = x2.shape
    per_seq = SEQ // bm
    row = pl.BlockSpec((bm, d), lambda i: (i, 0))
    vec = pl.BlockSpec((1, d), lambda i: (0, 0))
    return pl.pallas_call(
        _mix_out_kernel,
        grid=(m // bm,),
        in_specs=[row, pl.BlockSpec((d, d), lambda i: (0, 0)), row,
                  pl.BlockSpec((1, 1, d), lambda i: (i // per_seq, 0, 0)), vec, vec],
        out_specs=row,
        out_shape=jax.ShapeDtypeStruct((m, d), F32),
        compiler_params=_params(1),
        name="mix_out_ln",
    )(merged, w_o, x2, gate1, ln_g, ln_b)


def _ffn_kernel(x_ref, sc_ref, sh_ref, g2_ref, wg_ref, wu_ref, wo_ref, lg_ref, lb_ref, o_ref,
                u_ref, acc_ref):
    j = pl.program_id(1)

    @pl.when(j == 0)
    def _():
        u_ref[...] = _modulate(x_ref[...], sc_ref[0], sh_ref[0])
        acc_ref[...] = jnp.zeros_like(acc_ref)

    u = u_ref[...]
    hg = _dot(u, wg_ref[...])
    h = (hg * jax.nn.sigmoid(hg) * _dot(u, wu_ref[...])).astype(BF16)
    acc_ref[...] += _dot(h, wo_ref[...])

    @pl.when(j == pl.num_programs(1) - 1)
    def _():
        y = DEEPNORM_ALPHA * x_ref[...] + g2_ref[0] * acc_ref[...]
        o_ref[...] = _layer_norm(y, lg_ref[...], lb_ref[...])


def _ffn(x1, scale, shift, gate, w_in, w_out, ln_g, ln_b, bm=512, tf=512):
    m, d = x1.shape
    nf = D_FF // tf
    per_seq = SEQ // bm
    row = pl.BlockSpec((bm, d), lambda i, j: (i, 0))
    mod_spec = pl.BlockSpec((1, 1, d), lambda i, j: (i // per_seq, 0, 0))
    vec = pl.BlockSpec((1, d), lambda i, j: (0, 0))
    return pl.pallas_call(
        _ffn_kernel,
        grid=(m // bm, nf),
        in_specs=[row, mod_spec, mod_spec, mod_spec,
                  pl.BlockSpec((d, tf), lambda i, j: (0, j)),
                  pl.BlockSpec((d, tf), lambda i, j: (0, j + nf)),
                  pl.BlockSpec((tf, d), lambda i, j: (j, 0)), vec, vec],
        out_specs=row,
        out_shape=jax.ShapeDtypeStruct((m, d), F32),
        scratch_shapes=[pltpu.VMEM((bm, d), BF16), pltpu.VMEM((bm, d), F32)],
        compiler_params=_params(2),
        name="swiglu_ffn_ln",
    )(x1, scale, shift, gate, w_in, w_in, w_out, ln_g, ln_b)


def _rot_last(w):
    return jnp.concatenate([-w[..., HALF:], w[..., :HALF]], axis=-1)


def _pad_lanes(w):
    return jnp.pad(w, [(0, 0)] * (w.ndim - 1) + [(0, LANES - w.shape[-1])])


def kernel(x, c, positions, w_ada, b_ada, w_in, g_q_a, w_q_b, g_kv_a, w_kv_b, w_o_a,
           w_conv, w_o_b, w_o, ln1_g, ln1_b, w_ffn_in, w_ffn_out, ln2_g, ln2_b):
    bsz, seq, d = x.shape
    assert (seq, d) == (SEQ, D_MODEL) and w_ada.shape[0] == DEPTH
    m = bsz * seq
    x_cur = x.reshape(m, d)

    inv_freq = 1.0 / (ROPE_THETA ** (jnp.arange(0, QK_ROPE, 2, dtype=F32) / QK_ROPE))
    inv_row = jnp.tile(inv_freq, LANES // HALF).reshape(1, LANES)
    pos_b = jnp.broadcast_to(positions.reshape(m, 1), (m, LANES))
    c_pad = jnp.pad(c, ((0, SUBLANES - bsz), (0, 0)))

    for l in range(DEPTH):
        mod = _ada(c_pad, w_ada[l], b_ada[l].reshape(1, -1))[:bsz]
        shift1, scale1, gate1, shift2, scale2, gate2 = [
            t.reshape(bsz, 1, d) for t in jnp.split(mod, 6, axis=-1)]

        o_kr = Q_LORA + KV_LORA
        o_cb = o_kr + QK_ROPE
        w_tf = jnp.swapaxes(w_in[l], 0, 1)
        w_lat_t = w_tf[:o_kr].astype(BF16)
        w_kr_rows = w_tf[o_kr:o_cb]
        w_rot_rows = jnp.concatenate([-w_kr_rows[HALF:], w_kr_rows[:HALF]], axis=0)
        lane_pad = jnp.zeros((LANES - QK_ROPE, d), F32)
        w_kr_t = jnp.concatenate([w_kr_rows, lane_pad, w_rot_rows, lane_pad], axis=0).astype(BF16)

        wq = w_q_b[l].reshape(Q_LORA, N_HEADS, QK_NOPE + QK_ROPE)
        wq_rope = wq[:, :, QK_NOPE:]
        w_q = jnp.concatenate([
            wq[:, :, :QK_NOPE].reshape(Q_LORA, HEAD_PAD),
            _pad_lanes(wq_rope).reshape(Q_LORA, HEAD_PAD),
            _pad_lanes(_rot_last(wq_rope)).reshape(Q_LORA, HEAD_PAD)], axis=1).astype(BF16)
        wkv = w_kv_b[l].reshape(KV_LORA, N_HEADS, QK_NOPE + V_HEAD)
        w_k = wkv[:, :, :QK_NOPE].reshape(KV_LORA, HEAD_PAD).astype(BF16)
        w_vt = jnp.swapaxes(wkv[:, :, QK_NOPE:].reshape(KV_LORA, HEAD_PAD), 0, 1).astype(BF16)

        qn, kvn, kr_raw = _latent(x_cur, scale1, shift1, w_lat_t, w_kr_t,
                                  g_q_a[l].reshape(1, -1), g_kv_a[l].reshape(1, -1))
        q_nope, q_rope, k_nope, v_t, k_rope = _qkv_up(qn, kvn, kr_raw, pos_b, inv_row,
                                                      w_q, w_k, w_vt)
        attn, (w_cg, w_oa, w_ob, w_om, w_fi, w_fo) = _attention(
            q_nope, q_rope, k_nope, k_rope, v_t,
            [(w_tf, o_cb, None), (w_o_a[l], 0, None), (w_o_b[l], 0, None), (w_o[l], 0, None),
             (w_ffn_in[l], 0, None), (w_ffn_out[l], 0, None)])
        cm = _conv_branch(x_cur, scale1, shift1, w_cg, 0, w_conv[l])
        gates = _gates(x_cur, scale1, shift1, w_cg, 3 * d, 2 * d)
        merged = _merge(attn, cm, w_oa, w_ob, gates)
        x_cur = _mix_out(merged, w_om, x_cur, gate1,
                         ln1_g[l].reshape(1, -1), ln1_b[l].reshape(1, -1))

        x_cur = _ffn(x_cur, scale2, shift2, gate2, w_fi, w_fo,
                     ln2_g[l].reshape(1, -1), ln2_b[l].reshape(1, -1))

    return x_cur.reshape(bsz, seq, d)
```

```python
import functools
import math

import jax
import jax.numpy as jnp
from jax import lax
from jax.experimental import pallas as pl
from jax.experimental.pallas import tpu as pltpu

F32 = jnp.float32
BF16 = jnp.bfloat16

D_MODEL = 2048
SEQ = 2048
CHUNK = 64
N_HEADS = 16
QK_NOPE = 128
QK_ROPE = 64
V_HEAD = 128
Q_LORA = 512
KV_LORA = 512
ROPE_THETA = 10000.0
ATTN_SCALE = (QK_NOPE + QK_ROPE) ** -0.5
CONV_K = 3
D_FF = 5632
DEPTH = 1
DEEPNORM_ALPHA = (2.0 * DEPTH) ** 0.25
LN_EPS = 1e-5
RMS_EPS = 1e-6

LANES = 128
SUBLANES = 8
HALF = QK_ROPE // 2
HEAD_PAD = N_HEADS * LANES
ATTN_TQ = 256
ATTN_LOOKAHEAD = 3
UP_HEAD_GROUP = 4
ROW_CHUNK = 256
Q_PRESCALE = ATTN_SCALE * math.log2(math.e)

VMEM_LIMIT = 52 * 1024 * 1024


def _params(n_axes, vmem=VMEM_LIMIT):
    return pltpu.CompilerParams(
        dimension_semantics=("arbitrary",) * n_axes, vmem_limit_bytes=vmem)


def _dot(a, b):
    return jnp.dot(a, b, preferred_element_type=F32)


def _dot_nt(a, b_t):
    return lax.dot_general(a, b_t, (((1,), (1,)), ((), ())), preferred_element_type=F32)


def _modulate(x, scale, shift):
    return (x * (1.0 + scale) + shift).astype(BF16)


def _layer_norm(y, g, b):
    mu = jnp.mean(y, axis=-1, keepdims=True)
    yc = y - mu
    var = jnp.mean(yc * yc, axis=-1, keepdims=True)
    return yc * lax.rsqrt(var + LN_EPS) * g + b


def _rms_norm(y, g):
    return y * lax.rsqrt(jnp.mean(y * y, axis=-1, keepdims=True) + RMS_EPS) * g


def _ada_kernel(c_ref, w_ref, b_ref, o_ref):
    c = c_ref[...]
    c_act = (c * jax.nn.sigmoid(c)).astype(BF16)
    o_ref[...] = _dot(c_act, w_ref[...].astype(BF16)) + b_ref[...]


def _ada(c_pad, w_ada, b_ada, tn=1024):
    rows, d = c_pad.shape
    n = w_ada.shape[1]
    return pl.pallas_call(
        _ada_kernel,
        grid=(n // tn,),
        in_specs=[pl.BlockSpec((rows, d), lambda j: (0, 0)),
                  pl.BlockSpec((d, tn), lambda j: (0, j)),
                  pl.BlockSpec((1, tn), lambda j: (0, j))],
        out_specs=pl.BlockSpec((rows, tn), lambda j: (0, j)),
        out_shape=jax.ShapeDtypeStruct((rows, n), F32),
        compiler_params=_params(1),
        name="ada_mod",
    )(c_pad, w_ada, b_ada)


def _latent_kernel(x_ref, sc_ref, sh_ref, w_ref, wkr_ref, gq_ref, gkv_ref,
                   qn_ref, kvn_ref, kr_ref, wbf_ref):
    @pl.when(pl.program_id(0) == 0)
    def _():
        wbf_ref[...] = w_ref[...].astype(BF16)

    u = _modulate(x_ref[...], sc_ref[0], sh_ref[0])
    p = _dot_nt(u, wbf_ref[...])
    qn_ref[...] = _rms_norm(p[:, :Q_LORA], gq_ref[...]).astype(BF16)
    kvn_ref[...] = _rms_norm(p[:, Q_LORA:], gkv_ref[...]).astype(BF16)
    kr_ref[...] = _dot_nt(u, wkr_ref[...])


def _latent(x2, scale, shift, w_t, w_kr_t, g_q, g_kv, bm=512):
    m, d = x2.shape
    per_seq = SEQ // bm
    n_lat = Q_LORA + KV_LORA
    mod_spec = pl.BlockSpec((1, 1, d), lambda i: (i // per_seq, 0, 0))
    return pl.pallas_call(
        _latent_kernel,
        grid=(m // bm,),
        in_specs=[pl.BlockSpec((bm, d), lambda i: (i, 0)),
                  mod_spec, mod_spec,
                  pl.BlockSpec((n_lat, d), lambda i: (0, 0), pipeline_mode=pl.Buffered(1)),
                  pl.BlockSpec((2 * LANES, d), lambda i: (0, 0)),
                  pl.BlockSpec((1, Q_LORA), lambda i: (0, 0)),
                  pl.BlockSpec((1, KV_LORA), lambda i: (0, 0))],
        out_specs=[pl.BlockSpec((bm, Q_LORA), lambda i: (i, 0)),
                   pl.BlockSpec((bm, KV_LORA), lambda i: (i, 0)),
                   pl.BlockSpec((bm, 2 * LANES), lambda i: (i, 0))],
        out_shape=[jax.ShapeDtypeStruct((m, Q_LORA), BF16),
                   jax.ShapeDtypeStruct((m, KV_LORA), BF16),
                   jax.ShapeDtypeStruct((m, 2 * LANES), F32)],
        scratch_shapes=[pltpu.VMEM((n_lat, d), BF16)],
        compiler_params=_params(1),
        name="latent_proj",
    )(x2, scale, shift, w_t, w_kr_t, g_q, g_kv)


def _qkv_up_kernel(qn_ref, kvn_ref, krr_ref, pos_ref, inv_ref, wq_ref, wk_ref, wvt_ref,
                   qnope_ref, qrope_ref, knope_ref, vt_ref, kr_ref):
    ang = pos_ref[...].astype(F32) * inv_ref[...]
    cos = jnp.cos(ang)
    sin = jnp.sin(ang)
    qn = qn_ref[...]
    kvn = kvn_ref[...]
    grp = UP_HEAD_GROUP * LANES
    cos_g = jnp.concatenate([cos * Q_PRESCALE] * UP_HEAD_GROUP, axis=1)
    sin_g = jnp.concatenate([sin * Q_PRESCALE] * UP_HEAD_GROUP, axis=1)
    for g in range(N_HEADS // UP_HEAD_GROUP):
        lo, hi = g * grp, (g + 1) * grp
        qnope_ref[:, lo:hi] = (_dot(qn, wq_ref[:, lo:hi]) * Q_PRESCALE).astype(BF16)
        r = _dot(qn, wq_ref[:, HEAD_PAD + lo:HEAD_PAD + hi])
        rr = _dot(qn, wq_ref[:, 2 * HEAD_PAD + lo:2 * HEAD_PAD + hi])
        qrope_ref[:, lo:hi] = (r * cos_g + rr * sin_g).astype(BF16)
        knope_ref[:, lo:hi] = _dot(kvn, wk_ref[:, lo:hi]).astype(BF16)
        vt_ref[lo:hi, :] = _dot_nt(wvt_ref[lo:hi, :], kvn).astype(BF16)
    krr = krr_ref[...]
    kr_ref[...] = (krr[:, :LANES] * cos + krr[:, LANES:] * sin).astype(BF16)


def _qkv_up(qn, kvn, kr_raw, pos_b, inv_row, w_q, w_k, w_vt, bm=512):
    m = qn.shape[0]
    row = lambda n: pl.BlockSpec((bm, n), lambda i: (i, 0))
    const = lambda a: pl.BlockSpec(a.shape, lambda i: (0, 0))
    return pl.pallas_call(
        _qkv_up_kernel,
        grid=(m // bm,),
        in_specs=[row(Q_LORA), row(KV_LORA), row(2 * LANES), row(LANES),
                  const(inv_row), const(w_q), const(w_k), const(w_vt)],
        out_specs=[row(HEAD_PAD), row(HEAD_PAD), row(HEAD_PAD),
                   pl.BlockSpec((HEAD_PAD, bm), lambda i: (0, i)), row(LANES)],
        out_shape=[jax.ShapeDtypeStruct((m, HEAD_PAD), BF16)] * 3
        + [jax.ShapeDtypeStruct((HEAD_PAD, m), BF16), jax.ShapeDtypeStruct((m, LANES), BF16)],
        compiler_params=_params(1),
        name="qkv_up",
    )(qn, kvn, kr_raw, pos_b, inv_row, w_q, w_k, w_vt)


def _conv_kernel(x_ref, sc_ref, sh_ref, wb_ref, wc_ref, wx_ref, wconv_ref, o_ref,
                 u_ref, halo_ref, *, tiles_per_seq):
    i = pl.program_id(0)
    j = pl.program_id(1)

    @pl.when(j == 0)
    def _():
        u_ref[...] = _modulate(x_ref[...], sc_ref[0], sh_ref[0])

    u = u_ref[...]
    cb = _dot_nt(u, wb_ref[...])
    z = _dot_nt(u, wc_ref[...]) * _dot_nt(u, wx_ref[...])
    bm = z.shape[0]

    @pl.when(i % tiles_per_seq == 0)
    def _():
        halo_ref[j] = jnp.zeros(halo_ref.shape[1:], F32)

    prev = halo_ref[j]
    rows = lax.broadcasted_iota(jnp.int32, z.shape, 0)
    z1 = jnp.where(rows == 0, prev[SUBLANES - 1:SUBLANES, :], pltpu.roll(z, 1, 0))
    z2 = jnp.where(rows == 0, prev[SUBLANES - 2:SUBLANES - 1, :],
                   jnp.where(rows == 1, prev[SUBLANES - 1:SUBLANES, :], pltpu.roll(z, 2, 0)))
    halo_ref[j] = z[bm - SUBLANES:, :]
    w = wconv_ref[...]
    o_ref[...] = (cb * (w[0:1, :] * z2 + w[1:2, :] * z1 + w[2:3, :] * z)).astype(BF16)


def _conv_branch(x2, scale, shift, w_t, row0, w_conv, bm=1024, tn=512):
    m, d = x2.shape
    n = w_conv.shape[1]
    per_seq = SEQ // bm
    mod_spec = pl.BlockSpec((1, 1, d), lambda i, j: (i // per_seq, 0, 0))

    def w_spec(first_row):
        return pl.BlockSpec((tn, d), lambda i, j: (first_row // tn + j, 0))

    return pl.pallas_call(
        functools.partial(_conv_kernel, tiles_per_seq=per_seq),
        grid=(m // bm, n // tn),
        in_specs=[pl.BlockSpec((bm, d), lambda i, j: (i, 0)), mod_spec, mod_spec,
                  w_spec(row0), w_spec(row0 + n), w_spec(row0 + 2 * n),
                  pl.BlockSpec((CONV_K, tn), lambda i, j: (0, j))],
        out_specs=pl.BlockSpec((bm, tn), lambda i, j: (i, j)),
        out_shape=jax.ShapeDtypeStruct((m, n), BF16),
        scratch_shapes=[pltpu.VMEM((bm, d), BF16),
                        pltpu.VMEM((n // tn, SUBLANES, tn), F32)],
        compiler_params=_params(2),
        name="conv_branch",
    )(x2, scale, shift, w_t, w_t, w_t, w_conv)


def _gate_kernel(x_ref, sc_ref, sh_ref, w_ref, o_ref, u_ref):
    @pl.when(pl.program_id(1) == 0)
    def _():
        u_ref[...] = _modulate(x_ref[...], sc_ref[0], sh_ref[0])

    o_ref[...] = jax.nn.sigmoid(_dot_nt(u_ref[...], w_ref[...])).astype(BF16)


def _gates(x2, scale, shift, w_t, row0, n, bm=1024, tn=1024):
    m, d = x2.shape
    per_seq = SEQ // bm
    mod_spec = pl.BlockSpec((1, 1, d), lambda i, j: (i // per_seq, 0, 0))
    return pl.pallas_call(
        _gate_kernel,
        grid=(m // bm, n // tn),
        in_specs=[pl.BlockSpec((bm, d), lambda i, j: (i, 0)), mod_spec, mod_spec,
                  pl.BlockSpec((tn, d), lambda i, j: (row0 // tn + j, 0))],
        out_specs=pl.BlockSpec((bm, tn), lambda i, j: (i, j)),
        out_shape=jax.ShapeDtypeStruct((m, n), BF16),
        scratch_shapes=[pltpu.VMEM((bm, d), BF16)],
        compiler_params=_params(2),
        name="merge_gates",
    )(x2, scale, shift, w_t)


def _attn_kernel(*refs, n_cast):
    qn_ref, qr_ref, kn_ref, kr_ref, vt_ref = refs[:5]
    cast_src = refs[5:5 + n_cast]
    o_ref = refs[5 + n_cast]
    cast_dst = refs[6 + n_cast:]
    for src, dst in zip(cast_src, cast_dst, strict=True):
        dst[...] = src[...].astype(BF16)

    k = jnp.concatenate([kn_ref[...], kr_ref[...]], axis=1)
    key = lax.broadcasted_iota(jnp.int32, (ATTN_TQ, ATTN_TQ), 0)
    qry = lax.broadcasted_iota(jnp.int32, (ATTN_TQ, ATTN_TQ), 1)
    allowed = (key // CHUNK) <= (qry // CHUNK)
    n_blk = SEQ // ATTN_TQ

    def scores(i):
        q0 = i * ATTN_TQ
        L = q0 + ATTN_TQ
        q = jnp.concatenate([qn_ref[q0:L, :], qr_ref[q0:L, :]], axis=1)
        return _dot_nt(k[:L], q)

    order = list(range(n_blk - 1, -1, -1))
    pending = [scores(i) for i in order[:ATTN_LOOKAHEAD]]
    for pos, i in enumerate(order):
        q0 = i * ATTN_TQ
        L = q0 + ATTN_TQ
        s_t = pending.pop(0)
        if pos + ATTN_LOOKAHEAD < n_blk:
            pending.append(scores(order[pos + ATTN_LOOKAHEAD]))
        s_diag = jnp.where(allowed, s_t[q0:, :], -1e30)
        s_t = s_diag if i == 0 else jnp.concatenate([s_t[:q0, :], s_diag], axis=0)
        e = jnp.exp2(s_t - jnp.max(s_t, axis=0, keepdims=True))
        denom = jnp.sum(e, axis=0, keepdims=True)
        o_t = _dot(vt_ref[:, :L], e.astype(BF16))
        o_ref[q0:L, :] = (o_t / denom).T.astype(BF16)


def _slab_specs(w, n_steps, row0=0, n_rows=None):
    n_rows = w.shape[0] - row0 if n_rows is None else n_rows
    cols = w.shape[1]
    bf16_rows = 2 * SUBLANES
    step = lambda b, h: b * N_HEADS + h
    if n_rows % (n_steps * bf16_rows) == 0:
        rb = n_rows // n_steps
        in_spec = pl.BlockSpec(
            (pl.Element(rb), pl.Element(cols)),
            lambda b, h: (pl.multiple_of(row0 + step(b, h) * rb, SUBLANES), 0))
        out_spec = pl.BlockSpec((rb, cols), lambda b, h: (step(b, h), 0))
    else:
        assert row0 == 0 and n_rows % (n_steps // 2 * bf16_rows) == 0
        rb = n_rows // (n_steps // 2)
        in_spec = out_spec = pl.BlockSpec(
            (rb, cols // 2), lambda b, h: (step(b, h) // 2, step(b, h) % 2))
    return in_spec, out_spec, jax.ShapeDtypeStruct((n_rows, cols), BF16)


def _attention(q_nope, q_rope, k_nope, k_rope, v_t, cast_jobs):
    m = q_nope.shape[0]
    grid = (m // SEQ, N_HEADS)
    head = pl.BlockSpec((SEQ, LANES), lambda b, h: (b, h))
    specs = [_slab_specs(w, grid[0] * grid[1], row0, n_rows) for w, row0, n_rows in cast_jobs]
    outs = pl.pallas_call(
        functools.partial(_attn_kernel, n_cast=len(cast_jobs)),
        grid=grid,
        in_specs=[head, head, head, pl.BlockSpec((SEQ, LANES), lambda b, h: (b, 0)),
                  pl.BlockSpec((V_HEAD, SEQ), lambda b, h: (h, b))] + [s[0] for s in specs],
        out_specs=[head] + [s[1] for s in specs],
        out_shape=[jax.ShapeDtypeStruct((m, HEAD_PAD), BF16)] + [s[2] for s in specs],
        compiler_params=_params(2),
        name="mla_attention",
    )(q_nope, q_rope, k_nope, k_rope, v_t, *[w for w, _, _ in cast_jobs])
    return outs[0], outs[1:]


def _merge_kernel(a_ref, c_ref, wa_ref, wb_ref, ga_ref, gb_ref, o_ref):
    y_a = _dot(a_ref[...], wa_ref[...])
    y_b = _dot(c_ref[...], wb_ref[...])
    o_ref[...] = (ga_ref[...].astype(F32) * y_a + gb_ref[...].astype(F32) * y_b).astype(BF16)


def _merge(attn, cm, w_oa, w_ob, gates, bm=1024, tn=512):
    m, d = attn.shape
    n = w_oa.shape[1]
    nb = n // tn
    lhs = pl.BlockSpec((bm, d), lambda i, j: (i, 0))
    w_spec = pl.BlockSpec((d, tn), lambda i, j: (0, j))
    return pl.pallas_call(
        _merge_kernel,
        grid=(m // bm, nb),
        in_specs=[lhs, lhs, w_spec, w_spec,
                  pl.BlockSpec((bm, tn), lambda i, j: (i, j)),
                  pl.BlockSpec((bm, tn), lambda i, j: (i, j + nb))],
        out_specs=pl.BlockSpec((bm, tn), lambda i, j: (i, j)),
        out_shape=jax.ShapeDtypeStruct((m, n), BF16),
        compiler_params=_params(2),
        name="branch_merge",
    )(attn, cm, w_oa, w_ob, gates, gates)


def _mix_out_kernel(mg_ref, w_ref, x_ref, g1_ref, lg_ref, lb_ref, o_ref):
    n_chunks = mg_ref.shape[0] // ROW_CHUNK
    mix = _dot(mg_ref[0:ROW_CHUNK, :], w_ref[...])
    for r in range(n_chunks):
        rows = slice(r * ROW_CHUNK, (r + 1) * ROW_CHUNK)
        cur = mix
        if r + 1 < n_chunks:
            mix = _dot(mg_ref[(r + 1) * ROW_CHUNK:(r + 2) * ROW_CHUNK, :], w_ref[...])
        y = DEEPNORM_ALPHA * x_ref[rows, :] + g1_ref[0] * cur
        o_ref[rows, :] = _layer_norm(y, lg_ref[...], lb_ref[...])


def _mix_out(merged, w_o, x2, gate1, ln_g, ln_b, bm=512):
    m, d = x2.shape
    per_seq = SEQ // bm
    row = pl.BlockSpec((bm, d), lambda i: (i, 0))
    vec = pl.BlockSpec((1, d), lambda i: (0, 0))
    return pl.pallas_call(
        _mix_out_kernel,
        grid=(m // bm,),
        in_specs=[row, pl.BlockSpec((d, d), lambda i: (0, 0)), row,
                  pl.BlockSpec((1, 1, d), lambda i: (i // per_seq, 0, 0)), vec, vec],
        out_specs=row,
        out_shape=jax.ShapeDtypeStruct((m, d), F32),
        compiler_params=_params(1),
        name="mix_out_ln",
    )(merged, w_o, x2, gate1, ln_g, ln_b)


def _ffn_kernel(x_ref, sc_ref, sh_ref, g2_ref, wg_ref, wu_ref, wo_ref, lg_ref, lb_ref, o_ref,
                u_ref, acc_ref):
    j = pl.program_id(1)

    @pl.when(j == 0)
    def _():
        u_ref[...] = _modulate(x_ref[...], sc_ref[0], sh_ref[0])
        acc_ref[...] = jnp.zeros_like(acc_ref)

    u = u_ref[...]
    hg = _dot(u, wg_ref[...])
    h = (hg * jax.nn.sigmoid(hg) * _dot(u, wu_ref[...])).astype(BF16)
    acc_ref[...] += _dot(h, wo_ref[...])

    @pl.when(j == pl.num_programs(1) - 1)
    def _():
        y = DEEPNORM_ALPHA * x_ref[...] + g2_ref[0] * acc_ref[...]
        o_ref[...] = _layer_norm(y, lg_ref[...], lb_ref[...])


def _ffn(x1, scale, shift, gate, w_in, w_out, ln_g, ln_b, bm=512, tf=512):
    m, d = x1.shape
    nf = D_FF // tf
    per_seq = SEQ // bm
    row = pl.BlockSpec((bm, d), lambda i, j: (i, 0))
    mod_spec = pl.BlockSpec((1, 1, d), lambda i, j: (i // per_seq, 0, 0))
    vec = pl.BlockSpec((1, d), lambda i, j: (0, 0))
    return pl.pallas_call(
        _ffn_kernel,
        grid=(m // bm, nf),
        in_specs=[row, mod_spec, mod_spec, mod_spec,
                  pl.BlockSpec((d, tf), lambda i, j: (0, j)),
                  pl.BlockSpec((d, tf), lambda i, j: (0, j + nf)),
                  pl.BlockSpec((tf, d), lambda i, j: (j, 0)), vec, vec],
        out_specs=row,
        out_shape=jax.ShapeDtypeStruct((m, d), F32),
        scratch_shapes=[pltpu.VMEM((bm, d), BF16), pltpu.VMEM((bm, d), F32)],
        compiler_params=_params(2),
        name="swiglu_ffn_ln",
    )(x1, scale, shift, gate, w_in, w_in, w_out, ln_g, ln_b)


def _rot_last(w):
    return jnp.concatenate([-w[..., HALF:], w[..., :HALF]], axis=-1)


def _pad_lanes(w):
    return jnp.pad(w, [(0, 0)] * (w.ndim - 1) + [(0, LANES - w.shape[-1])])


def kernel(x, c, positions, w_ada, b_ada, w_in, g_q_a, w_q_b, g_kv_a, w_kv_b, w_o_a,
           w_conv, w_o_b, w_o, ln1_g, ln1_b, w_ffn_in, w_ffn_out, ln2_g, ln2_b):
    bsz, seq, d = x.shape
    assert (seq, d) == (SEQ, D_MODEL) and w_ada.shape[0] == DEPTH
    m = bsz * seq
    x_cur = x.reshape(m, d)

    inv_freq = 1.0 / (ROPE_THETA ** (jnp.arange(0, QK_ROPE, 2, dtype=F32) / QK_ROPE))
    inv_row = jnp.tile(inv_freq, LANES // HALF).reshape(1, LANES)
    pos_b = jnp.broadcast_to(positions.reshape(m, 1), (m, LANES))
    c_pad = jnp.pad(c, ((0, SUBLANES - bsz), (0, 0)))

    for l in range(DEPTH):
        mod = _ada(c_pad, w_ada[l], b_ada[l].reshape(1, -1))[:bsz]
        shift1, scale1, gate1, shift2, scale2, gate2 = [
            t.reshape(bsz, 1, d) for t in jnp.split(mod, 6, axis=-1)]

        o_kr = Q_LORA + KV_LORA
        o_cb = o_kr + QK_ROPE
        w_tf = jnp.swapaxes(w_in[l], 0, 1)
        w_kr_rows = w_tf[o_kr:o_cb]
        w_rot_rows = jnp.concatenate([-w_kr_rows[HALF:], w_kr_rows[:HALF]], axis=0)
        lane_pad = jnp.zeros((LANES - QK_ROPE, d), F32)
        w_kr_t = jnp.concatenate([w_kr_rows, lane_pad, w_rot_rows, lane_pad], axis=0).astype(BF16)

        wq = w_q_b[l].reshape(Q_LORA, N_HEADS, QK_NOPE + QK_ROPE)
        wq_rope = wq[:, :, QK_NOPE:]
        w_q = jnp.concatenate([
            wq[:, :, :QK_NOPE].reshape(Q_LORA, HEAD_PAD),
            _pad_lanes(wq_rope).reshape(Q_LORA, HEAD_PAD),
            _pad_lanes(_rot_last(wq_rope)).reshape(Q_LORA, HEAD_PAD)], axis=1).astype(BF16)
        wkv = w_kv_b[l].reshape(KV_LORA, N_HEADS, QK_NOPE + V_HEAD)
        w_k = wkv[:, :, :QK_NOPE].reshape(KV_LORA, HEAD_PAD).astype(BF16)
        w_vt = jnp.swapaxes(wkv[:, :, QK_NOPE:].reshape(KV_LORA, HEAD_PAD), 0, 1).astype(BF16)

        qn, kvn, kr_raw = _latent(x_cur, scale1, shift1, w_tf, w_kr_t,
                                  g_q_a[l].reshape(1, -1), g_kv_a[l].reshape(1, -1))
        q_nope, q_rope, k_nope, v_t, k_rope = _qkv_up(qn, kvn, kr_raw, pos_b, inv_row,
                                                      w_q, w_k, w_vt)
        attn, (w_cg, w_oa, w_ob, w_om, w_fi, w_fo) = _attention(
            q_nope, q_rope, k_nope, k_rope, v_t,
            [(w_tf, o_cb, None), (w_o_a[l], 0, None), (w_o_b[l], 0, None), (w_o[l], 0, None),
             (w_ffn_in[l], 0, None), (w_ffn_out[l], 0, None)])
        cm = _conv_branch(x_cur, scale1, shift1, w_cg, 0, w_conv[l])
        gates = _gates(x_cur, scale1, shift1, w_cg, 3 * d, 2 * d)
        merged = _merge(attn, cm, w_oa, w_ob, gates)
        x_cur = _mix_out(merged, w_om, x_cur, gate1,
                         ln1_g[l].reshape(1, -1), ln1_b[l].reshape(1, -1))

        x_cur = _ffn(x_cur, scale2, shift2, gate2, w_fi, w_fo,
                     ln2_g[l].reshape(1, -1), ln2_b[l].reshape(1, -1))

    return x_cur.reshape(bsz, seq, d)
```

```python
import functools
import math

import jax
import jax.numpy as jnp
from jax import lax
from jax.experimental import pallas as pl
from jax.experimental.pallas import tpu as pltpu

F32 = jnp.float32
BF16 = jnp.bfloat16

D_MODEL = 2048
SEQ = 2048
CHUNK = 64
N_HEADS = 16
QK_NOPE = 128
QK_ROPE = 64
V_HEAD = 128
Q_LORA = 512
KV_LORA = 512
ROPE_THETA = 10000.0
ATTN_SCALE = (QK_NOPE + QK_ROPE) ** -0.5
CONV_K = 3
D_FF = 5632
DEPTH = 1
DEEPNORM_ALPHA = (2.0 * DEPTH) ** 0.25
LN_EPS = 1e-5
RMS_EPS = 1e-6

LANES = 128
SUBLANES = 8
HALF = QK_ROPE // 2
HEAD_PAD = N_HEADS * LANES
ATTN_TQ = 256
ATTN_LOOKAHEAD = 3
UP_HEAD_GROUP = 4
ROW_CHUNK = 256
COL_CHUNK = 512
Q_PRESCALE = ATTN_SCALE * math.log2(math.e)

VMEM_LIMIT = 52 * 1024 * 1024


def _params(n_axes, vmem=VMEM_LIMIT):
    return pltpu.CompilerParams(
        dimension_semantics=("arbitrary",) * n_axes, vmem_limit_bytes=vmem)


def _resident(shape, index_map):
    return pl.BlockSpec(shape, index_map, pipeline_mode=pl.Buffered(1))


def _dot(a, b):
    return jnp.dot(a, b, preferred_element_type=F32)


def _dot_nt(a, b_t):
    return lax.dot_general(a, b_t, (((1,), (1,)), ((), ())), preferred_element_type=F32)


def _modulate(x, scale, shift):
    return (x * (1.0 + scale) + shift).astype(BF16)


def _layer_norm(y, g, b):
    mu = jnp.mean(y, axis=-1, keepdims=True)
    yc = y - mu
    var = jnp.mean(yc * yc, axis=-1, keepdims=True)
    return yc * lax.rsqrt(var + LN_EPS) * g + b


def _rms_norm(y, g):
    return y * lax.rsqrt(jnp.mean(y * y, axis=-1, keepdims=True) + RMS_EPS) * g


def _ada_kernel(c_ref, w_ref, b_ref, o_ref):
    c = c_ref[...]
    c_act = (c * jax.nn.sigmoid(c)).astype(BF16)
    o_ref[...] = _dot(c_act, w_ref[...].astype(BF16)) + b_ref[...]


def _ada(c_pad, w_ada, b_ada, tn=1024):
    rows, d = c_pad.shape
    n = w_ada.shape[1]
    return pl.pallas_call(
        _ada_kernel,
        grid=(n // tn,),
        in_specs=[pl.BlockSpec((rows, d), lambda j: (0, 0)),
                  pl.BlockSpec((d, tn), lambda j: (0, j)),
                  pl.BlockSpec((1, tn), lambda j: (0, j))],
        out_specs=pl.BlockSpec((rows, tn), lambda j: (0, j)),
        out_shape=jax.ShapeDtypeStruct((rows, n), F32),
        compiler_params=_params(1),
        name="ada_mod",
    )(c_pad, w_ada, b_ada)


def _latent_kernel(x_ref, sc_ref, sh_ref, w_ref, wkr_ref, gq_ref, gkv_ref,
                   qn_ref, kvn_ref, kr_ref, wbf_ref):
    @pl.when(pl.program_id(0) == 0)
    def _():
        wbf_ref[...] = w_ref[...].astype(BF16)

    u = _modulate(x_ref[...], sc_ref[0], sh_ref[0])
    p = _dot_nt(u, wbf_ref[...])
    qn_ref[...] = _rms_norm(p[:, :Q_LORA], gq_ref[...]).astype(BF16)
    kvn_ref[...] = _rms_norm(p[:, Q_LORA:], gkv_ref[...]).astype(BF16)
    kr_ref[...] = _dot_nt(u, wkr_ref[...].astype(BF16))


def _latent(x2, scale, shift, w_t, w_kr_t, g_q, g_kv, bm=512):
    m, d = x2.shape
    per_seq = SEQ // bm
    n_lat = Q_LORA + KV_LORA
    mod_spec = pl.BlockSpec((1, 1, d), lambda i: (i // per_seq, 0, 0))
    return pl.pallas_call(
        _latent_kernel,
        grid=(m // bm,),
        in_specs=[pl.BlockSpec((bm, d), lambda i: (i, 0)),
                  mod_spec, mod_spec,
                  _resident((n_lat, d), lambda i: (0, 0)),
                  pl.BlockSpec((2 * LANES, d), lambda i: (0, 0)),
                  pl.BlockSpec((1, Q_LORA), lambda i: (0, 0)),
                  pl.BlockSpec((1, KV_LORA), lambda i: (0, 0))],
        out_specs=[pl.BlockSpec((bm, Q_LORA), lambda i: (i, 0)),
                   pl.BlockSpec((bm, KV_LORA), lambda i: (i, 0)),
                   pl.BlockSpec((bm, 2 * LANES), lambda i: (i, 0))],
        out_shape=[jax.ShapeDtypeStruct((m, Q_LORA), BF16),
                   jax.ShapeDtypeStruct((m, KV_LORA), BF16),
                   jax.ShapeDtypeStruct((m, 2 * LANES), F32)],
        scratch_shapes=[pltpu.VMEM((n_lat, d), BF16)],
        compiler_params=_params(1),
        name="latent_proj",
    )(x2, scale, shift, w_t, w_kr_t, g_q, g_kv)


def _qkv_up_kernel(qn_ref, kvn_ref, krr_ref, pos_ref, inv_ref, wq_ref, wk_ref, wvt_ref,
                   qnope_ref, qrope_ref, knope_ref, vt_ref, kr_ref):
    ang = pos_ref[...].astype(F32) * inv_ref[...]
    cos = jnp.cos(ang)
    sin = jnp.sin(ang)
    qn = qn_ref[...]
    kvn = kvn_ref[...]
    grp = UP_HEAD_GROUP * LANES
    cos_g = jnp.concatenate([cos * Q_PRESCALE] * UP_HEAD_GROUP, axis=1)
    sin_g = jnp.concatenate([sin * Q_PRESCALE] * UP_HEAD_GROUP, axis=1)
    for g in range(N_HEADS // UP_HEAD_GROUP):
        lo, hi = g * grp, (g + 1) * grp
        qnope_ref[:, lo:hi] = (_dot(qn, wq_ref[:, lo:hi]) * Q_PRESCALE).astype(BF16)
        r = _dot(qn, wq_ref[:, HEAD_PAD + lo:HEAD_PAD + hi])
        rr = _dot(qn, wq_ref[:, 2 * HEAD_PAD + lo:2 * HEAD_PAD + hi])
        qrope_ref[:, lo:hi] = (r * cos_g + rr * sin_g).astype(BF16)
        knope_ref[:, lo:hi] = _dot(kvn, wk_ref[:, lo:hi]).astype(BF16)
        vt_ref[lo:hi, :] = _dot_nt(wvt_ref[lo:hi, :], kvn).astype(BF16)
    krr = krr_ref[...]
    kr_ref[...] = (krr[:, :LANES] * cos + krr[:, LANES:] * sin).astype(BF16)


def _qkv_up(qn, kvn, kr_raw, pos_b, inv_row, w_q, w_k, w_vt, bm=512):
    m = qn.shape[0]
    row = lambda n: pl.BlockSpec((bm, n), lambda i: (i, 0))
    const = lambda a: pl.BlockSpec(a.shape, lambda i: (0, 0))
    return pl.pallas_call(
        _qkv_up_kernel,
        grid=(m // bm,),
        in_specs=[row(Q_LORA), row(KV_LORA), row(2 * LANES), row(LANES),
                  const(inv_row), const(w_q), const(w_k), const(w_vt)],
        out_specs=[row(HEAD_PAD), row(HEAD_PAD), row(HEAD_PAD),
                   pl.BlockSpec((HEAD_PAD, bm), lambda i: (0, i)), row(LANES)],
        out_shape=[jax.ShapeDtypeStruct((m, HEAD_PAD), BF16)] * 3
        + [jax.ShapeDtypeStruct((HEAD_PAD, m), BF16), jax.ShapeDtypeStruct((m, LANES), BF16)],
        compiler_params=_params(1),
        name="qkv_up",
    )(qn, kvn, kr_raw, pos_b, inv_row, w_q, w_k, w_vt)


def _conv_kernel(x_ref, sc_ref, sh_ref, wb_ref, wc_ref, wx_ref, wconv_ref, o_ref,
                 halo_ref, *, tiles_per_seq):
    u = _modulate(x_ref[...], sc_ref[0], sh_ref[0])
    bm, n = o_ref.shape

    @pl.when(pl.program_id(0) % tiles_per_seq == 0)
    def _():
        halo_ref[...] = jnp.zeros_like(halo_ref)

    rows = lax.broadcasted_iota(jnp.int32, (bm, COL_CHUNK), 0)
    for c0 in range(0, n, COL_CHUNK):
        cols = slice(c0, c0 + COL_CHUNK)
        cb = _dot_nt(u, wb_ref[cols, :])
        z = _dot_nt(u, wc_ref[cols, :]) * _dot_nt(u, wx_ref[cols, :])
        prev = halo_ref[:, cols]
        z1 = jnp.where(rows == 0, prev[SUBLANES - 1:SUBLANES, :], pltpu.roll(z, 1, 0))
        z2 = jnp.where(rows == 0, prev[SUBLANES - 2:SUBLANES - 1, :],
                       jnp.where(rows == 1, prev[SUBLANES - 1:SUBLANES, :], pltpu.roll(z, 2, 0)))
        halo_ref[:, cols] = z[bm - SUBLANES:, :]
        w = wconv_ref[:, cols]
        o_ref[:, cols] = (cb * (w[0:1, :] * z2 + w[1:2, :] * z1 + w[2:3, :] * z)).astype(BF16)


def _conv_branch(x2, scale, shift, w_rows, w_conv, bm=512):
    m, d = x2.shape
    n = w_conv.shape[1]
    per_seq = SEQ // bm
    mod_spec = pl.BlockSpec((1, 1, d), lambda i: (i // per_seq, 0, 0))
    return pl.pallas_call(
        functools.partial(_conv_kernel, tiles_per_seq=per_seq),
        grid=(m // bm,),
        in_specs=[pl.BlockSpec((bm, d), lambda i: (i, 0)), mod_spec, mod_spec,
                  _resident((n, d), lambda i: (0, 0)), _resident((n, d), lambda i: (1, 0)),
                  _resident((n, d), lambda i: (2, 0)),
                  pl.BlockSpec((CONV_K, n), lambda i: (0, 0))],
        out_specs=pl.BlockSpec((bm, n), lambda i: (i, 0)),
        out_shape=jax.ShapeDtypeStruct((m, n), BF16),
        scratch_shapes=[pltpu.VMEM((SUBLANES, n), F32)],
        compiler_params=_params(1),
        name="conv_branch",
    )(x2, scale, shift, w_rows, w_rows, w_rows, w_conv)


def _gate_kernel(x_ref, sc_ref, sh_ref, wa_ref, wb_ref, o_ref):
    u = _modulate(x_ref[...], sc_ref[0], sh_ref[0])
    n = wa_ref.shape[0]
    for g, w_ref in enumerate((wa_ref, wb_ref)):
        for c0 in range(0, n, 2 * COL_CHUNK):
            o_ref[:, g * n + c0:g * n + c0 + 2 * COL_CHUNK] = jax.nn.sigmoid(
                _dot_nt(u, w_ref[c0:c0 + 2 * COL_CHUNK, :])).astype(BF16)


def _gates(x2, scale, shift, w_rows, first_block, bm=512):
    m, d = x2.shape
    per_seq = SEQ // bm
    mod_spec = pl.BlockSpec((1, 1, d), lambda i: (i // per_seq, 0, 0))
    return pl.pallas_call(
        _gate_kernel,
        grid=(m // bm,),
        in_specs=[pl.BlockSpec((bm, d), lambda i: (i, 0)), mod_spec, mod_spec,
                  _resident((d, d), lambda i: (first_block, 0)),
                  _resident((d, d), lambda i: (first_block + 1, 0))],
        out_specs=pl.BlockSpec((bm, 2 * d), lambda i: (i, 0)),
        out_shape=jax.ShapeDtypeStruct((m, 2 * d), BF16),
        compiler_params=_params(1),
        name="merge_gates",
    )(x2, scale, shift, w_rows, w_rows)


def _attn_kernel(*refs, n_cast):
    qn_ref, qr_ref, kn_ref, kr_ref, vt_ref = refs[:5]
    cast_src = refs[5:5 + n_cast]
    o_ref = refs[5 + n_cast]
    cast_dst = refs[6 + n_cast:]
    for src, dst in zip(cast_src, cast_dst, strict=True):
        dst[...] = src[...].astype(BF16)

    k = jnp.concatenate([kn_ref[...], kr_ref[...]], axis=1)
    key = lax.broadcasted_iota(jnp.int32, (ATTN_TQ, ATTN_TQ), 0)
    qry = lax.broadcasted_iota(jnp.int32, (ATTN_TQ, ATTN_TQ), 1)
    allowed = (key // CHUNK) <= (qry // CHUNK)
    n_blk = SEQ // ATTN_TQ

    def scores(i):
        q0 = i * ATTN_TQ
        L = q0 + ATTN_TQ
        q = jnp.concatenate([qn_ref[q0:L, :], qr_ref[q0:L, :]], axis=1)
        return _dot_nt(k[:L], q)

    order = list(range(n_blk - 1, -1, -1))
    pending = [scores(i) for i in order[:ATTN_LOOKAHEAD]]
    for pos, i in enumerate(order):
        q0 = i * ATTN_TQ
        L = q0 + ATTN_TQ
        s_t = pending.pop(0)
        if pos + ATTN_LOOKAHEAD < n_blk:
            pending.append(scores(order[pos + ATTN_LOOKAHEAD]))
        s_diag = jnp.where(allowed, s_t[q0:, :], -1e30)
        s_t = s_diag if i == 0 else jnp.concatenate([s_t[:q0, :], s_diag], axis=0)
        e = jnp.exp2(s_t - jnp.max(s_t, axis=0, keepdims=True))
        denom = jnp.sum(e, axis=0, keepdims=True)
        o_t = _dot(vt_ref[:, :L], e.astype(BF16))
        o_ref[q0:L, :] = (o_t / denom).T.astype(BF16)


def _slab_specs(w, n_steps, row0=0, n_rows=None):
    n_rows = w.shape[0] - row0 if n_rows is None else n_rows
    cols = w.shape[1]
    bf16_rows = 2 * SUBLANES
    step = lambda b, h: b * N_HEADS + h
    if n_rows % (n_steps * bf16_rows) == 0:
        rb = n_rows // n_steps
        in_spec = pl.BlockSpec(
            (pl.Element(rb), pl.Element(cols)),
            lambda b, h: (pl.multiple_of(row0 + step(b, h) * rb, SUBLANES), 0))
        out_spec = pl.BlockSpec((rb, cols), lambda b, h: (step(b, h), 0))
    else:
        assert row0 == 0 and n_rows % (n_steps // 2 * bf16_rows) == 0
        rb = n_rows // (n_steps // 2)
        in_spec = out_spec = pl.BlockSpec(
            (rb, cols // 2), lambda b, h: (step(b, h) // 2, step(b, h) % 2))
    return in_spec, out_spec, jax.ShapeDtypeStruct((n_rows, cols), BF16)


def _attention(q_nope, q_rope, k_nope, k_rope, v_t, cast_jobs):
    m = q_nope.shape[0]
    grid = (m // SEQ, N_HEADS)
    head = pl.BlockSpec((SEQ, LANES), lambda b, h: (b, h))
    specs = [_slab_specs(w, grid[0] * grid[1], row0, n_rows) for w, row0, n_rows in cast_jobs]
    outs = pl.pallas_call(
        functools.partial(_attn_kernel, n_cast=len(cast_jobs)),
        grid=grid,
        in_specs=[head, head, head, pl.BlockSpec((SEQ, LANES), lambda b, h: (b, 0)),
                  pl.BlockSpec((V_HEAD, SEQ), lambda b, h: (h, b))] + [s[0] for s in specs],
        out_specs=[head] + [s[1] for s in specs],
        out_shape=[jax.ShapeDtypeStruct((m, HEAD_PAD), BF16)] + [s[2] for s in specs],
        compiler_params=_params(2),
        name="mla_attention",
    )(q_nope, q_rope, k_nope, k_rope, v_t, *[w for w, _, _ in cast_jobs])
    return outs[0], outs[1:]


def _merge_kernel(a_ref, c_ref, wa_ref, wb_ref, g_ref, o_ref):
    a = a_ref[...]
    cm = c_ref[...]
    n = o_ref.shape[1]
    for c0 in range(0, n, COL_CHUNK):
        cols = slice(c0, c0 + COL_CHUNK)
        y_a = _dot(a, wa_ref[:, cols])
        y_b = _dot(cm, wb_ref[:, cols])
        o_ref[:, cols] = (g_ref[:, cols].astype(F32) * y_a
                          + g_ref[:, n + c0:n + c0 + COL_CHUNK].astype(F32) * y_b).astype(BF16)


def _merge(attn, cm, w_oa, w_ob, gates, bm=512):
    m, d = attn.shape
    n = w_oa.shape[1]
    lhs = pl.BlockSpec((bm, d), lambda i: (i, 0))
    return pl.pallas_call(
        _merge_kernel,
        grid=(m // bm,),
        in_specs=[lhs, lhs, _resident((d, n), lambda i: (0, 0)), _resident((d, n), lambda i: (0, 0)),
                  pl.BlockSpec((bm, 2 * n), lambda i: (i, 0))],
        out_specs=pl.BlockSpec((bm, n), lambda i: (i, 0)),
        out_shape=jax.ShapeDtypeStruct((m, n), BF16),
        compiler_params=_params(1),
        name="branch_merge",
    )(attn, cm, w_oa, w_ob, gates)


def _mix_out_kernel(mg_ref, w_ref, x_ref, g1_ref, lg_ref, lb_ref, o_ref):
    n_chunks = mg_ref.shape[0] // ROW_CHUNK
    mix = _dot(mg_ref[0:ROW_CHUNK, :], w_ref[...])
    for r in range(n_chunks):
        rows = slice(r * ROW_CHUNK, (r + 1) * ROW_CHUNK)
        cur = mix
        if r + 1 < n_chunks:
            mix = _dot(mg_ref[(r + 1) * ROW_CHUNK:(r + 2) * ROW_CHUNK, :], w_ref[...])
        y = DEEPNORM_ALPHA * x_ref[rows, :] + g1_ref[0] * cur
        o_ref[rows, :] = _layer_norm(y, lg_ref[...], lb_ref[...])


def _mix_out(merged, w_o, x2, gate1, ln_g, ln_b, bm=512):
    m, d = x2.shape
    per_seq = SEQ // bm
    row = pl.BlockSpec((bm, d), lambda i: (i, 0))
    vec = pl.BlockSpec((1, d), lambda i: (0, 0))
    return pl.pallas_call(
        _mix_out_kernel,
        grid=(m // bm,),
        in_specs=[row, pl.BlockSpec((d, d), lambda i: (0, 0)), row,
                  pl.BlockSpec((1, 1, d), lambda i: (i // per_seq, 0, 0)), vec, vec],
        out_specs=row,
        out_shape=jax.ShapeDtypeStruct((m, d), F32),
        compiler_params=_params(1),
        name="mix_out_ln",
    )(merged, w_o, x2, gate1, ln_g, ln_b)


def _ffn_kernel(x_ref, sc_ref, sh_ref, g2_ref, wg_ref, wu_ref, wo_ref, lg_ref, lb_ref, o_ref,
                u_ref, acc_ref):
    j = pl.program_id(1)

    @pl.when(j == 0)
    def _():
        u_ref[...] = _modulate(x_ref[...], sc_ref[0], sh_ref[0])
        acc_ref[...] = jnp.zeros_like(acc_ref)

    u = u_ref[...]
    hg = _dot(u, wg_ref[...])
    h = (hg * jax.nn.sigmoid(hg) * _dot(u, wu_ref[...])).astype(BF16)
    acc_ref[...] += _dot(h, wo_ref[...])

    @pl.when(j == pl.num_programs(1) - 1)
    def _():
        y = DEEPNORM_ALPHA * x_ref[...] + g2_ref[0] * acc_ref[...]
        o_ref[...] = _layer_norm(y, lg_ref[...], lb_ref[...])


def _ffn(x1, scale, shift, gate, w_in, w_out, ln_g, ln_b, bm=512, tf=512):
    m, d = x1.shape
    nf = D_FF // tf
    per_seq = SEQ // bm
    row = pl.BlockSpec((bm, d), lambda i, j: (i, 0))
    mod_spec = pl.BlockSpec((1, 1, d), lambda i, j: (i // per_seq, 0, 0))
    vec = pl.BlockSpec((1, d), lambda i, j: (0, 0))
    return pl.pallas_call(
        _ffn_kernel,
        grid=(m // bm, nf),
        in_specs=[row, mod_spec, mod_spec, mod_spec,
                  pl.BlockSpec((d, tf), lambda i, j: (0, j)),
                  pl.BlockSpec((d, tf), lambda i, j: (0, j + nf)),
                  pl.BlockSpec((tf, d), lambda i, j: (j, 0)), vec, vec],
        out_specs=row,
        out_shape=jax.ShapeDtypeStruct((m, d), F32),
        scratch_shapes=[pltpu.VMEM((bm, d), BF16), pltpu.VMEM((bm, d), F32)],
        compiler_params=_params(2),
        name="swiglu_ffn_ln",
    )(x1, scale, shift, gate, w_in, w_in, w_out, ln_g, ln_b)


def _rot_last(w):
    return jnp.concatenate([-w[..., HALF:], w[..., :HALF]], axis=-1)


def _pad_lanes(w):
    return jnp.pad(w, [(0, 0)] * (w.ndim - 1) + [(0, LANES - w.shape[-1])])


def kernel(x, c, positions, w_ada, b_ada, w_in, g_q_a, w_q_b, g_kv_a, w_kv_b, w_o_a,
           w_conv, w_o_b, w_o, ln1_g, ln1_b, w_ffn_in, w_ffn_out, ln2_g, ln2_b):
    bsz, seq, d = x.shape
    assert (seq, d) == (SEQ, D_MODEL) and w_ada.shape[0] == DEPTH
    m = bsz * seq
    x_cur = x.reshape(m, d)

    inv_freq = 1.0 / (ROPE_THETA ** (jnp.arange(0, QK_ROPE, 2, dtype=F32) / QK_ROPE))
    inv_row = jnp.tile(inv_freq, LANES // HALF).reshape(1, LANES)
    pos_b = jnp.broadcast_to(positions.reshape(m, 1), (m, LANES))
    c_pad = jnp.pad(c, ((0, SUBLANES - bsz), (0, 0)))

    for l in range(DEPTH):
        mod = _ada(c_pad, w_ada[l], b_ada[l].reshape(1, -1))[:bsz]
        shift1, scale1, gate1, shift2, scale2, gate2 = [
            t.reshape(bsz, 1, d) for t in jnp.split(mod, 6, axis=-1)]

        o_kr = Q_LORA + KV_LORA
        o_cb = o_kr + QK_ROPE
        w_tf = jnp.swapaxes(w_in[l], 0, 1)
        w_kr_rows = w_tf[o_kr:o_cb]
        w_rot_rows = jnp.concatenate([-w_kr_rows[HALF:], w_kr_rows[:HALF]], axis=0)
        lane_pad = jnp.zeros((LANES - QK_ROPE, d), F32)
        w_kr_t = jnp.concatenate([w_kr_rows, lane_pad, w_rot_rows, lane_pad], axis=0)

        wq = w_q_b[l].reshape(Q_LORA, N_HEADS, QK_NOPE + QK_ROPE)
        wq_rope = wq[:, :, QK_NOPE:]
        w_q = jnp.concatenate([
            wq[:, :, :QK_NOPE].reshape(Q_LORA, HEAD_PAD),
            _pad_lanes(wq_rope).reshape(Q_LORA, HEAD_PAD),
            _pad_lanes(_rot_last(wq_rope)).reshape(Q_LORA, HEAD_PAD)], axis=1).astype(BF16)
        wkv = w_kv_b[l].reshape(KV_LORA, N_HEADS, QK_NOPE + V_HEAD)
        w_k = wkv[:, :, :QK_NOPE].reshape(KV_LORA, HEAD_PAD).astype(BF16)
        w_vt = jnp.swapaxes(wkv[:, :, QK_NOPE:].reshape(KV_LORA, HEAD_PAD), 0, 1).astype(BF16)

        qn, kvn, kr_raw = _latent(x_cur, scale1, shift1, w_tf, w_kr_t,
                                  g_q_a[l].reshape(1, -1), g_kv_a[l].reshape(1, -1))
        q_nope, q_rope, k_nope, v_t, k_rope = _qkv_up(qn, kvn, kr_raw, pos_b, inv_row,
                                                      w_q, w_k, w_vt)
        attn, (w_cg, w_oa, w_ob, w_om, w_fi, w_fo) = _attention(
            q_nope, q_rope, k_nope, k_rope, v_t,
            [(w_tf, o_cb, None), (w_o_a[l], 0, None), (w_o_b[l], 0, None), (w_o[l], 0, None),
             (w_ffn_in[l], 0, None), (w_ffn_out[l], 0, None)])
        cm = _conv_branch(x_cur, scale1, shift1, w_cg, w_conv[l])
        gates = _gates(x_cur, scale1, shift1, w_cg, 3)
        merged = _merge(attn, cm, w_oa, w_ob, gates)
        x_cur = _mix_out(merged, w_om, x_cur, gate1,
                         ln1_g[l].reshape(1, -1), ln1_b[l].reshape(1, -1))

        x_cur = _ffn(x_cur, scale2, shift2, gate2, w_fi, w_fo,
                     ln2_g[l].reshape(1, -1), ln2_b[l].reshape(1, -1))

    return x_cur.reshape(bsz, seq, d)
```

```python
import functools
import math

import jax
import jax.numpy as jnp
from jax import lax
from jax.experimental import pallas as pl
from jax.experimental.pallas import tpu as pltpu

F32 = jnp.float32
BF16 = jnp.bfloat16

D_MODEL = 2048
SEQ = 2048
CHUNK = 64
N_HEADS = 16
QK_NOPE = 128
QK_ROPE = 64
V_HEAD = 128
Q_LORA = 512
KV_LORA = 512
ROPE_THETA = 10000.0
ATTN_SCALE = (QK_NOPE + QK_ROPE) ** -0.5
CONV_K = 3
D_FF = 5632
DEPTH = 1
DEEPNORM_ALPHA = (2.0 * DEPTH) ** 0.25
LN_EPS = 1e-5
RMS_EPS = 1e-6

LANES = 128
SUBLANES = 8
HALF = QK_ROPE // 2
HEAD_PAD = N_HEADS * LANES
ATTN_TQ = 256
ATTN_LOOKAHEAD = 4
ATTN_HEADS_PER_STEP = 2
CAST_PIECE_ROWS = 32
UP_HEAD_GROUP = 4
ROW_CHUNK = 256
COL_CHUNK = 512
Q_PRESCALE = ATTN_SCALE * math.log2(math.e)

VMEM_LIMIT = 52 * 1024 * 1024


def _params(n_axes, vmem=VMEM_LIMIT):
    return pltpu.CompilerParams(
        dimension_semantics=("arbitrary",) * n_axes, vmem_limit_bytes=vmem)


def _resident(shape, index_map):
    return pl.BlockSpec(shape, index_map, pipeline_mode=pl.Buffered(1))


def _slab_specs(w, n_steps, step_of, row0=0):
    n_rows = w.shape[0] - row0
    cols = w.shape[1]
    bf16_rows = 2 * SUBLANES
    if n_rows % (n_steps * bf16_rows) == 0:
        rb = n_rows // n_steps
        in_spec = pl.BlockSpec(
            (pl.Element(rb), pl.Element(cols)),
            lambda *g: (pl.multiple_of(row0 + step_of(*g) * rb, SUBLANES), 0))
        out_spec = pl.BlockSpec((rb, cols), lambda *g: (step_of(*g), 0))
    else:
        assert row0 == 0 and n_rows % (n_steps // 2 * bf16_rows) == 0
        rb = n_rows // (n_steps // 2)
        in_spec = out_spec = pl.BlockSpec(
            (rb, cols // 2), lambda *g: (step_of(*g) // 2, step_of(*g) % 2))
    return in_spec, out_spec, jax.ShapeDtypeStruct((n_rows, cols), BF16)


def _split_refs(refs, n_in, n_cast):
    return refs[:n_in], refs[n_in:n_in + n_cast], refs[n_in + n_cast], refs[n_in + n_cast + 1:]


def _cast_pieces(cast_src, cast_dst):
    pieces = []
    for src, dst in zip(cast_src, cast_dst, strict=True):
        rows = src.shape[0]
        n_piece = next(n for n in range(max(1, rows // CAST_PIECE_ROWS), 0, -1)
                       if rows % n == 0 and (rows // n) % (2 * SUBLANES) == 0)
        pieces += [(src, dst, slice(p * rows // n_piece, (p + 1) * rows // n_piece))
                   for p in range(n_piece)]
    return pieces


def _convert_pieces(pieces, k, n):
    for src, dst, rows in pieces[k::n]:
        dst[rows, :] = src[rows, :].astype(BF16)


def _dot(a, b):
    return jnp.dot(a, b, preferred_element_type=F32)


def _dot_nt(a, b_t):
    return lax.dot_general(a, b_t, (((1,), (1,)), ((), ())), preferred_element_type=F32)


def _modulate(x, scale, shift):
    return (x * (1.0 + scale) + shift).astype(BF16)


def _layer_norm(y, g, b):
    mu = jnp.mean(y, axis=-1, keepdims=True)
    yc = y - mu
    var = jnp.mean(yc * yc, axis=-1, keepdims=True)
    return yc * lax.rsqrt(var + LN_EPS) * g + b


def _rms_norm(y, g):
    return y * lax.rsqrt(jnp.mean(y * y, axis=-1, keepdims=True) + RMS_EPS) * g


def _ada_kernel(c_ref, w_ref, b_ref, o_ref):
    c = c_ref[...]
    c_act = (c * jax.nn.sigmoid(c)).astype(BF16)
    o_ref[...] = _dot(c_act, w_ref[...].astype(BF16)) + b_ref[...]


def _ada(c_pad, w_ada, b_ada, tn=1024):
    rows, d = c_pad.shape
    n = w_ada.shape[1]
    return pl.pallas_call(
        _ada_kernel,
        grid=(n // tn,),
        in_specs=[pl.BlockSpec((rows, d), lambda j: (0, 0)),
                  pl.BlockSpec((d, tn), lambda j: (0, j)),
                  pl.BlockSpec((1, tn), lambda j: (0, j))],
        out_specs=pl.BlockSpec((rows, tn), lambda j: (0, j)),
        out_shape=jax.ShapeDtypeStruct((rows, n), F32),
        compiler_params=_params(1),
        name="ada_mod",
    )(c_pad, w_ada, b_ada)


def _latent_kernel(x_ref, sc_ref, sh_ref, w_ref, wkr_ref, gq_ref, gkv_ref,
                   qn_ref, kvn_ref, kr_ref, wbf_ref):
    @pl.when(pl.program_id(0) == 0)
    def _():
        wbf_ref[...] = w_ref[...].astype(BF16)

    u = _modulate(x_ref[...], sc_ref[0], sh_ref[0])
    p = _dot_nt(u, wbf_ref[...])
    qn_ref[...] = _rms_norm(p[:, :Q_LORA], gq_ref[...]).astype(BF16)
    kvn_ref[...] = _rms_norm(p[:, Q_LORA:], gkv_ref[...]).astype(BF16)
    kr_ref[...] = _dot_nt(u, wkr_ref[...].astype(BF16))


def _latent(x2, scale, shift, w_t, w_kr_t, g_q, g_kv, bm=512):
    m, d = x2.shape
    per_seq = SEQ // bm
    n_lat = Q_LORA + KV_LORA
    mod_spec = pl.BlockSpec((1, 1, d), lambda i: (i // per_seq, 0, 0))
    return pl.pallas_call(
        _latent_kernel,
        grid=(m // bm,),
        in_specs=[pl.BlockSpec((bm, d), lambda i: (i, 0)),
                  mod_spec, mod_spec,
                  _resident((n_lat, d), lambda i: (0, 0)),
                  pl.BlockSpec((2 * LANES, d), lambda i: (0, 0)),
                  pl.BlockSpec((1, Q_LORA), lambda i: (0, 0)),
                  pl.BlockSpec((1, KV_LORA), lambda i: (0, 0))],
        out_specs=[pl.BlockSpec((bm, Q_LORA), lambda i: (i, 0)),
                   pl.BlockSpec((bm, KV_LORA), lambda i: (i, 0)),
                   pl.BlockSpec((bm, 2 * LANES), lambda i: (i, 0))],
        out_shape=[jax.ShapeDtypeStruct((m, Q_LORA), BF16),
                   jax.ShapeDtypeStruct((m, KV_LORA), BF16),
                   jax.ShapeDtypeStruct((m, 2 * LANES), F32)],
        scratch_shapes=[pltpu.VMEM((n_lat, d), BF16)],
        compiler_params=_params(1),
        name="latent_proj",
    )(x2, scale, shift, w_t, w_kr_t, g_q, g_kv)


def _qkv_up_kernel(qn_ref, kvn_ref, krr_ref, pos_ref, inv_ref, wq_ref, wk_ref, wvt_ref,
                   qnope_ref, qrope_ref, knope_ref, vt_ref, kr_ref):
    ang = pos_ref[...].astype(F32) * inv_ref[...]
    cos = jnp.cos(ang)
    sin = jnp.sin(ang)
    qn = qn_ref[...]
    kvn = kvn_ref[...]
    grp = UP_HEAD_GROUP * LANES
    cos_g = jnp.concatenate([cos * Q_PRESCALE] * UP_HEAD_GROUP, axis=1)
    sin_g = jnp.concatenate([sin * Q_PRESCALE] * UP_HEAD_GROUP, axis=1)
    for g in range(N_HEADS // UP_HEAD_GROUP):
        lo, hi = g * grp, (g + 1) * grp
        qnope_ref[:, lo:hi] = (_dot(qn, wq_ref[:, lo:hi]) * Q_PRESCALE).astype(BF16)
        r = _dot(qn, wq_ref[:, HEAD_PAD + lo:HEAD_PAD + hi])
        rr = _dot(qn, wq_ref[:, 2 * HEAD_PAD + lo:2 * HEAD_PAD + hi])
        qrope_ref[:, lo:hi] = (r * cos_g + rr * sin_g).astype(BF16)
        knope_ref[:, lo:hi] = _dot(kvn, wk_ref[:, lo:hi]).astype(BF16)
        vt_ref[lo:hi, :] = _dot_nt(wvt_ref[lo:hi, :], kvn).astype(BF16)
    krr = krr_ref[...]
    kr_ref[...] = (krr[:, :LANES] * cos + krr[:, LANES:] * sin).astype(BF16)


def _qkv_up(qn, kvn, kr_raw, pos_b, inv_row, w_q, w_k, w_vt, bm=512):
    m = qn.shape[0]
    row = lambda n: pl.BlockSpec((bm, n), lambda i: (i, 0))
    const = lambda a: pl.BlockSpec(a.shape, lambda i: (0, 0))
    return pl.pallas_call(
        _qkv_up_kernel,
        grid=(m // bm,),
        in_specs=[row(Q_LORA), row(KV_LORA), row(2 * LANES), row(LANES),
                  const(inv_row), const(w_q), const(w_k), const(w_vt)],
        out_specs=[row(HEAD_PAD), row(HEAD_PAD), row(HEAD_PAD),
                   pl.BlockSpec((HEAD_PAD, bm), lambda i: (0, i)), row(LANES)],
        out_shape=[jax.ShapeDtypeStruct((m, HEAD_PAD), BF16)] * 3
        + [jax.ShapeDtypeStruct((HEAD_PAD, m), BF16), jax.ShapeDtypeStruct((m, LANES), BF16)],
        compiler_params=_params(1),
        name="qkv_up",
    )(qn, kvn, kr_raw, pos_b, inv_row, w_q, w_k, w_vt)


def _conv_kernel(x_ref, sc_ref, sh_ref, wb_ref, wc_ref, wx_ref, wconv_ref, o_ref,
                 u_ref, halo_ref, *, tiles_per_seq):
    i = pl.program_id(0)
    j = pl.program_id(1)

    @pl.when(j == 0)
    def _():
        u_ref[...] = _modulate(x_ref[...], sc_ref[0], sh_ref[0])

    u = u_ref[...]
    cb = _dot_nt(u, wb_ref[...])
    z = _dot_nt(u, wc_ref[...]) * _dot_nt(u, wx_ref[...])
    bm = z.shape[0]

    @pl.when(i % tiles_per_seq == 0)
    def _():
        halo_ref[j] = jnp.zeros(halo_ref.shape[1:], F32)

    prev = halo_ref[j]
    rows = lax.broadcasted_iota(jnp.int32, z.shape, 0)
    z1 = jnp.where(rows == 0, prev[SUBLANES - 1:SUBLANES, :], pltpu.roll(z, 1, 0))
    z2 = jnp.where(rows == 0, prev[SUBLANES - 2:SUBLANES - 1, :],
                   jnp.where(rows == 1, prev[SUBLANES - 1:SUBLANES, :], pltpu.roll(z, 2, 0)))
    halo_ref[j] = z[bm - SUBLANES:, :]
    w = wconv_ref[...]
    o_ref[...] = (cb * (w[0:1, :] * z2 + w[1:2, :] * z1 + w[2:3, :] * z)).astype(BF16)


def _conv_branch(x2, scale, shift, w_rows, w_conv, bm=1024, tn=512):
    m, d = x2.shape
    n = w_conv.shape[1]
    per_seq = SEQ // bm
    mod_spec = pl.BlockSpec((1, 1, d), lambda i, j: (i // per_seq, 0, 0))
    w_spec = lambda piece: pl.BlockSpec((tn, d), lambda i, j: (piece * (n // tn) + j, 0))
    return pl.pallas_call(
        functools.partial(_conv_kernel, tiles_per_seq=per_seq),
        grid=(m // bm, n // tn),
        in_specs=[pl.BlockSpec((bm, d), lambda i, j: (i, 0)), mod_spec, mod_spec,
                  w_spec(0), w_spec(1), w_spec(2),
                  pl.BlockSpec((CONV_K, tn), lambda i, j: (0, j))],
        out_specs=pl.BlockSpec((bm, tn), lambda i, j: (i, j)),
        out_shape=jax.ShapeDtypeStruct((m, n), BF16),
        scratch_shapes=[pltpu.VMEM((bm, d), BF16),
                        pltpu.VMEM((n // tn, SUBLANES, tn), F32)],
        compiler_params=_params(2),
        name="conv_branch",
    )(x2, scale, shift, w_rows, w_rows, w_rows, w_conv)


def _gate_kernel(*refs, n_cast):
    (x_ref, sc_ref, sh_ref, wa_ref, wb_ref), cast_src, o_ref, cast_dst = _split_refs(refs, 5, n_cast)
    pieces = _cast_pieces(cast_src, cast_dst)
    u = _modulate(x_ref[...], sc_ref[0], sh_ref[0])
    n = wa_ref.shape[0]
    passes = [(g, c0) for g in range(2) for c0 in range(0, n, 2 * COL_CHUNK)]
    for k, (g, c0) in enumerate(passes):
        w_ref = (wa_ref, wb_ref)[g]
        o_ref[:, g * n + c0:g * n + c0 + 2 * COL_CHUNK] = jax.nn.sigmoid(
            _dot_nt(u, w_ref[c0:c0 + 2 * COL_CHUNK, :])).astype(BF16)
        _convert_pieces(pieces, k, len(passes))


def _gates(x2, scale, shift, w_rows, first_block, cast_jobs, bm=512):
    m, d = x2.shape
    per_seq = SEQ // bm
    mod_spec = pl.BlockSpec((1, 1, d), lambda i: (i // per_seq, 0, 0))
    specs = [_slab_specs(w, m // bm, lambda i: i, row0) for w, row0 in cast_jobs]
    outs = pl.pallas_call(
        functools.partial(_gate_kernel, n_cast=len(cast_jobs)),
        grid=(m // bm,),
        in_specs=[pl.BlockSpec((bm, d), lambda i: (i, 0)), mod_spec, mod_spec,
                  _resident((d, d), lambda i: (first_block, 0)),
                  _resident((d, d), lambda i: (first_block + 1, 0))] + [s[0] for s in specs],
        out_specs=[pl.BlockSpec((bm, 2 * d), lambda i: (i, 0))] + [s[1] for s in specs],
        out_shape=[jax.ShapeDtypeStruct((m, 2 * d), BF16)] + [s[2] for s in specs],
        compiler_params=_params(1),
        name="merge_gates",
    )(x2, scale, shift, w_rows, w_rows, *[w for w, _ in cast_jobs])
    return outs[0], outs[1:]


def _attn_kernel(*refs, n_cast):
    (qn_ref, qr_ref, kn_ref, kr_ref, vt_ref), cast_src, o_ref, cast_dst = _split_refs(refs, 5, n_cast)
    pieces = _cast_pieces(cast_src, cast_dst)
    kr = kr_ref[...]
    ks = [jnp.concatenate([kn_ref[:, a * LANES:(a + 1) * LANES], kr], axis=1)
          for a in range(ATTN_HEADS_PER_STEP)]
    key = lax.broadcasted_iota(jnp.int32, (ATTN_TQ, ATTN_TQ), 0)
    qry = lax.broadcasted_iota(jnp.int32, (ATTN_TQ, ATTN_TQ), 1)
    allowed = (key // CHUNK) <= (qry // CHUNK)
    n_blk = SEQ // ATTN_TQ

    def scores(task):
        a, i = task
        q0 = i * ATTN_TQ
        L = q0 + ATTN_TQ
        lanes = slice(a * LANES, (a + 1) * LANES)
        q = jnp.concatenate([qn_ref[q0:L, lanes], qr_ref[q0:L, lanes]], axis=1)
        return _dot_nt(ks[a][:L], q)

    tasks = [(a, i) for i in range(n_blk - 1, -1, -1) for a in range(ATTN_HEADS_PER_STEP)]
    pending = [scores(t) for t in tasks[:ATTN_LOOKAHEAD]]
    for pos, (a, i) in enumerate(tasks):
        q0 = i * ATTN_TQ
        L = q0 + ATTN_TQ
        s_t = pending.pop(0)
        if pos + ATTN_LOOKAHEAD < len(tasks):
            pending.append(scores(tasks[pos + ATTN_LOOKAHEAD]))
        s_diag = jnp.where(allowed, s_t[q0:, :], -1e30)
        s_t = s_diag if i == 0 else jnp.concatenate([s_t[:q0, :], s_diag], axis=0)
        e = jnp.exp2(s_t - jnp.max(s_t, axis=0, keepdims=True))
        denom = jnp.sum(e, axis=0, keepdims=True)
        o_t = _dot(vt_ref[a * V_HEAD:(a + 1) * V_HEAD, :L], e.astype(BF16))
        o_ref[q0:L, a * LANES:(a + 1) * LANES] = (o_t / denom).T.astype(BF16)
        _convert_pieces(pieces, pos, len(tasks))


def _attention(q_nope, q_rope, k_nope, k_rope, v_t, cast_jobs):
    m = q_nope.shape[0]
    hps = ATTN_HEADS_PER_STEP
    grid = (m // SEQ, N_HEADS // hps)
    head = pl.BlockSpec((SEQ, hps * LANES), lambda b, h: (b, h))
    specs = [_slab_specs(w, grid[0] * grid[1], lambda b, h: b * grid[1] + h, row0)
             for w, row0 in cast_jobs]
    outs = pl.pallas_call(
        functools.partial(_attn_kernel, n_cast=len(cast_jobs)),
        grid=grid,
        in_specs=[head, head, head, pl.BlockSpec((SEQ, LANES), lambda b, h: (b, 0)),
                  pl.BlockSpec((hps * V_HEAD, SEQ), lambda b, h: (h, b))] + [s[0] for s in specs],
        out_specs=[head] + [s[1] for s in specs],
        out_shape=[jax.ShapeDtypeStruct((m, HEAD_PAD), BF16)] + [s[2] for s in specs],
        compiler_params=_params(2),
        name="mla_attention",
    )(q_nope, q_rope, k_nope, k_rope, v_t, *[w for w, _ in cast_jobs])
    return outs[0], outs[1:]


def _merge_kernel(*refs, n_cast):
    (a_ref, c_ref, wa_ref, wb_ref, g_ref), cast_src, o_ref, cast_dst = _split_refs(refs, 5, n_cast)
    pieces = _cast_pieces(cast_src, cast_dst)
    a = a_ref[...]
    cm = c_ref[...]
    n = o_ref.shape[1]
    for k, c0 in enumerate(range(0, n, COL_CHUNK)):
        cols = slice(c0, c0 + COL_CHUNK)
        y_a = _dot(a, wa_ref[:, cols])
        y_b = _dot(cm, wb_ref[:, cols])
        o_ref[:, cols] = (g_ref[:, cols].astype(F32) * y_a
                          + g_ref[:, n + c0:n + c0 + COL_CHUNK].astype(F32) * y_b).astype(BF16)
        _convert_pieces(pieces, k, n // COL_CHUNK)


def _merge(attn, cm, w_oa, w_ob, gates, cast_jobs, bm=512):
    m, d = attn.shape
    n = w_oa.shape[1]
    lhs = pl.BlockSpec((bm, d), lambda i: (i, 0))
    specs = [_slab_specs(w, m // bm, lambda i: i, row0) for w, row0 in cast_jobs]
    outs = pl.pallas_call(
        functools.partial(_merge_kernel, n_cast=len(cast_jobs)),
        grid=(m // bm,),
        in_specs=[lhs, lhs, _resident((d, n), lambda i: (0, 0)), _resident((d, n), lambda i: (0, 0)),
                  pl.BlockSpec((bm, 2 * n), lambda i: (i, 0))] + [s[0] for s in specs],
        out_specs=[pl.BlockSpec((bm, n), lambda i: (i, 0))] + [s[1] for s in specs],
        out_shape=[jax.ShapeDtypeStruct((m, n), BF16)] + [s[2] for s in specs],
        compiler_params=_params(1),
        name="branch_merge",
    )(attn, cm, w_oa, w_ob, gates, *[w for w, _ in cast_jobs])
    return outs[0], outs[1:]


def _mix_out_kernel(mg_ref, w_ref, x_ref, g1_ref, lg_ref, lb_ref, o_ref):
    n_chunks = mg_ref.shape[0] // ROW_CHUNK
    mix = _dot(mg_ref[0:ROW_CHUNK, :], w_ref[...])
    for r in range(n_chunks):
        rows = slice(r * ROW_CHUNK, (r + 1) * ROW_CHUNK)
        cur = mix
        if r + 1 < n_chunks:
            mix = _dot(mg_ref[(r + 1) * ROW_CHUNK:(r + 2) * ROW_CHUNK, :], w_ref[...])
        y = DEEPNORM_ALPHA * x_ref[rows, :] + g1_ref[0] * cur
        o_ref[rows, :] = _layer_norm(y, lg_ref[...], lb_ref[...])


def _mix_out(merged, w_o, x2, gate1, ln_g, ln_b, bm=512):
    m, d = x2.shape
    per_seq = SEQ // bm
    row = pl.BlockSpec((bm, d), lambda i: (i, 0))
    vec = pl.BlockSpec((1, d), lambda i: (0, 0))
    return pl.pallas_call(
        _mix_out_kernel,
        grid=(m // bm,),
        in_specs=[row, pl.BlockSpec((d, d), lambda i: (0, 0)), row,
                  pl.BlockSpec((1, 1, d), lambda i: (i // per_seq, 0, 0)), vec, vec],
        out_specs=row,
        out_shape=jax.ShapeDtypeStruct((m, d), F32),
        compiler_params=_params(1),
        name="mix_out_ln",
    )(merged, w_o, x2, gate1, ln_g, ln_b)


def _ffn_kernel(x_ref, sc_ref, sh_ref, g2_ref, wg_ref, wu_ref, wo_ref, lg_ref, lb_ref, o_ref,
                u_ref, acc_ref):
    j = pl.program_id(1)

    @pl.when(j == 0)
    def _():
        u_ref[...] = _modulate(x_ref[...], sc_ref[0], sh_ref[0])
        acc_ref[...] = jnp.zeros_like(acc_ref)

    u = u_ref[...]
    tf = wg_ref.shape[1]
    half = tf // 2
    hs = []
    for c0 in (0, half):
        hg = _dot(u, wg_ref[:, c0:c0 + half])
        hs.append((hg * jax.nn.sigmoid(hg) * _dot(u, wu_ref[:, c0:c0 + half])).astype(BF16))
    acc_ref[...] += _dot(hs[0], wo_ref[:half, :]) + _dot(hs[1], wo_ref[half:, :])

    @pl.when(j == pl.num_programs(1) - 1)
    def _():
        y = DEEPNORM_ALPHA * x_ref[...] + g2_ref[0] * acc_ref[...]
        o_ref[...] = _layer_norm(y, lg_ref[...], lb_ref[...])


def _ffn(x1, scale, shift, gate, w_in, w_out, ln_g, ln_b, bm=512, tf=512):
    m, d = x1.shape
    nf = D_FF // tf
    per_seq = SEQ // bm
    row = pl.BlockSpec((bm, d), lambda i, j: (i, 0))
    mod_spec = pl.BlockSpec((1, 1, d), lambda i, j: (i // per_seq, 0, 0))
    vec = pl.BlockSpec((1, d), lambda i, j: (0, 0))
    return pl.pallas_call(
        _ffn_kernel,
        grid=(m // bm, nf),
        in_specs=[row, mod_spec, mod_spec, mod_spec,
                  pl.BlockSpec((d, tf), lambda i, j: (0, j)),
                  pl.BlockSpec((d, tf), lambda i, j: (0, j + nf)),
                  pl.BlockSpec((tf, d), lambda i, j: (j, 0)), vec, vec],
        out_specs=row,
        out_shape=jax.ShapeDtypeStruct((m, d), F32),
        scratch_shapes=[pltpu.VMEM((bm, d), BF16), pltpu.VMEM((bm, d), F32)],
        compiler_params=_params(2),
        name="swiglu_ffn_ln",
    )(x1, scale, shift, gate, w_in, w_in, w_out, ln_g, ln_b)


def _rot_last(w):
    return jnp.concatenate([-w[..., HALF:], w[..., :HALF]], axis=-1)


def _pad_lanes(w):
    return jnp.pad(w, [(0, 0)] * (w.ndim - 1) + [(0, LANES - w.shape[-1])])


def kernel(x, c, positions, w_ada, b_ada, w_in, g_q_a, w_q_b, g_kv_a, w_kv_b, w_o_a,
           w_conv, w_o_b, w_o, ln1_g, ln1_b, w_ffn_in, w_ffn_out, ln2_g, ln2_b):
    bsz, seq, d = x.shape
    assert (seq, d) == (SEQ, D_MODEL) and w_ada.shape[0] == DEPTH
    m = bsz * seq
    x_cur = x.reshape(m, d)

    inv_freq = 1.0 / (ROPE_THETA ** (jnp.arange(0, QK_ROPE, 2, dtype=F32) / QK_ROPE))
    inv_row = jnp.tile(inv_freq, LANES // HALF).reshape(1, LANES)
    pos_b = jnp.broadcast_to(positions.reshape(m, 1), (m, LANES))
    c_pad = jnp.pad(c, ((0, SUBLANES - bsz), (0, 0)))

    for l in range(DEPTH):
        mod = _ada(c_pad, w_ada[l], b_ada[l].reshape(1, -1))[:bsz]
        shift1, scale1, gate1, shift2, scale2, gate2 = [
            t.reshape(bsz, 1, d) for t in jnp.split(mod, 6, axis=-1)]

        o_kr = Q_LORA + KV_LORA
        o_cb = o_kr + QK_ROPE
        w_tf = jnp.swapaxes(w_in[l], 0, 1)
        w_kr_rows = w_tf[o_kr:o_cb]
        w_rot_rows = jnp.concatenate([-w_kr_rows[HALF:], w_kr_rows[:HALF]], axis=0)
        lane_pad = jnp.zeros((LANES - QK_ROPE, d), F32)
        w_kr_t = jnp.concatenate([w_kr_rows, lane_pad, w_rot_rows, lane_pad], axis=0)

        wq = w_q_b[l].reshape(Q_LORA, N_HEADS, QK_NOPE + QK_ROPE)
        wq_rope = wq[:, :, QK_NOPE:]
        w_q = jnp.concatenate([
            wq[:, :, :QK_NOPE].reshape(Q_LORA, HEAD_PAD),
            _pad_lanes(wq_rope).reshape(Q_LORA, HEAD_PAD),
            _pad_lanes(_rot_last(wq_rope)).reshape(Q_LORA, HEAD_PAD)], axis=1).astype(BF16)
        wkv = w_kv_b[l].reshape(KV_LORA, N_HEADS, QK_NOPE + V_HEAD)
        w_k = wkv[:, :, :QK_NOPE].reshape(KV_LORA, HEAD_PAD).astype(BF16)
        w_vt = jnp.swapaxes(wkv[:, :, QK_NOPE:].reshape(KV_LORA, HEAD_PAD), 0, 1).astype(BF16)

        qn, kvn, kr_raw = _latent(x_cur, scale1, shift1, w_tf, w_kr_t,
                                  g_q_a[l].reshape(1, -1), g_kv_a[l].reshape(1, -1))
        q_nope, q_rope, k_nope, v_t, k_rope = _qkv_up(qn, kvn, kr_raw, pos_b, inv_row,
                                                      w_q, w_k, w_vt)
        attn, (w_cg, w_oa, w_ob, w_om) = _attention(
            q_nope, q_rope, k_nope, k_rope, v_t,
            [(w_tf, o_cb), (w_o_a[l], 0), (w_o_b[l], 0), (w_o[l], 0)])
        cm = _conv_branch(x_cur, scale1, shift1, w_cg, w_conv[l])
        gates, (w_fi,) = _gates(x_cur, scale1, shift1, w_cg, 3, [(w_ffn_in[l], 0)])
        merged, (w_fo,) = _merge(attn, cm, w_oa, w_ob, gates, [(w_ffn_out[l], 0)])
        x_cur = _mix_out(merged, w_om, x_cur, gate1,
                         ln1_g[l].reshape(1, -1), ln1_b[l].reshape(1, -1))

        x_cur = _ffn(x_cur, scale2, shift2, gate2, w_fi, w_fo,
                     ln2_g[l].reshape(1, -1), ln2_b[l].reshape(1, -1))

    return x_cur.reshape(bsz, seq, d)
```

```python
import functools
import math

import jax
import jax.numpy as jnp
from jax import lax
from jax.experimental import pallas as pl
from jax.experimental.pallas import tpu as pltpu

F32 = jnp.float32
BF16 = jnp.bfloat16

D_MODEL = 2048
SEQ = 2048
CHUNK = 64
N_HEADS = 16
QK_NOPE = 128
QK_ROPE = 64
V_HEAD = 128
Q_LORA = 512
KV_LORA = 512
ROPE_THETA = 10000.0
ATTN_SCALE = (QK_NOPE + QK_ROPE) ** -0.5
CONV_K = 3
D_FF = 5632
DEPTH = 1
DEEPNORM_ALPHA = (2.0 * DEPTH) ** 0.25
LN_EPS = 1e-5
RMS_EPS = 1e-6

LANES = 128
SUBLANES = 8
HALF = QK_ROPE // 2
HEAD_PAD = N_HEADS * LANES
ATTN_TQ = 256
ATTN_LOOKAHEAD = 4
ATTN_HEADS_PER_STEP = 2
CAST_PIECE_ROWS = 32
UP_HEAD_GROUP = 4
ROW_CHUNK = 256
COL_CHUNK = 512
Q_PRESCALE = ATTN_SCALE * math.log2(math.e)

VMEM_LIMIT = 52 * 1024 * 1024


def _params(n_axes, vmem=VMEM_LIMIT):
    return pltpu.CompilerParams(
        dimension_semantics=("arbitrary",) * n_axes, vmem_limit_bytes=vmem)


def _resident(shape, index_map):
    return pl.BlockSpec(shape, index_map, pipeline_mode=pl.Buffered(1))


def _slab_specs(w, n_steps, step_of, row0=0):
    n_rows = w.shape[0] - row0
    cols = w.shape[1]
    bf16_rows = 2 * SUBLANES
    if n_rows % (n_steps * bf16_rows) == 0:
        rb = n_rows // n_steps
        in_spec = pl.BlockSpec(
            (pl.Element(rb), pl.Element(cols)),
            lambda *g: (pl.multiple_of(row0 + step_of(*g) * rb, SUBLANES), 0))
        out_spec = pl.BlockSpec((rb, cols), lambda *g: (step_of(*g), 0))
    else:
        assert row0 == 0 and n_rows % (n_steps // 2 * bf16_rows) == 0
        rb = n_rows // (n_steps // 2)
        in_spec = out_spec = pl.BlockSpec(
            (rb, cols // 2), lambda *g: (step_of(*g) // 2, step_of(*g) % 2))
    return in_spec, out_spec, jax.ShapeDtypeStruct((n_rows, cols), BF16)


def _split_refs(refs, n_in, n_cast):
    return refs[:n_in], refs[n_in:n_in + n_cast], refs[n_in + n_cast], refs[n_in + n_cast + 1:]


def _cast_pieces(cast_src, cast_dst):
    pieces = []
    for src, dst in zip(cast_src, cast_dst, strict=True):
        rows = src.shape[0]
        n_piece = next(n for n in range(max(1, rows // CAST_PIECE_ROWS), 0, -1)
                       if rows % n == 0 and (rows // n) % (2 * SUBLANES) == 0)
        pieces += [(src, dst, slice(p * rows // n_piece, (p + 1) * rows // n_piece))
                   for p in range(n_piece)]
    return pieces


def _convert_pieces(pieces, k, n):
    for src, dst, rows in pieces[k::n]:
        dst[rows, :] = src[rows, :].astype(BF16)


def _dot(a, b):
    return jnp.dot(a, b, preferred_element_type=F32)


def _dot_nt(a, b_t):
    return lax.dot_general(a, b_t, (((1,), (1,)), ((), ())), preferred_element_type=F32)


def _modulate(x, scale, shift):
    return (x * (1.0 + scale) + shift).astype(BF16)


def _layer_norm(y, g, b):
    mu = jnp.mean(y, axis=-1, keepdims=True)
    yc = y - mu
    var = jnp.mean(yc * yc, axis=-1, keepdims=True)
    return yc * lax.rsqrt(var + LN_EPS) * g + b


def _rms_norm(y, g):
    return y * lax.rsqrt(jnp.mean(y * y, axis=-1, keepdims=True) + RMS_EPS) * g


def _ada_kernel(c_ref, w_ref, b_ref, o_ref):
    c = c_ref[...]
    c_act = (c * jax.nn.sigmoid(c)).astype(BF16)
    o_ref[...] = _dot(c_act, w_ref[...].astype(BF16)) + b_ref[...]


def _ada(c_pad, w_ada, b_ada, tn=1024):
    rows, d = c_pad.shape
    n = w_ada.shape[1]
    return pl.pallas_call(
        _ada_kernel,
        grid=(n // tn,),
        in_specs=[pl.BlockSpec((rows, d), lambda j: (0, 0)),
                  pl.BlockSpec((d, tn), lambda j: (0, j)),
                  pl.BlockSpec((1, tn), lambda j: (0, j))],
        out_specs=pl.BlockSpec((rows, tn), lambda j: (0, j)),
        out_shape=jax.ShapeDtypeStruct((rows, n), F32),
        compiler_params=_params(1),
        name="ada_mod",
    )(c_pad, w_ada, b_ada)


def _latent_kernel(x_ref, sc_ref, sh_ref, w_ref, wkr_ref, gq_ref, gkv_ref,
                   qn_ref, kvn_ref, kr_ref, wbf_ref):
    @pl.when(pl.program_id(0) == 0)
    def _():
        wbf_ref[...] = w_ref[...].astype(BF16)

    u = _modulate(x_ref[...], sc_ref[0], sh_ref[0])
    p = _dot_nt(u, wbf_ref[...])
    qn_ref[...] = _rms_norm(p[:, :Q_LORA], gq_ref[...]).astype(BF16)
    kvn_ref[...] = _rms_norm(p[:, Q_LORA:], gkv_ref[...]).astype(BF16)
    kr_ref[...] = _dot_nt(u, wkr_ref[...].astype(BF16))


def _latent(x2, scale, shift, w_t, w_kr_t, g_q, g_kv, bm=512):
    m, d = x2.shape
    per_seq = SEQ // bm
    n_lat = Q_LORA + KV_LORA
    mod_spec = pl.BlockSpec((1, 1, d), lambda i: (i // per_seq, 0, 0))
    return pl.pallas_call(
        _latent_kernel,
        grid=(m // bm,),
        in_specs=[pl.BlockSpec((bm, d), lambda i: (i, 0)),
                  mod_spec, mod_spec,
                  _resident((n_lat, d), lambda i: (0, 0)),
                  pl.BlockSpec((2 * LANES, d), lambda i: (0, 0)),
                  pl.BlockSpec((1, Q_LORA), lambda i: (0, 0)),
                  pl.BlockSpec((1, KV_LORA), lambda i: (0, 0))],
        out_specs=[pl.BlockSpec((bm, Q_LORA), lambda i: (i, 0)),
                   pl.BlockSpec((bm, KV_LORA), lambda i: (i, 0)),
                   pl.BlockSpec((bm, 2 * LANES), lambda i: (i, 0))],
        out_shape=[jax.ShapeDtypeStruct((m, Q_LORA), BF16),
                   jax.ShapeDtypeStruct((m, KV_LORA), BF16),
                   jax.ShapeDtypeStruct((m, 2 * LANES), F32)],
        scratch_shapes=[pltpu.VMEM((n_lat, d), BF16)],
        compiler_params=_params(1),
        name="latent_proj",
    )(x2, scale, shift, w_t, w_kr_t, g_q, g_kv)


def _qkv_up_kernel(qn_ref, kvn_ref, krr_ref, pos_ref, inv_ref, wq_ref, wk_ref, wvt_ref,
                   qnope_ref, qrope_ref, knope_ref, vt_ref, kr_ref):
    ang = pos_ref[...].astype(F32) * inv_ref[...]
    cos = jnp.cos(ang)
    sin = jnp.sin(ang)
    qn = qn_ref[...]
    kvn = kvn_ref[...]
    grp = UP_HEAD_GROUP * LANES
    lane = lax.broadcasted_iota(jnp.int32, cos.shape, 1)
    cs = jnp.where(lane < QK_ROPE, cos, sin) * Q_PRESCALE
    for g in range(N_HEADS // UP_HEAD_GROUP):
        lo, hi = g * grp, (g + 1) * grp
        qnope_ref[:, lo:hi] = (_dot(qn, wq_ref[:, lo:hi]) * Q_PRESCALE).astype(BF16)
        p = _dot(qn, wq_ref[:, HEAD_PAD + lo:HEAD_PAD + hi])
        for h in range(UP_HEAD_GROUP):
            t = p[:, h * LANES:(h + 1) * LANES] * cs
            qrope_ref[:, lo + h * LANES:lo + (h + 1) * LANES] = (
                t + pltpu.roll(t, QK_ROPE, 1)).astype(BF16)
        knope_ref[:, lo:hi] = _dot(kvn, wk_ref[:, lo:hi]).astype(BF16)
        vt_ref[lo:hi, :] = _dot_nt(wvt_ref[lo:hi, :], kvn).astype(BF16)
    krr = krr_ref[...]
    kr_ref[...] = (krr[:, :LANES] * cos + krr[:, LANES:] * sin).astype(BF16)


def _qkv_up(qn, kvn, kr_raw, pos_b, inv_row, w_q, w_k, w_vt, bm=512):
    m = qn.shape[0]
    row = lambda n: pl.BlockSpec((bm, n), lambda i: (i, 0))
    const = lambda a: pl.BlockSpec(a.shape, lambda i: (0, 0))
    return pl.pallas_call(
        _qkv_up_kernel,
        grid=(m // bm,),
        in_specs=[row(Q_LORA), row(KV_LORA), row(2 * LANES), row(LANES),
                  const(inv_row), const(w_q), const(w_k), const(w_vt)],
        out_specs=[row(HEAD_PAD), row(HEAD_PAD), row(HEAD_PAD),
                   pl.BlockSpec((HEAD_PAD, bm), lambda i: (0, i)), row(LANES)],
        out_shape=[jax.ShapeDtypeStruct((m, HEAD_PAD), BF16)] * 3
        + [jax.ShapeDtypeStruct((HEAD_PAD, m), BF16), jax.ShapeDtypeStruct((m, LANES), BF16)],
        compiler_params=_params(1),
        name="qkv_up",
    )(qn, kvn, kr_raw, pos_b, inv_row, w_q, w_k, w_vt)


def _conv_kernel(x_ref, sc_ref, sh_ref, wb_ref, wc_ref, wx_ref, wconv_ref, o_ref,
                 u_ref, halo_ref, *, tiles_per_seq):
    i = pl.program_id(0)
    j = pl.program_id(1)

    @pl.when(j == 0)
    def _():
        u_ref[...] = _modulate(x_ref[...], sc_ref[0], sh_ref[0])

    u = u_ref[...]
    cb = _dot_nt(u, wb_ref[...])
    z = _dot_nt(u, wc_ref[...]) * _dot_nt(u, wx_ref[...])
    bm = z.shape[0]

    @pl.when(i % tiles_per_seq == 0)
    def _():
        halo_ref[j] = jnp.zeros(halo_ref.shape[1:], F32)

    prev = halo_ref[j]
    rows = lax.broadcasted_iota(jnp.int32, z.shape, 0)
    z1 = jnp.where(rows == 0, prev[SUBLANES - 1:SUBLANES, :], pltpu.roll(z, 1, 0))
    z2 = jnp.where(rows == 0, prev[SUBLANES - 2:SUBLANES - 1, :],
                   jnp.where(rows == 1, prev[SUBLANES - 1:SUBLANES, :], pltpu.roll(z, 2, 0)))
    halo_ref[j] = z[bm - SUBLANES:, :]
    w = wconv_ref[...]
    o_ref[...] = (cb * (w[0:1, :] * z2 + w[1:2, :] * z1 + w[2:3, :] * z)).astype(BF16)


def _conv_branch(x2, scale, shift, w_rows, w_conv, bm=1024, tn=512):
    m, d = x2.shape
    n = w_conv.shape[1]
    per_seq = SEQ // bm
    mod_spec = pl.BlockSpec((1, 1, d), lambda i, j: (i // per_seq, 0, 0))
    w_spec = lambda piece: pl.BlockSpec((tn, d), lambda i, j: (piece * (n // tn) + j, 0))
    return pl.pallas_call(
        functools.partial(_conv_kernel, tiles_per_seq=per_seq),
        grid=(m // bm, n // tn),
        in_specs=[pl.BlockSpec((bm, d), lambda i, j: (i, 0)), mod_spec, mod_spec,
                  w_spec(0), w_spec(1), w_spec(2),
                  pl.BlockSpec((CONV_K, tn), lambda i, j: (0, j))],
        out_specs=pl.BlockSpec((bm, tn), lambda i, j: (i, j)),
        out_shape=jax.ShapeDtypeStruct((m, n), BF16),
        scratch_shapes=[pltpu.VMEM((bm, d), BF16),
                        pltpu.VMEM((n // tn, SUBLANES, tn), F32)],
        compiler_params=_params(2),
        name="conv_branch",
    )(x2, scale, shift, w_rows, w_rows, w_rows, w_conv)


def _gate_kernel(*refs, n_cast):
    (x_ref, sc_ref, sh_ref, wa_ref, wb_ref), cast_src, o_ref, cast_dst = _split_refs(refs, 5, n_cast)
    pieces = _cast_pieces(cast_src, cast_dst)
    u = _modulate(x_ref[...], sc_ref[0], sh_ref[0])
    n = wa_ref.shape[0]
    passes = [(g, c0) for g in range(2) for c0 in range(0, n, 2 * COL_CHUNK)]
    for k, (g, c0) in enumerate(passes):
        w_ref = (wa_ref, wb_ref)[g]
        o_ref[:, g * n + c0:g * n + c0 + 2 * COL_CHUNK] = jax.nn.sigmoid(
            _dot_nt(u, w_ref[c0:c0 + 2 * COL_CHUNK, :])).astype(BF16)
        _convert_pieces(pieces, k, len(passes))


def _gates(x2, scale, shift, w_rows, first_block, cast_jobs, bm=512):
    m, d = x2.shape
    per_seq = SEQ // bm
    mod_spec = pl.BlockSpec((1, 1, d), lambda i: (i // per_seq, 0, 0))
    specs = [_slab_specs(w, m // bm, lambda i: i, row0) for w, row0 in cast_jobs]
    outs = pl.pallas_call(
        functools.partial(_gate_kernel, n_cast=len(cast_jobs)),
        grid=(m // bm,),
        in_specs=[pl.BlockSpec((bm, d), lambda i: (i, 0)), mod_spec, mod_spec,
                  _resident((d, d), lambda i: (first_block, 0)),
                  _resident((d, d), lambda i: (first_block + 1, 0))] + [s[0] for s in specs],
        out_specs=[pl.BlockSpec((bm, 2 * d), lambda i: (i, 0))] + [s[1] for s in specs],
        out_shape=[jax.ShapeDtypeStruct((m, 2 * d), BF16)] + [s[2] for s in specs],
        compiler_params=_params(1),
        name="merge_gates",
    )(x2, scale, shift, w_rows, w_rows, *[w for w, _ in cast_jobs])
    return outs[0], outs[1:]


def _attn_kernel(*refs, n_cast):
    (qn_ref, qr_ref, kn_ref, kr_ref, vt_ref), cast_src, o_ref, cast_dst = _split_refs(refs, 5, n_cast)
    pieces = _cast_pieces(cast_src, cast_dst)
    kr = kr_ref[...]
    ks = [jnp.concatenate([kn_ref[:, a * LANES:(a + 1) * LANES], kr], axis=1)
          for a in range(ATTN_HEADS_PER_STEP)]
    key = lax.broadcasted_iota(jnp.int32, (ATTN_TQ, ATTN_TQ), 0)
    qry = lax.broadcasted_iota(jnp.int32, (ATTN_TQ, ATTN_TQ), 1)
    allowed = (key // CHUNK) <= (qry // CHUNK)
    n_blk = SEQ // ATTN_TQ

    def scores(task):
        a, i = task
        q0 = i * ATTN_TQ
        L = q0 + ATTN_TQ
        lanes = slice(a * LANES, (a + 1) * LANES)
        q = jnp.concatenate([qn_ref[q0:L, lanes], qr_ref[q0:L, lanes]], axis=1)
        return _dot_nt(ks[a][:L], q)

    tasks = [(a, i) for i in range(n_blk - 1, -1, -1) for a in range(ATTN_HEADS_PER_STEP)]
    pending = [scores(t) for t in tasks[:ATTN_LOOKAHEAD]]
    for pos, (a, i) in enumerate(tasks):
        q0 = i * ATTN_TQ
        L = q0 + ATTN_TQ
        s_t = pending.pop(0)
        if pos + ATTN_LOOKAHEAD < len(tasks):
            pending.append(scores(tasks[pos + ATTN_LOOKAHEAD]))
        s_diag = jnp.where(allowed, s_t[q0:, :], -1e30)
        s_t = s_diag if i == 0 else jnp.concatenate([s_t[:q0, :], s_diag], axis=0)
        e = jnp.exp2(s_t - jnp.max(s_t, axis=0, keepdims=True))
        denom = jnp.sum(e, axis=0, keepdims=True)
        o_t = _dot(vt_ref[a * V_HEAD:(a + 1) * V_HEAD, :L], e.astype(BF16))
        o_ref[q0:L, a * LANES:(a + 1) * LANES] = (o_t / denom).T.astype(BF16)
        _convert_pieces(pieces, pos, len(tasks))


def _attention(q_nope, q_rope, k_nope, k_rope, v_t, cast_jobs):
    m = q_nope.shape[0]
    hps = ATTN_HEADS_PER_STEP
    grid = (m // SEQ, N_HEADS // hps)
    head = pl.BlockSpec((SEQ, hps * LANES), lambda b, h: (b, h))
    specs = [_slab_specs(w, grid[0] * grid[1], lambda b, h: b * grid[1] + h, row0)
             for w, row0 in cast_jobs]
    outs = pl.pallas_call(
        functools.partial(_attn_kernel, n_cast=len(cast_jobs)),
        grid=grid,
        in_specs=[head, head, head, pl.BlockSpec((SEQ, LANES), lambda b, h: (b, 0)),
                  pl.BlockSpec((hps * V_HEAD, SEQ), lambda b, h: (h, b))] + [s[0] for s in specs],
        out_specs=[head] + [s[1] for s in specs],
        out_shape=[jax.ShapeDtypeStruct((m, HEAD_PAD), BF16)] + [s[2] for s in specs],
        compiler_params=_params(2),
        name="mla_attention",
    )(q_nope, q_rope, k_nope, k_rope, v_t, *[w for w, _ in cast_jobs])
    return outs[0], outs[1:]


def _merge_kernel(*refs, n_cast):
    (a_ref, c_ref, wa_ref, wb_ref, g_ref), cast_src, o_ref, cast_dst = _split_refs(refs, 5, n_cast)
    pieces = _cast_pieces(cast_src, cast_dst)
    a = a_ref[...]
    cm = c_ref[...]
    n = o_ref.shape[1]
    for k, c0 in enumerate(range(0, n, COL_CHUNK)):
        cols = slice(c0, c0 + COL_CHUNK)
        y_a = _dot(a, wa_ref[:, cols])
        y_b = _dot(cm, wb_ref[:, cols])
        o_ref[:, cols] = (g_ref[:, cols].astype(F32) * y_a
                          + g_ref[:, n + c0:n + c0 + COL_CHUNK].astype(F32) * y_b).astype(BF16)
        _convert_pieces(pieces, k, n // COL_CHUNK)


def _merge(attn, cm, w_oa, w_ob, gates, cast_jobs, bm=512):
    m, d = attn.shape
    n = w_oa.shape[1]
    lhs = pl.BlockSpec((bm, d), lambda i: (i, 0))
    specs = [_slab_specs(w, m // bm, lambda i: i, row0) for w, row0 in cast_jobs]
    outs = pl.pallas_call(
        functools.partial(_merge_kernel, n_cast=len(cast_jobs)),
        grid=(m // bm,),
        in_specs=[lhs, lhs, _resident((d, n), lambda i: (0, 0)), _resident((d, n), lambda i: (0, 0)),
                  pl.BlockSpec((bm, 2 * n), lambda i: (i, 0))] + [s[0] for s in specs],
        out_specs=[pl.BlockSpec((bm, n), lambda i: (i, 0))] + [s[1] for s in specs],
        out_shape=[jax.ShapeDtypeStruct((m, n), BF16)] + [s[2] for s in specs],
        compiler_params=_params(1),
        name="branch_merge",
    )(attn, cm, w_oa, w_ob, gates, *[w for w, _ in cast_jobs])
    return outs[0], outs[1:]


def _mix_out_kernel(mg_ref, w_ref, x_ref, g1_ref, lg_ref, lb_ref, o_ref):
    n_chunks = mg_ref.shape[0] // ROW_CHUNK
    mix = _dot(mg_ref[0:ROW_CHUNK, :], w_ref[...])
    for r in range(n_chunks):
        rows = slice(r * ROW_CHUNK, (r + 1) * ROW_CHUNK)
        cur = mix
        if r + 1 < n_chunks:
            mix = _dot(mg_ref[(r + 1) * ROW_CHUNK:(r + 2) * ROW_CHUNK, :], w_ref[...])
        y = DEEPNORM_ALPHA * x_ref[rows, :] + g1_ref[0] * cur
        o_ref[rows, :] = _layer_norm(y, lg_ref[...], lb_ref[...])


def _mix_out(merged, w_o, x2, gate1, ln_g, ln_b, bm=512):
    m, d = x2.shape
    per_seq = SEQ // bm
    row = pl.BlockSpec((bm, d), lambda i: (i, 0))
    vec = pl.BlockSpec((1, d), lambda i: (0, 0))
    return pl.pallas_call(
        _mix_out_kernel,
        grid=(m // bm,),
        in_specs=[row, pl.BlockSpec((d, d), lambda i: (0, 0)), row,
                  pl.BlockSpec((1, 1, d), lambda i: (i // per_seq, 0, 0)), vec, vec],
        out_specs=row,
        out_shape=jax.ShapeDtypeStruct((m, d), F32),
        compiler_params=_params(1),
        name="mix_out_ln",
    )(merged, w_o, x2, gate1, ln_g, ln_b)


def _ffn_kernel(x_ref, sc_ref, sh_ref, g2_ref, wg_ref, wu_ref, wo_ref, lg_ref, lb_ref, o_ref,
                u_ref, acc_ref):
    j = pl.program_id(1)

    @pl.when(j == 0)
    def _():
        u_ref[...] = _modulate(x_ref[...], sc_ref[0], sh_ref[0])
        acc_ref[...] = jnp.zeros_like(acc_ref)

    u = u_ref[...]
    tf = wg_ref.shape[1]
    half = tf // 2
    hs = []
    for c0 in (0, half):
        hg = _dot(u, wg_ref[:, c0:c0 + half])
        hs.append((hg * jax.nn.sigmoid(hg) * _dot(u, wu_ref[:, c0:c0 + half])).astype(BF16))
    acc_ref[...] += _dot(hs[0], wo_ref[:half, :]) + _dot(hs[1], wo_ref[half:, :])

    @pl.when(j == pl.num_programs(1) - 1)
    def _():
        y = DEEPNORM_ALPHA * x_ref[...] + g2_ref[0] * acc_ref[...]
        o_ref[...] = _layer_norm(y, lg_ref[...], lb_ref[...])


def _ffn(x1, scale, shift, gate, w_in, w_out, ln_g, ln_b, bm=512, tf=512):
    m, d = x1.shape
    nf = D_FF // tf
    per_seq = SEQ // bm
    row = pl.BlockSpec((bm, d), lambda i, j: (i, 0))
    mod_spec = pl.BlockSpec((1, 1, d), lambda i, j: (i // per_seq, 0, 0))
    vec = pl.BlockSpec((1, d), lambda i, j: (0, 0))
    return pl.pallas_call(
        _ffn_kernel,
        grid=(m // bm, nf),
        in_specs=[row, mod_spec, mod_spec, mod_spec,
                  pl.BlockSpec((d, tf), lambda i, j: (0, j)),
                  pl.BlockSpec((d, tf), lambda i, j: (0, j + nf)),
                  pl.BlockSpec((tf, d), lambda i, j: (j, 0)), vec, vec],
        out_specs=row,
        out_shape=jax.ShapeDtypeStruct((m, d), F32),
        scratch_shapes=[pltpu.VMEM((bm, d), BF16), pltpu.VMEM((bm, d), F32)],
        compiler_params=_params(2),
        name="swiglu_ffn_ln",
    )(x1, scale, shift, gate, w_in, w_in, w_out, ln_g, ln_b)


def _rot_last(w):
    return jnp.concatenate([-w[..., HALF:], w[..., :HALF]], axis=-1)


def kernel(x, c, positions, w_ada, b_ada, w_in, g_q_a, w_q_b, g_kv_a, w_kv_b, w_o_a,
           w_conv, w_o_b, w_o, ln1_g, ln1_b, w_ffn_in, w_ffn_out, ln2_g, ln2_b):
    bsz, seq, d = x.shape
    assert (seq, d) == (SEQ, D_MODEL) and w_ada.shape[0] == DEPTH
    m = bsz * seq
    x_cur = x.reshape(m, d)

    inv_freq = 1.0 / (ROPE_THETA ** (jnp.arange(0, QK_ROPE, 2, dtype=F32) / QK_ROPE))
    inv_row = jnp.tile(inv_freq, LANES // HALF).reshape(1, LANES)
    pos_b = jnp.broadcast_to(positions.reshape(m, 1), (m, LANES))
    c_pad = jnp.pad(c, ((0, SUBLANES - bsz), (0, 0)))

    for l in range(DEPTH):
        mod = _ada(c_pad, w_ada[l], b_ada[l].reshape(1, -1))[:bsz]
        shift1, scale1, gate1, shift2, scale2, gate2 = [
            t.reshape(bsz, 1, d) for t in jnp.split(mod, 6, axis=-1)]

        o_kr = Q_LORA + KV_LORA
        o_cb = o_kr + QK_ROPE
        w_tf = jnp.swapaxes(w_in[l], 0, 1)
        w_kr_rows = w_tf[o_kr:o_cb]
        w_rot_rows = jnp.concatenate([-w_kr_rows[HALF:], w_kr_rows[:HALF]], axis=0)
        lane_pad = jnp.zeros((LANES - QK_ROPE, d), F32)
        w_kr_t = jnp.concatenate([w_kr_rows, lane_pad, w_rot_rows, lane_pad], axis=0)

        wq = w_q_b[l].reshape(Q_LORA, N_HEADS, QK_NOPE + QK_ROPE)
        wq_rope = wq[:, :, QK_NOPE:]
        w_q = jnp.concatenate([
            wq[:, :, :QK_NOPE].reshape(Q_LORA, HEAD_PAD),
            jnp.concatenate([wq_rope, _rot_last(wq_rope)], axis=-1).reshape(Q_LORA, HEAD_PAD)],
            axis=1).astype(BF16)
        wkv = w_kv_b[l].reshape(KV_LORA, N_HEADS, QK_NOPE + V_HEAD)
        w_k = wkv[:, :, :QK_NOPE].reshape(KV_LORA, HEAD_PAD).astype(BF16)
        w_vt = jnp.swapaxes(wkv[:, :, QK_NOPE:].reshape(KV_LORA, HEAD_PAD), 0, 1).astype(BF16)

        qn, kvn, kr_raw = _latent(x_cur, scale1, shift1, w_tf, w_kr_t,
                                  g_q_a[l].reshape(1, -1), g_kv_a[l].reshape(1, -1))
        q_nope, q_rope, k_nope, v_t, k_rope = _qkv_up(qn, kvn, kr_raw, pos_b, inv_row,
                                                      w_q, w_k, w_vt)
        attn, (w_cg, w_oa, w_ob, w_om) = _attention(
            q_nope, q_rope, k_nope, k_rope, v_t,
            [(w_tf, o_cb), (w_o_a[l], 0), (w_o_b[l], 0), (w_o[l], 0)])
        cm = _conv_branch(x_cur, scale1, shift1, w_cg, w_conv[l])
        gates, (w_fi,) = _gates(x_cur, scale1, shift1, w_cg, 3, [(w_ffn_in[l], 0)])
        merged, (w_fo,) = _merge(attn, cm, w_oa, w_ob, gates, [(w_ffn_out[l], 0)])
        x_cur = _mix_out(merged, w_om, x_cur, gate1,
                         ln1_g[l].reshape(1, -1), ln1_b[l].reshape(1, -1))

        x_cur = _ffn(x_cur, scale2, shift2, gate2, w_fi, w_fo,
                     ln2_g[l].reshape(1, -1), ln2_b[l].reshape(1, -1))

    return x_cur.reshape(bsz, seq, d)
```

```python
import functools
import math

import jax
import jax.numpy as jnp
from jax import lax
from jax.experimental import pallas as pl
from jax.experimental.pallas import tpu as pltpu

F32 = jnp.float32
BF16 = jnp.bfloat16

D_MODEL = 2048
SEQ = 2048
CHUNK = 64
N_HEADS = 16
QK_NOPE = 128
QK_ROPE = 64
V_HEAD = 128
Q_LORA = 512
KV_LORA = 512
ROPE_THETA = 10000.0
ATTN_SCALE = (QK_NOPE + QK_ROPE) ** -0.5
CONV_K = 3
D_FF = 5632
DEPTH = 1
DEEPNORM_ALPHA = (2.0 * DEPTH) ** 0.25
LN_EPS = 1e-5
RMS_EPS = 1e-6

LANES = 128
SUBLANES = 8
HALF = QK_ROPE // 2
HEAD_PAD = N_HEADS * LANES
ATTN_TQ = 256
ATTN_LOOKAHEAD = 4
ATTN_HEADS_PER_STEP = 2
CAST_PIECE_ROWS = 32
UP_HEAD_GROUP = 4
ROW_CHUNK = 256
COL_CHUNK = 512
Q_PRESCALE = ATTN_SCALE * math.log2(math.e)

VMEM_LIMIT = 52 * 1024 * 1024
FFN_VMEM_LIMIT = 62 * 1024 * 1024


def _params(n_axes, vmem=VMEM_LIMIT):
    return pltpu.CompilerParams(
        dimension_semantics=("arbitrary",) * n_axes, vmem_limit_bytes=vmem)


def _resident(shape, index_map):
    return pl.BlockSpec(shape, index_map, pipeline_mode=pl.Buffered(1))


def _slab_specs(w, n_steps, step_of, row0=0):
    n_rows = w.shape[0] - row0
    cols = w.shape[1]
    bf16_rows = 2 * SUBLANES
    if n_rows % (n_steps * bf16_rows) == 0:
        rb = n_rows // n_steps
        in_spec = pl.BlockSpec(
            (pl.Element(rb), pl.Element(cols)),
            lambda *g: (pl.multiple_of(row0 + step_of(*g) * rb, SUBLANES), 0))
        out_spec = pl.BlockSpec((rb, cols), lambda *g: (step_of(*g), 0))
    else:
        assert row0 == 0 and n_rows % (n_steps // 2 * bf16_rows) == 0
        rb = n_rows // (n_steps // 2)
        in_spec = out_spec = pl.BlockSpec(
            (rb, cols // 2), lambda *g: (step_of(*g) // 2, step_of(*g) % 2))
    return in_spec, out_spec, jax.ShapeDtypeStruct((n_rows, cols), BF16)


def _split_refs(refs, n_in, n_cast):
    return refs[:n_in], refs[n_in:n_in + n_cast], refs[n_in + n_cast], refs[n_in + n_cast + 1:]


def _cast_pieces(cast_src, cast_dst):
    pieces = []
    for src, dst in zip(cast_src, cast_dst, strict=True):
        rows = src.shape[0]
        n_piece = next(n for n in range(max(1, rows // CAST_PIECE_ROWS), 0, -1)
                       if rows % n == 0 and (rows // n) % (2 * SUBLANES) == 0)
        pieces += [(src, dst, slice(p * rows // n_piece, (p + 1) * rows // n_piece))
                   for p in range(n_piece)]
    return pieces


def _convert_pieces(pieces, k, n):
    for src, dst, rows in pieces[k::n]:
        dst[rows, :] = src[rows, :].astype(BF16)


def _dot(a, b):
    return jnp.dot(a, b, preferred_element_type=F32)


def _dot_nt(a, b_t):
    return lax.dot_general(a, b_t, (((1,), (1,)), ((), ())), preferred_element_type=F32)


def _modulate(x, scale, shift):
    return (x * (1.0 + scale) + shift).astype(BF16)


def _layer_norm(y, g, b):
    mu = jnp.mean(y, axis=-1, keepdims=True)
    yc = y - mu
    var = jnp.mean(yc * yc, axis=-1, keepdims=True)
    return yc * lax.rsqrt(var + LN_EPS) * g + b


def _rms_norm(y, g):
    return y * lax.rsqrt(jnp.mean(y * y, axis=-1, keepdims=True) + RMS_EPS) * g


def _ada_kernel(c_ref, w_ref, b_ref, o_ref):
    c = c_ref[...]
    c_act = (c * jax.nn.sigmoid(c)).astype(BF16)
    o_ref[...] = _dot(c_act, w_ref[...].astype(BF16)) + b_ref[...]


def _ada(c_pad, w_ada, b_ada, tn=1024):
    rows, d = c_pad.shape
    n = w_ada.shape[1]
    return pl.pallas_call(
        _ada_kernel,
        grid=(n // tn,),
        in_specs=[pl.BlockSpec((rows, d), lambda j: (0, 0)),
                  pl.BlockSpec((d, tn), lambda j: (0, j)),
                  pl.BlockSpec((1, tn), lambda j: (0, j))],
        out_specs=pl.BlockSpec((rows, tn), lambda j: (0, j)),
        out_shape=jax.ShapeDtypeStruct((rows, n), F32),
        compiler_params=_params(1),
        name="ada_mod",
    )(c_pad, w_ada, b_ada)


def _latent_kernel(x_ref, sc_ref, sh_ref, w_ref, wkr_ref, gq_ref, gkv_ref,
                   qn_ref, kvn_ref, kr_ref, wbf_ref):
    @pl.when(pl.program_id(0) == 0)
    def _():
        wbf_ref[...] = w_ref[...].astype(BF16)

    u = _modulate(x_ref[...], sc_ref[0], sh_ref[0])
    p = _dot_nt(u, wbf_ref[...])
    qn_ref[...] = _rms_norm(p[:, :Q_LORA], gq_ref[...]).astype(BF16)
    kvn_ref[...] = _rms_norm(p[:, Q_LORA:], gkv_ref[...]).astype(BF16)
    kr_ref[...] = _dot_nt(u, wkr_ref[...].astype(BF16))


def _latent(x2, scale, shift, w_t, w_kr_t, g_q, g_kv, bm=512):
    m, d = x2.shape
    per_seq = SEQ // bm
    n_lat = Q_LORA + KV_LORA
    mod_spec = pl.BlockSpec((1, 1, d), lambda i: (i // per_seq, 0, 0))
    return pl.pallas_call(
        _latent_kernel,
        grid=(m // bm,),
        in_specs=[pl.BlockSpec((bm, d), lambda i: (i, 0)),
                  mod_spec, mod_spec,
                  _resident((n_lat, d), lambda i: (0, 0)),
                  pl.BlockSpec((2 * LANES, d), lambda i: (0, 0)),
                  pl.BlockSpec((1, Q_LORA), lambda i: (0, 0)),
                  pl.BlockSpec((1, KV_LORA), lambda i: (0, 0))],
        out_specs=[pl.BlockSpec((bm, Q_LORA), lambda i: (i, 0)),
                   pl.BlockSpec((bm, KV_LORA), lambda i: (i, 0)),
                   pl.BlockSpec((bm, 2 * LANES), lambda i: (i, 0))],
        out_shape=[jax.ShapeDtypeStruct((m, Q_LORA), BF16),
                   jax.ShapeDtypeStruct((m, KV_LORA), BF16),
                   jax.ShapeDtypeStruct((m, 2 * LANES), F32)],
        scratch_shapes=[pltpu.VMEM((n_lat, d), BF16)],
        compiler_params=_params(1),
        name="latent_proj",
    )(x2, scale, shift, w_t, w_kr_t, g_q, g_kv)


def _qkv_up_kernel(qn_ref, kvn_ref, krr_ref, pos_ref, inv_ref, wq_ref, wk_ref, wvt_ref,
                   qnope_ref, qrope_ref, knope_ref, vt_ref, kr_ref):
    ang = pos_ref[...].astype(F32) * inv_ref[...]
    cos = jnp.cos(ang)
    sin = jnp.sin(ang)
    qn = qn_ref[...]
    kvn = kvn_ref[...]
    grp = UP_HEAD_GROUP * LANES
    lane = lax.broadcasted_iota(jnp.int32, cos.shape, 1)
    cs = jnp.where(lane < QK_ROPE, cos, sin) * Q_PRESCALE
    for g in range(N_HEADS // UP_HEAD_GROUP):
        lo, hi = g * grp, (g + 1) * grp
        qnope_ref[:, lo:hi] = (_dot(qn, wq_ref[:, lo:hi]) * Q_PRESCALE).astype(BF16)
        p = _dot(qn, wq_ref[:, HEAD_PAD + lo:HEAD_PAD + hi])
        for h in range(UP_HEAD_GROUP):
            t = p[:, h * LANES:(h + 1) * LANES] * cs
            qrope_ref[:, lo + h * LANES:lo + (h + 1) * LANES] = (
                t + pltpu.roll(t, QK_ROPE, 1)).astype(BF16)
        knope_ref[:, lo:hi] = _dot(kvn, wk_ref[:, lo:hi]).astype(BF16)
        vt_ref[lo:hi, :] = _dot_nt(wvt_ref[lo:hi, :], kvn).astype(BF16)
    krr = krr_ref[...]
    kr_ref[...] = (krr[:, :LANES] * cos + krr[:, LANES:] * sin).astype(BF16)


def _qkv_up(qn, kvn, kr_raw, pos_b, inv_row, w_q, w_k, w_vt, bm=512):
    m = qn.shape[0]
    row = lambda n: pl.BlockSpec((bm, n), lambda i: (i, 0))
    const = lambda a: pl.BlockSpec(a.shape, lambda i: (0, 0))
    return pl.pallas_call(
        _qkv_up_kernel,
        grid=(m // bm,),
        in_specs=[row(Q_LORA), row(KV_LORA), row(2 * LANES), row(LANES),
                  const(inv_row), const(w_q), const(w_k), const(w_vt)],
        out_specs=[row(HEAD_PAD), row(HEAD_PAD), row(HEAD_PAD),
                   pl.BlockSpec((HEAD_PAD, bm), lambda i: (0, i)), row(LANES)],
        out_shape=[jax.ShapeDtypeStruct((m, HEAD_PAD), BF16)] * 3
        + [jax.ShapeDtypeStruct((HEAD_PAD, m), BF16), jax.ShapeDtypeStruct((m, LANES), BF16)],
        compiler_params=_params(1),
        name="qkv_up",
    )(qn, kvn, kr_raw, pos_b, inv_row, w_q, w_k, w_vt)


def _conv_kernel(x_ref, sc_ref, sh_ref, wb_ref, wc_ref, wx_ref, wconv_ref, o_ref,
                 u_ref, halo_ref, *, tiles_per_seq):
    i = pl.program_id(0)
    j = pl.program_id(1)

    @pl.when(j == 0)
    def _():
        u_ref[...] = _modulate(x_ref[...], sc_ref[0], sh_ref[0])

    u = u_ref[...]
    cb = _dot_nt(u, wb_ref[...])
    z = _dot_nt(u, wc_ref[...]) * _dot_nt(u, wx_ref[...])
    bm = z.shape[0]

    @pl.when(i % tiles_per_seq == 0)
    def _():
        halo_ref[j] = jnp.zeros(halo_ref.shape[1:], F32)

    prev = halo_ref[j]
    rows = lax.broadcasted_iota(jnp.int32, z.shape, 0)
    z1 = jnp.where(rows == 0, prev[SUBLANES - 1:SUBLANES, :], pltpu.roll(z, 1, 0))
    z2 = jnp.where(rows == 0, prev[SUBLANES - 2:SUBLANES - 1, :],
                   jnp.where(rows == 1, prev[SUBLANES - 1:SUBLANES, :], pltpu.roll(z, 2, 0)))
    halo_ref[j] = z[bm - SUBLANES:, :]
    w = wconv_ref[...]
    o_ref[...] = (cb * (w[0:1, :] * z2 + w[1:2, :] * z1 + w[2:3, :] * z)).astype(BF16)


def _conv_branch(x2, scale, shift, w_rows, w_conv, bm=1024, tn=512):
    m, d = x2.shape
    n = w_conv.shape[1]
    per_seq = SEQ // bm
    mod_spec = pl.BlockSpec((1, 1, d), lambda i, j: (i // per_seq, 0, 0))
    w_spec = lambda piece: pl.BlockSpec((tn, d), lambda i, j: (piece * (n // tn) + j, 0))
    return pl.pallas_call(
        functools.partial(_conv_kernel, tiles_per_seq=per_seq),
        grid=(m // bm, n // tn),
        in_specs=[pl.BlockSpec((bm, d), lambda i, j: (i, 0)), mod_spec, mod_spec,
                  w_spec(0), w_spec(1), w_spec(2),
                  pl.BlockSpec((CONV_K, tn), lambda i, j: (0, j))],
        out_specs=pl.BlockSpec((bm, tn), lambda i, j: (i, j)),
        out_shape=jax.ShapeDtypeStruct((m, n), BF16),
        scratch_shapes=[pltpu.VMEM((bm, d), BF16),
                        pltpu.VMEM((n // tn, SUBLANES, tn), F32)],
        compiler_params=_params(2),
        name="conv_branch",
    )(x2, scale, shift, w_rows, w_rows, w_rows, w_conv)


def _gate_kernel(*refs, n_cast):
    (x_ref, sc_ref, sh_ref, wa_ref, wb_ref), cast_src, o_ref, cast_dst = _split_refs(refs, 5, n_cast)
    pieces = _cast_pieces(cast_src, cast_dst)
    u = _modulate(x_ref[...], sc_ref[0], sh_ref[0])
    n = wa_ref.shape[0]
    passes = [(g, c0) for g in range(2) for c0 in range(0, n, 2 * COL_CHUNK)]
    for k, (g, c0) in enumerate(passes):
        w_ref = (wa_ref, wb_ref)[g]
        o_ref[:, g * n + c0:g * n + c0 + 2 * COL_CHUNK] = jax.nn.sigmoid(
            _dot_nt(u, w_ref[c0:c0 + 2 * COL_CHUNK, :])).astype(BF16)
        _convert_pieces(pieces, k, len(passes))


def _gates(x2, scale, shift, w_rows, first_block, cast_jobs, bm=512):
    m, d = x2.shape
    per_seq = SEQ // bm
    mod_spec = pl.BlockSpec((1, 1, d), lambda i: (i // per_seq, 0, 0))
    specs = [_slab_specs(w, m // bm, lambda i: i, row0) for w, row0 in cast_jobs]
    outs = pl.pallas_call(
        functools.partial(_gate_kernel, n_cast=len(cast_jobs)),
        grid=(m // bm,),
        in_specs=[pl.BlockSpec((bm, d), lambda i: (i, 0)), mod_spec, mod_spec,
                  _resident((d, d), lambda i: (first_block, 0)),
                  _resident((d, d), lambda i: (first_block + 1, 0))] + [s[0] for s in specs],
        out_specs=[pl.BlockSpec((bm, 2 * d), lambda i: (i, 0))] + [s[1] for s in specs],
        out_shape=[jax.ShapeDtypeStruct((m, 2 * d), BF16)] + [s[2] for s in specs],
        compiler_params=_params(1),
        name="merge_gates",
    )(x2, scale, shift, w_rows, w_rows, *[w for w, _ in cast_jobs])
    return outs[0], outs[1:]


def _attn_kernel(*refs, n_cast):
    (qn_ref, qr_ref, kn_ref, kr_ref, vt_ref), cast_src, o_ref, cast_dst = _split_refs(refs, 5, n_cast)
    pieces = _cast_pieces(cast_src, cast_dst)
    kr = kr_ref[...]
    ks = [jnp.concatenate([kn_ref[:, a * LANES:(a + 1) * LANES], kr], axis=1)
          for a in range(ATTN_HEADS_PER_STEP)]
    key = lax.broadcasted_iota(jnp.int32, (ATTN_TQ, ATTN_TQ), 0)
    qry = lax.broadcasted_iota(jnp.int32, (ATTN_TQ, ATTN_TQ), 1)
    allowed = (key // CHUNK) <= (qry // CHUNK)
    n_blk = SEQ // ATTN_TQ

    def scores(task):
        a, i = task
        q0 = i * ATTN_TQ
        L = q0 + ATTN_TQ
        lanes = slice(a * LANES, (a + 1) * LANES)
        q = jnp.concatenate([qn_ref[q0:L, lanes], qr_ref[q0:L, lanes]], axis=1)
        return _dot_nt(ks[a][:L], q)

    tasks = [(a, i) for i in range(n_blk - 1, -1, -1) for a in range(ATTN_HEADS_PER_STEP)]
    pending = [scores(t) for t in tasks[:ATTN_LOOKAHEAD]]
    for pos, (a, i) in enumerate(tasks):
        q0 = i * ATTN_TQ
        L = q0 + ATTN_TQ
        s_t = pending.pop(0)
        if pos + ATTN_LOOKAHEAD < len(tasks):
            pending.append(scores(tasks[pos + ATTN_LOOKAHEAD]))
        s_diag = jnp.where(allowed, s_t[q0:, :], -1e30)
        s_t = s_diag if i == 0 else jnp.concatenate([s_t[:q0, :], s_diag], axis=0)
        e = jnp.exp2(s_t - jnp.max(s_t, axis=0, keepdims=True))
        denom = jnp.sum(e, axis=0, keepdims=True)
        o_t = _dot(vt_ref[a * V_HEAD:(a + 1) * V_HEAD, :L], e.astype(BF16))
        o_ref[q0:L, a * LANES:(a + 1) * LANES] = (o_t / denom).T.astype(BF16)
        _convert_pieces(pieces, pos, len(tasks))


def _attention(q_nope, q_rope, k_nope, k_rope, v_t, cast_jobs):
    m = q_nope.shape[0]
    hps = ATTN_HEADS_PER_STEP
    grid = (m // SEQ, N_HEADS // hps)
    head = pl.BlockSpec((SEQ, hps * LANES), lambda b, h: (b, h))
    specs = [_slab_specs(w, grid[0] * grid[1], lambda b, h: b * grid[1] + h, row0)
             for w, row0 in cast_jobs]
    outs = pl.pallas_call(
        functools.partial(_attn_kernel, n_cast=len(cast_jobs)),
        grid=grid,
        in_specs=[head, head, head, pl.BlockSpec((SEQ, LANES), lambda b, h: (b, 0)),
                  pl.BlockSpec((hps * V_HEAD, SEQ), lambda b, h: (h, b))] + [s[0] for s in specs],
        out_specs=[head] + [s[1] for s in specs],
        out_shape=[jax.ShapeDtypeStruct((m, HEAD_PAD), BF16)] + [s[2] for s in specs],
        compiler_params=_params(2),
        name="mla_attention",
    )(q_nope, q_rope, k_nope, k_rope, v_t, *[w for w, _ in cast_jobs])
    return outs[0], outs[1:]


def _merge_kernel(*refs, n_cast):
    (a_ref, c_ref, wa_ref, wb_ref, g_ref), cast_src, o_ref, cast_dst = _split_refs(refs, 5, n_cast)
    pieces = _cast_pieces(cast_src, cast_dst)
    a = a_ref[...]
    cm = c_ref[...]
    n = o_ref.shape[1]
    for k, c0 in enumerate(range(0, n, COL_CHUNK)):
        cols = slice(c0, c0 + COL_CHUNK)
        y_a = _dot(a, wa_ref[:, cols])
        y_b = _dot(cm, wb_ref[:, cols])
        o_ref[:, cols] = (g_ref[:, cols].astype(F32) * y_a
                          + g_ref[:, n + c0:n + c0 + COL_CHUNK].astype(F32) * y_b).astype(BF16)
        _convert_pieces(pieces, k, n // COL_CHUNK)


def _merge(attn, cm, w_oa, w_ob, gates, cast_jobs, bm=512):
    m, d = attn.shape
    n = w_oa.shape[1]
    lhs = pl.BlockSpec((bm, d), lambda i: (i, 0))
    specs = [_slab_specs(w, m // bm, lambda i: i, row0) for w, row0 in cast_jobs]
    outs = pl.pallas_call(
        functools.partial(_merge_kernel, n_cast=len(cast_jobs)),
        grid=(m // bm,),
        in_specs=[lhs, lhs, _resident((d, n), lambda i: (0, 0)), _resident((d, n), lambda i: (0, 0)),
                  pl.BlockSpec((bm, 2 * n), lambda i: (i, 0))] + [s[0] for s in specs],
        out_specs=[pl.BlockSpec((bm, n), lambda i: (i, 0))] + [s[1] for s in specs],
        out_shape=[jax.ShapeDtypeStruct((m, n), BF16)] + [s[2] for s in specs],
        compiler_params=_params(1),
        name="branch_merge",
    )(attn, cm, w_oa, w_ob, gates, *[w for w, _ in cast_jobs])
    return outs[0], outs[1:]


def _mix_out_kernel(mg_ref, w_ref, x_ref, g1_ref, lg_ref, lb_ref, o_ref):
    n_chunks = mg_ref.shape[0] // ROW_CHUNK
    mix = _dot(mg_ref[0:ROW_CHUNK, :], w_ref[...])
    for r in range(n_chunks):
        rows = slice(r * ROW_CHUNK, (r + 1) * ROW_CHUNK)
        cur = mix
        if r + 1 < n_chunks:
            mix = _dot(mg_ref[(r + 1) * ROW_CHUNK:(r + 2) * ROW_CHUNK, :], w_ref[...])
        y = DEEPNORM_ALPHA * x_ref[rows, :] + g1_ref[0] * cur
        o_ref[rows, :] = _layer_norm(y, lg_ref[...], lb_ref[...])


def _mix_out(merged, w_o, x2, gate1, ln_g, ln_b, bm=512):
    m, d = x2.shape
    per_seq = SEQ // bm
    row = pl.BlockSpec((bm, d), lambda i: (i, 0))
    vec = pl.BlockSpec((1, d), lambda i: (0, 0))
    return pl.pallas_call(
        _mix_out_kernel,
        grid=(m // bm,),
        in_specs=[row, pl.BlockSpec((d, d), lambda i: (0, 0)), row,
                  pl.BlockSpec((1, 1, d), lambda i: (i // per_seq, 0, 0)), vec, vec],
        out_specs=row,
        out_shape=jax.ShapeDtypeStruct((m, d), F32),
        compiler_params=_params(1),
        name="mix_out_ln",
    )(merged, w_o, x2, gate1, ln_g, ln_b)


def _ffn_kernel(x_ref, sc_ref, sh_ref, g2_ref, wg_ref, wu_ref, wo_ref, lg_ref, lb_ref, o_ref,
                u_ref):
    acc_ref = o_ref
    j = pl.program_id(1)

    @pl.when(j == 0)
    def _():
        u_ref[...] = _modulate(x_ref[...], sc_ref[0], sh_ref[0])
        acc_ref[...] = jnp.zeros_like(acc_ref)

    u = u_ref[...]
    tf = wg_ref.shape[1]
    half = tf // 2
    hs = []
    for c0 in (0, half):
        hg = _dot(u, wg_ref[:, c0:c0 + half])
        hs.append((hg * jax.nn.sigmoid(hg) * _dot(u, wu_ref[:, c0:c0 + half])).astype(BF16))
    for c0 in range(0, acc_ref.shape[1], COL_CHUNK):
        cols = slice(c0, c0 + COL_CHUNK)
        acc_ref[:, cols] += _dot(hs[0], wo_ref[:half, cols]) + _dot(hs[1], wo_ref[half:, cols])

    @pl.when(j == pl.num_programs(1) - 1)
    def _():
        y = DEEPNORM_ALPHA * x_ref[...] + g2_ref[0] * acc_ref[...]
        o_ref[...] = _layer_norm(y, lg_ref[...], lb_ref[...])


def _ffn(x1, scale, shift, gate, w_in, w_out, ln_g, ln_b, bm=1024, tf=512):
    m, d = x1.shape
    nf = D_FF // tf
    per_seq = SEQ // bm
    row = pl.BlockSpec((bm, d), lambda i, j: (i, 0))
    mod_spec = pl.BlockSpec((1, 1, d), lambda i, j: (i // per_seq, 0, 0))
    vec = pl.BlockSpec((1, d), lambda i, j: (0, 0))
    return pl.pallas_call(
        _ffn_kernel,
        grid=(m // bm, nf),
        in_specs=[row, mod_spec, mod_spec, mod_spec,
                  pl.BlockSpec((d, tf), lambda i, j: (0, j)),
                  pl.BlockSpec((d, tf), lambda i, j: (0, j + nf)),
                  pl.BlockSpec((tf, d), lambda i, j: (j, 0)), vec, vec],
        out_specs=row,
        out_shape=jax.ShapeDtypeStruct((m, d), F32),
        scratch_shapes=[pltpu.VMEM((bm, d), BF16)],
        compiler_params=_params(2, vmem=FFN_VMEM_LIMIT),
        name="swiglu_ffn_ln",
    )(x1, scale, shift, gate, w_in, w_in, w_out, ln_g, ln_b)


def _rot_last(w):
    return jnp.concatenate([-w[..., HALF:], w[..., :HALF]], axis=-1)


def kernel(x, c, positions, w_ada, b_ada, w_in, g_q_a, w_q_b, g_kv_a, w_kv_b, w_o_a,
           w_conv, w_o_b, w_o, ln1_g, ln1_b, w_ffn_in, w_ffn_out, ln2_g, ln2_b):
    bsz, seq, d = x.shape
    assert (seq, d) == (SEQ, D_MODEL) and w_ada.shape[0] == DEPTH
    m = bsz * seq
    x_cur = x.reshape(m, d)

    inv_freq = 1.0 / (ROPE_THETA ** (jnp.arange(0, QK_ROPE, 2, dtype=F32) / QK_ROPE))
    inv_row = jnp.tile(inv_freq, LANES // HALF).reshape(1, LANES)
    pos_b = jnp.broadcast_to(positions.reshape(m, 1), (m, LANES))
    c_pad = jnp.pad(c, ((0, SUBLANES - bsz), (0, 0)))

    for l in range(DEPTH):
        mod = _ada(c_pad, w_ada[l], b_ada[l].reshape(1, -1))[:bsz]
        shift1, scale1, gate1, shift2, scale2, gate2 = [
            t.reshape(bsz, 1, d) for t in jnp.split(mod, 6, axis=-1)]

        o_kr = Q_LORA + KV_LORA
        o_cb = o_kr + QK_ROPE
        w_tf = jnp.swapaxes(w_in[l], 0, 1)
        w_kr_rows = w_tf[o_kr:o_cb]
        w_rot_rows = jnp.concatenate([-w_kr_rows[HALF:], w_kr_rows[:HALF]], axis=0)
        lane_pad = jnp.zeros((LANES - QK_ROPE, d), F32)
        w_kr_t = jnp.concatenate([w_kr_rows, lane_pad, w_rot_rows, lane_pad], axis=0)

        wq = w_q_b[l].reshape(Q_LORA, N_HEADS, QK_NOPE + QK_ROPE)
        wq_rope = wq[:, :, QK_NOPE:]
        w_q = jnp.concatenate([
            wq[:, :, :QK_NOPE].reshape(Q_LORA, HEAD_PAD),
            jnp.concatenate([wq_rope, _rot_last(wq_rope)], axis=-1).reshape(Q_LORA, HEAD_PAD)],
            axis=1).astype(BF16)
        wkv = w_kv_b[l].reshape(KV_LORA, N_HEADS, QK_NOPE + V_HEAD)
        w_k = wkv[:, :, :QK_NOPE].reshape(KV_LORA, HEAD_PAD).astype(BF16)
        w_vt = jnp.swapaxes(wkv[:, :, QK_NOPE:].reshape(KV_LORA, HEAD_PAD), 0, 1).astype(BF16)

        qn, kvn, kr_raw = _latent(x_cur, scale1, shift1, w_tf, w_kr_t,
                                  g_q_a[l].reshape(1, -1), g_kv_a[l].reshape(1, -1))
        q_nope, q_rope, k_nope, v_t, k_rope = _qkv_up(qn, kvn, kr_raw, pos_b, inv_row,
                                                      w_q, w_k, w_vt)
        attn, (w_cg, w_oa, w_ob, w_om) = _attention(
            q_nope, q_rope, k_nope, k_rope, v_t,
            [(w_tf, o_cb), (w_o_a[l], 0), (w_o_b[l], 0), (w_o[l], 0)])
        cm = _conv_branch(x_cur, scale1, shift1, w_cg, w_conv[l])
        gates, (w_fi,) = _gates(x_cur, scale1, shift1, w_cg, 3, [(w_ffn_in[l], 0)])
        merged, (w_fo,) = _merge(attn, cm, w_oa, w_ob, gates, [(w_ffn_out[l], 0)])
        x_cur = _mix_out(merged, w_om, x_cur, gate1,
                         ln1_g[l].reshape(1, -1), ln1_b[l].reshape(1, -1))

        x_cur = _ffn(x_cur, scale2, shift2, gate2, w_fi, w_fo,
                     ln2_g[l].reshape(1, -1), ln2_b[l].reshape(1, -1))

    return x_cur.reshape(bsz, seq, d)
```

```python
import functools
import math

import jax
import jax.numpy as jnp
from jax import lax
from jax.experimental import pallas as pl
from jax.experimental.pallas import tpu as pltpu

F32 = jnp.float32
BF16 = jnp.bfloat16

D_MODEL = 2048
SEQ = 2048
CHUNK = 64
N_HEADS = 16
QK_NOPE = 128
QK_ROPE = 64
V_HEAD = 128
Q_LORA = 512
KV_LORA = 512
ROPE_THETA = 10000.0
ATTN_SCALE = (QK_NOPE + QK_ROPE) ** -0.5
CONV_K = 3
D_FF = 5632
DEPTH = 1
DEEPNORM_ALPHA = (2.0 * DEPTH) ** 0.25
LN_EPS = 1e-5
RMS_EPS = 1e-6

LANES = 128
SUBLANES = 8
HALF = QK_ROPE // 2
HEAD_PAD = N_HEADS * LANES
ATTN_TQ = 256
ATTN_LOOKAHEAD = 4
ATTN_HEADS_PER_STEP = 2
CAST_PIECE_ROWS = 32
UP_HEAD_GROUP = 4
ROW_CHUNK = 256
COL_CHUNK = 512
Q_PRESCALE = ATTN_SCALE * math.log2(math.e)

VMEM_LIMIT = 58 * 1024 * 1024
FFN_VMEM_LIMIT = 62 * 1024 * 1024


def _params(n_axes, vmem=VMEM_LIMIT):
    return pltpu.CompilerParams(
        dimension_semantics=("arbitrary",) * n_axes, vmem_limit_bytes=vmem)


def _resident(shape, index_map):
    return pl.BlockSpec(shape, index_map, pipeline_mode=pl.Buffered(1))


def _slab_specs(w, n_steps, step_of, row0=0):
    n_rows = w.shape[0] - row0
    cols = w.shape[1]
    bf16_rows = 2 * SUBLANES
    if n_rows % (n_steps * bf16_rows) == 0:
        rb = n_rows // n_steps
        in_spec = pl.BlockSpec(
            (pl.Element(rb), pl.Element(cols)),
            lambda *g: (pl.multiple_of(row0 + step_of(*g) * rb, SUBLANES), 0))
        out_spec = pl.BlockSpec((rb, cols), lambda *g: (step_of(*g), 0))
    else:
        assert row0 == 0 and n_rows % (n_steps // 2 * bf16_rows) == 0
        rb = n_rows // (n_steps // 2)
        in_spec = out_spec = pl.BlockSpec(
            (rb, cols // 2), lambda *g: (step_of(*g) // 2, step_of(*g) % 2))
    return in_spec, out_spec, jax.ShapeDtypeStruct((n_rows, cols), BF16)


def _split_refs(refs, n_in, n_cast):
    return refs[:n_in], refs[n_in:n_in + n_cast], refs[n_in + n_cast], refs[n_in + n_cast + 1:]


def _cast_pieces(cast_src, cast_dst):
    pieces = []
    for src, dst in zip(cast_src, cast_dst, strict=True):
        rows = src.shape[0]
        n_piece = next(n for n in range(max(1, rows // CAST_PIECE_ROWS), 0, -1)
                       if rows % n == 0 and (rows // n) % (2 * SUBLANES) == 0)
        pieces += [(src, dst, slice(p * rows // n_piece, (p + 1) * rows // n_piece))
                   for p in range(n_piece)]
    return pieces


def _convert_pieces(pieces, k, n):
    for src, dst, rows in pieces[k::n]:
        dst[rows, :] = src[rows, :].astype(BF16)


def _dot(a, b):
    return jnp.dot(a, b, preferred_element_type=F32)


def _dot_nt(a, b_t):
    return lax.dot_general(a, b_t, (((1,), (1,)), ((), ())), preferred_element_type=F32)


def _modulate(x, scale, shift):
    return (x * (1.0 + scale) + shift).astype(BF16)


def _layer_norm(y, g, b):
    mu = jnp.mean(y, axis=-1, keepdims=True)
    yc = y - mu
    var = jnp.mean(yc * yc, axis=-1, keepdims=True)
    return yc * lax.rsqrt(var + LN_EPS) * g + b


def _rms_norm(y, g):
    return y * lax.rsqrt(jnp.mean(y * y, axis=-1, keepdims=True) + RMS_EPS) * g


def _ada_kernel(c_ref, w_ref, b_ref, o_ref):
    c = c_ref[...]
    c_act = (c * jax.nn.sigmoid(c)).astype(BF16)
    o_ref[...] = _dot(c_act, w_ref[...].astype(BF16)) + b_ref[...]


def _ada(c_pad, w_ada, b_ada, tn=1024):
    rows, d = c_pad.shape
    n = w_ada.shape[1]
    return pl.pallas_call(
        _ada_kernel,
        grid=(n // tn,),
        in_specs=[pl.BlockSpec((rows, d), lambda j: (0, 0)),
                  pl.BlockSpec((d, tn), lambda j: (0, j)),
                  pl.BlockSpec((1, tn), lambda j: (0, j))],
        out_specs=pl.BlockSpec((rows, tn), lambda j: (0, j)),
        out_shape=jax.ShapeDtypeStruct((rows, n), F32),
        compiler_params=_params(1),
        name="ada_mod",
    )(c_pad, w_ada, b_ada)


def _latent_kernel(x_ref, sc_ref, sh_ref, w_ref, wkr_ref, gq_ref, gkv_ref,
                   qn_ref, kvn_ref, kr_ref, u_ref, wbf_ref):
    @pl.when(pl.program_id(0) == 0)
    def _():
        wbf_ref[...] = w_ref[...].astype(BF16)

    u = _modulate(x_ref[...], sc_ref[0], sh_ref[0])
    u_ref[...] = u
    p = _dot_nt(u, wbf_ref[...])
    qn_ref[...] = _rms_norm(p[:, :Q_LORA], gq_ref[...]).astype(BF16)
    kvn_ref[...] = _rms_norm(p[:, Q_LORA:], gkv_ref[...]).astype(BF16)
    kr_ref[...] = _dot_nt(u, wkr_ref[...].astype(BF16))


def _latent(x2, scale, shift, w_t, w_kr_t, g_q, g_kv, bm=1024):
    m, d = x2.shape
    per_seq = SEQ // bm
    n_lat = Q_LORA + KV_LORA
    mod_spec = pl.BlockSpec((1, 1, d), lambda i: (i // per_seq, 0, 0))
    return pl.pallas_call(
        _latent_kernel,
        grid=(m // bm,),
        in_specs=[pl.BlockSpec((bm, d), lambda i: (i, 0)),
                  mod_spec, mod_spec,
                  _resident((n_lat, d), lambda i: (0, 0)),
                  pl.BlockSpec((2 * LANES, d), lambda i: (0, 0)),
                  pl.BlockSpec((1, Q_LORA), lambda i: (0, 0)),
                  pl.BlockSpec((1, KV_LORA), lambda i: (0, 0))],
        out_specs=[pl.BlockSpec((bm, Q_LORA), lambda i: (i, 0)),
                   pl.BlockSpec((bm, KV_LORA), lambda i: (i, 0)),
                   pl.BlockSpec((bm, 2 * LANES), lambda i: (i, 0)),
                   pl.BlockSpec((bm, d), lambda i: (i, 0))],
        out_shape=[jax.ShapeDtypeStruct((m, Q_LORA), BF16),
                   jax.ShapeDtypeStruct((m, KV_LORA), BF16),
                   jax.ShapeDtypeStruct((m, 2 * LANES), F32),
                   jax.ShapeDtypeStruct((m, d), BF16)],
        scratch_shapes=[pltpu.VMEM((n_lat, d), BF16)],
        compiler_params=_params(1),
        name="latent_proj",
    )(x2, scale, shift, w_t, w_kr_t, g_q, g_kv)


def _qkv_up_kernel(qn_ref, kvn_ref, krr_ref, pos_ref, inv_ref, wq_ref, wk_ref, wvt_ref,
                   qnope_ref, qrope_ref, knope_ref, vt_ref, kr_ref):
    ang = pos_ref[...].astype(F32) * inv_ref[...]
    cos = jnp.cos(ang)
    sin = jnp.sin(ang)
    qn = qn_ref[...]
    kvn = kvn_ref[...]
    grp = UP_HEAD_GROUP * LANES
    lane = lax.broadcasted_iota(jnp.int32, cos.shape, 1)
    cs = jnp.where(lane < QK_ROPE, cos, sin) * Q_PRESCALE
    for g in range(N_HEADS // UP_HEAD_GROUP):
        lo, hi = g * grp, (g + 1) * grp
        qnope_ref[:, lo:hi] = (_dot(qn, wq_ref[:, lo:hi]) * Q_PRESCALE).astype(BF16)
        p = _dot(qn, wq_ref[:, HEAD_PAD + lo:HEAD_PAD + hi])
        for h in range(UP_HEAD_GROUP):
            t = p[:, h * LANES:(h + 1) * LANES] * cs
            qrope_ref[:, lo + h * LANES:lo + (h + 1) * LANES] = (
                t + pltpu.roll(t, QK_ROPE, 1)).astype(BF16)
        knope_ref[:, lo:hi] = _dot(kvn, wk_ref[:, lo:hi]).astype(BF16)
        vt_ref[lo:hi, :] = _dot_nt(wvt_ref[lo:hi, :], kvn).astype(BF16)
    krr = krr_ref[...]
    kr_ref[...] = (krr[:, :LANES] * cos + krr[:, LANES:] * sin).astype(BF16)


def _qkv_up(qn, kvn, kr_raw, pos_b, inv_row, w_q, w_k, w_vt, bm=512):
    m = qn.shape[0]
    row = lambda n: pl.BlockSpec((bm, n), lambda i: (i, 0))
    const = lambda a: pl.BlockSpec(a.shape, lambda i: (0, 0))
    return pl.pallas_call(
        _qkv_up_kernel,
        grid=(m // bm,),
        in_specs=[row(Q_LORA), row(KV_LORA), row(2 * LANES), row(LANES),
                  const(inv_row), const(w_q), const(w_k), const(w_vt)],
        out_specs=[row(HEAD_PAD), row(HEAD_PAD), row(HEAD_PAD),
                   pl.BlockSpec((HEAD_PAD, bm), lambda i: (0, i)), row(LANES)],
        out_shape=[jax.ShapeDtypeStruct((m, HEAD_PAD), BF16)] * 3
        + [jax.ShapeDtypeStruct((HEAD_PAD, m), BF16), jax.ShapeDtypeStruct((m, LANES), BF16)],
        compiler_params=_params(1),
        name="qkv_up",
    )(qn, kvn, kr_raw, pos_b, inv_row, w_q, w_k, w_vt)


def _conv_kernel(u_ref, wb_ref, wc_ref, wx_ref, wconv_ref, o_ref, halo_ref, *, tiles_per_seq):
    i = pl.program_id(0)
    j = pl.program_id(1)
    u = u_ref[...]
    bm, tn = o_ref.shape

    @pl.when(i % tiles_per_seq == 0)
    def _():
        halo_ref[j] = jnp.zeros(halo_ref.shape[1:], F32)

    rows = lax.broadcasted_iota(jnp.int32, (bm, COL_CHUNK), 0)
    for c0 in range(0, tn, COL_CHUNK):
        cols = slice(c0, c0 + COL_CHUNK)
        cb = _dot_nt(u, wb_ref[cols, :])
        z = _dot_nt(u, wc_ref[cols, :]) * _dot_nt(u, wx_ref[cols, :])
        prev = halo_ref[j, :, cols]
        z1 = jnp.where(rows == 0, prev[SUBLANES - 1:SUBLANES, :], pltpu.roll(z, 1, 0))
        z2 = jnp.where(rows == 0, prev[SUBLANES - 2:SUBLANES - 1, :],
                       jnp.where(rows == 1, prev[SUBLANES - 1:SUBLANES, :], pltpu.roll(z, 2, 0)))
        halo_ref[j, :, cols] = z[bm - SUBLANES:, :]
        w = wconv_ref[:, cols]
        o_ref[:, cols] = (cb * (w[0:1, :] * z2 + w[1:2, :] * z1 + w[2:3, :] * z)).astype(BF16)


def _conv_branch(u, w_rows, w_conv, bm=1024, tn=1024):
    m, d = u.shape
    n = w_conv.shape[1]
    per_seq = SEQ // bm
    w_spec = lambda piece: pl.BlockSpec((tn, d), lambda i, j: (piece * (n // tn) + j, 0))
    return pl.pallas_call(
        functools.partial(_conv_kernel, tiles_per_seq=per_seq),
        grid=(m // bm, n // tn),
        in_specs=[pl.BlockSpec((bm, d), lambda i, j: (i, 0)),
                  w_spec(0), w_spec(1), w_spec(2),
                  pl.BlockSpec((CONV_K, tn), lambda i, j: (0, j))],
        out_specs=pl.BlockSpec((bm, tn), lambda i, j: (i, j)),
        out_shape=jax.ShapeDtypeStruct((m, n), BF16),
        scratch_shapes=[pltpu.VMEM((n // tn, SUBLANES, tn), F32)],
        compiler_params=_params(2),
        name="conv_branch",
    )(u, w_rows, w_rows, w_rows, w_conv)


def _gate_kernel(*refs, n_cast):
    (u_ref, wa_ref, wb_ref), cast_src, o_ref, cast_dst = _split_refs(refs, 3, n_cast)
    pieces = _cast_pieces(cast_src, cast_dst)
    u = u_ref[...]
    n = wa_ref.shape[0]
    passes = [(g, c0) for g in range(2) for c0 in range(0, n, 2 * COL_CHUNK)]
    for k, (g, c0) in enumerate(passes):
        w_ref = (wa_ref, wb_ref)[g]
        o_ref[:, g * n + c0:g * n + c0 + 2 * COL_CHUNK] = jax.nn.sigmoid(
            _dot_nt(u, w_ref[c0:c0 + 2 * COL_CHUNK, :])).astype(BF16)
        _convert_pieces(pieces, k, len(passes))


def _gates(u, w_rows, first_block, cast_jobs, bm=512):
    m, d = u.shape
    specs = [_slab_specs(w, m // bm, lambda i: i, row0) for w, row0 in cast_jobs]
    outs = pl.pallas_call(
        functools.partial(_gate_kernel, n_cast=len(cast_jobs)),
        grid=(m // bm,),
        in_specs=[pl.BlockSpec((bm, d), lambda i: (i, 0)),
                  _resident((d, d), lambda i: (first_block, 0)),
                  _resident((d, d), lambda i: (first_block + 1, 0))] + [s[0] for s in specs],
        out_specs=[pl.BlockSpec((bm, 2 * d), lambda i: (i, 0))] + [s[1] for s in specs],
        out_shape=[jax.ShapeDtypeStruct((m, 2 * d), BF16)] + [s[2] for s in specs],
        compiler_params=_params(1),
        name="merge_gates",
    )(u, w_rows, w_rows, *[w for w, _ in cast_jobs])
    return outs[0], outs[1:]


def _attn_kernel(*refs, n_cast):
    (qn_ref, qr_ref, kn_ref, kr_ref, vt_ref), cast_src, o_ref, cast_dst = _split_refs(refs, 5, n_cast)
    pieces = _cast_pieces(cast_src, cast_dst)
    kr = kr_ref[...]
    ks = [jnp.concatenate([kn_ref[:, a * LANES:(a + 1) * LANES], kr], axis=1)
          for a in range(ATTN_HEADS_PER_STEP)]
    key = lax.broadcasted_iota(jnp.int32, (ATTN_TQ, ATTN_TQ), 0)
    qry = lax.broadcasted_iota(jnp.int32, (ATTN_TQ, ATTN_TQ), 1)
    allowed = (key // CHUNK) <= (qry // CHUNK)
    n_blk = SEQ // ATTN_TQ

    def scores(task):
        a, i = task
        q0 = i * ATTN_TQ
        L = q0 + ATTN_TQ
        lanes = slice(a * LANES, (a + 1) * LANES)
        q = jnp.concatenate([qn_ref[q0:L, lanes], qr_ref[q0:L, lanes]], axis=1)
        return _dot_nt(ks[a][:L], q)

    tasks = [(a, i) for i in range(n_blk - 1, -1, -1) for a in range(ATTN_HEADS_PER_STEP)]
    pending = [scores(t) for t in tasks[:ATTN_LOOKAHEAD]]
    for pos, (a, i) in enumerate(tasks):
        q0 = i * ATTN_TQ
        L = q0 + ATTN_TQ
        s_t = pending.pop(0)
        if pos + ATTN_LOOKAHEAD < len(tasks):
            pending.append(scores(tasks[pos + ATTN_LOOKAHEAD]))
        s_diag = jnp.where(allowed, s_t[q0:, :], -1e30)
        s_t = s_diag if i == 0 else jnp.concatenate([s_t[:q0, :], s_diag], axis=0)
        e = jnp.exp2(s_t - jnp.max(s_t, axis=0, keepdims=True))
        denom = jnp.sum(e, axis=0, keepdims=True)
        o_t = _dot(vt_ref[a * V_HEAD:(a + 1) * V_HEAD, :L], e.astype(BF16))
        o_ref[q0:L, a * LANES:(a + 1) * LANES] = (o_t / denom).T.astype(BF16)
        _convert_pieces(pieces, pos, len(tasks))


def _attention(q_nope, q_rope, k_nope, k_rope, v_t, cast_jobs):
    m = q_nope.shape[0]
    hps = ATTN_HEADS_PER_STEP
    grid = (m // SEQ, N_HEADS // hps)
    head = pl.BlockSpec((SEQ, hps * LANES), lambda b, h: (b, h))
    specs = [_slab_specs(w, grid[0] * grid[1], lambda b, h: b * grid[1] + h, row0)
             for w, row0 in cast_jobs]
    outs = pl.pallas_call(
        functools.partial(_attn_kernel, n_cast=len(cast_jobs)),
        grid=grid,
        in_specs=[head, head, head, pl.BlockSpec((SEQ, LANES), lambda b, h: (b, 0)),
                  pl.BlockSpec((hps * V_HEAD, SEQ), lambda b, h: (h, b))] + [s[0] for s in specs],
        out_specs=[head] + [s[1] for s in specs],
        out_shape=[jax.ShapeDtypeStruct((m, HEAD_PAD), BF16)] + [s[2] for s in specs],
        compiler_params=_params(2),
        name="mla_attention",
    )(q_nope, q_rope, k_nope, k_rope, v_t, *[w for w, _ in cast_jobs])
    return outs[0], outs[1:]


def _merge_kernel(*refs, n_cast):
    (a_ref, c_ref, wa_ref, wb_ref, g_ref), cast_src, o_ref, cast_dst = _split_refs(refs, 5, n_cast)
    pieces = _cast_pieces(cast_src, cast_dst)
    a = a_ref[...]
    cm = c_ref[...]
    n = o_ref.shape[1]
    for k, c0 in enumerate(range(0, n, COL_CHUNK)):
        cols = slice(c0, c0 + COL_CHUNK)
        y_a = _dot(a, wa_ref[:, cols])
        y_b = _dot(cm, wb_ref[:, cols])
        o_ref[:, cols] = (g_ref[:, cols].astype(F32) * y_a
                          + g_ref[:, n + c0:n + c0 + COL_CHUNK].astype(F32) * y_b).astype(BF16)
        _convert_pieces(pieces, k, n // COL_CHUNK)


def _merge(attn, cm, w_oa, w_ob, gates, cast_jobs, bm=512):
    m, d = attn.shape
    n = w_oa.shape[1]
    lhs = pl.BlockSpec((bm, d), lambda i: (i, 0))
    specs = [_slab_specs(w, m // bm, lambda i: i, row0) for w, row0 in cast_jobs]
    outs = pl.pallas_call(
        functools.partial(_merge_kernel, n_cast=len(cast_jobs)),
        grid=(m // bm,),
        in_specs=[lhs, lhs, _resident((d, n), lambda i: (0, 0)), _resident((d, n), lambda i: (0, 0)),
                  pl.BlockSpec((bm, 2 * n), lambda i: (i, 0))] + [s[0] for s in specs],
        out_specs=[pl.BlockSpec((bm, n), lambda i: (i, 0))] + [s[1] for s in specs],
        out_shape=[jax.ShapeDtypeStruct((m, n), BF16)] + [s[2] for s in specs],
        compiler_params=_params(1),
        name="branch_merge",
    )(attn, cm, w_oa, w_ob, gates, *[w for w, _ in cast_jobs])
    return outs[0], outs[1:]


def _mix_out_kernel(mg_ref, w_ref, x_ref, g1_ref, lg_ref, lb_ref, o_ref):
    n_chunks = mg_ref.shape[0] // ROW_CHUNK
    mix = _dot(mg_ref[0:ROW_CHUNK, :], w_ref[...])
    for r in range(n_chunks):
        rows = slice(r * ROW_CHUNK, (r + 1) * ROW_CHUNK)
        cur = mix
        if r + 1 < n_chunks:
            mix = _dot(mg_ref[(r + 1) * ROW_CHUNK:(r + 2) * ROW_CHUNK, :], w_ref[...])
        y = DEEPNORM_ALPHA * x_ref[rows, :] + g1_ref[0] * cur
        o_ref[rows, :] = _layer_norm(y, lg_ref[...], lb_ref[...])


def _mix_out(merged, w_o, x2, gate1, ln_g, ln_b, bm=512):
    m, d = x2.shape
    per_seq = SEQ // bm
    row = pl.BlockSpec((bm, d), lambda i: (i, 0))
    vec = pl.BlockSpec((1, d), lambda i: (0, 0))
    return pl.pallas_call(
        _mix_out_kernel,
        grid=(m // bm,),
        in_specs=[row, pl.BlockSpec((d, d), lambda i: (0, 0)), row,
                  pl.BlockSpec((1, 1, d), lambda i: (i // per_seq, 0, 0)), vec, vec],
        out_specs=row,
        out_shape=jax.ShapeDtypeStruct((m, d), F32),
        compiler_params=_params(1),
        name="mix_out_ln",
    )(merged, w_o, x2, gate1, ln_g, ln_b)


def _ffn_kernel(x_ref, sc_ref, sh_ref, g2_ref, wg_ref, wu_ref, wo_ref, lg_ref, lb_ref, o_ref,
                u_ref):
    acc_ref = o_ref
    j = pl.program_id(1)

    @pl.when(j == 0)
    def _():
        u_ref[...] = _modulate(x_ref[...], sc_ref[0], sh_ref[0])
        acc_ref[...] = jnp.zeros_like(acc_ref)

    u = u_ref[...]
    tf = wg_ref.shape[1]
    half = tf // 2
    hs = []
    for c0 in (0, half):
        hg = _dot(u, wg_ref[:, c0:c0 + half])
        hs.append((hg * jax.nn.sigmoid(hg) * _dot(u, wu_ref[:, c0:c0 + half])).astype(BF16))
    for c0 in range(0, acc_ref.shape[1], COL_CHUNK):
        cols = slice(c0, c0 + COL_CHUNK)
        acc_ref[:, cols] += _dot(hs[0], wo_ref[:half, cols]) + _dot(hs[1], wo_ref[half:, cols])

    @pl.when(j == pl.num_programs(1) - 1)
    def _():
        y = DEEPNORM_ALPHA * x_ref[...] + g2_ref[0] * acc_ref[...]
        o_ref[...] = _layer_norm(y, lg_ref[...], lb_ref[...])


def _ffn(x1, scale, shift, gate, w_in, w_out, ln_g, ln_b, bm=1024, tf=512):
    m, d = x1.shape
    nf = D_FF // tf
    per_seq = SEQ // bm
    row = pl.BlockSpec((bm, d), lambda i, j: (i, 0))
    mod_spec = pl.BlockSpec((1, 1, d), lambda i, j: (i // per_seq, 0, 0))
    vec = pl.BlockSpec((1, d), lambda i, j: (0, 0))
    return pl.pallas_call(
        _ffn_kernel,
        grid=(m // bm, nf),
        in_specs=[row, mod_spec, mod_spec, mod_spec,
                  pl.BlockSpec((d, tf), lambda i, j: (0, j)),
                  pl.BlockSpec((d, tf), lambda i, j: (0, j + nf)),
                  pl.BlockSpec((tf, d), lambda i, j: (j, 0)), vec, vec],
        out_specs=row,
        out_shape=jax.ShapeDtypeStruct((m, d), F32),
        scratch_shapes=[pltpu.VMEM((bm, d), BF16)],
        compiler_params=_params(2, vmem=FFN_VMEM_LIMIT),
        name="swiglu_ffn_ln",
    )(x1, scale, shift, gate, w_in, w_in, w_out, ln_g, ln_b)


def _rot_last(w):
    return jnp.concatenate([-w[..., HALF:], w[..., :HALF]], axis=-1)


def kernel(x, c, positions, w_ada, b_ada, w_in, g_q_a, w_q_b, g_kv_a, w_kv_b, w_o_a,
           w_conv, w_o_b, w_o, ln1_g, ln1_b, w_ffn_in, w_ffn_out, ln2_g, ln2_b):
    bsz, seq, d = x.shape
    assert (seq, d) == (SEQ, D_MODEL) and w_ada.shape[0] == DEPTH
    m = bsz * seq
    x_cur = x.reshape(m, d)

    inv_freq = 1.0 / (ROPE_THETA ** (jnp.arange(0, QK_ROPE, 2, dtype=F32) / QK_ROPE))
    inv_row = jnp.tile(inv_freq, LANES // HALF).reshape(1, LANES)
    pos_b = jnp.broadcast_to(positions.reshape(m, 1), (m, LANES))
    c_pad = jnp.pad(c, ((0, SUBLANES - bsz), (0, 0)))

    for l in range(DEPTH):
        mod = _ada(c_pad, w_ada[l], b_ada[l].reshape(1, -1))[:bsz]
        shift1, scale1, gate1, shift2, scale2, gate2 = [
            t.reshape(bsz, 1, d) for t in jnp.split(mod, 6, axis=-1)]

        o_kr = Q_LORA + KV_LORA
        o_cb = o_kr + QK_ROPE
        w_tf = jnp.swapaxes(w_in[l], 0, 1)
        w_kr_rows = w_tf[o_kr:o_cb]
        w_rot_rows = jnp.concatenate([-w_kr_rows[HALF:], w_kr_rows[:HALF]], axis=0)
        lane_pad = jnp.zeros((LANES - QK_ROPE, d), F32)
        w_kr_t = jnp.concatenate([w_kr_rows, lane_pad, w_rot_rows, lane_pad], axis=0)

        wq = w_q_b[l].reshape(Q_LORA, N_HEADS, QK_NOPE + QK_ROPE)
        wq_rope = wq[:, :, QK_NOPE:]
        w_q = jnp.concatenate([
            wq[:, :, :QK_NOPE].reshape(Q_LORA, HEAD_PAD),
            jnp.concatenate([wq_rope, _rot_last(wq_rope)], axis=-1).reshape(Q_LORA, HEAD_PAD)],
            axis=1).astype(BF16)
        wkv = w_kv_b[l].reshape(KV_LORA, N_HEADS, QK_NOPE + V_HEAD)
        w_k = wkv[:, :, :QK_NOPE].reshape(KV_LORA, HEAD_PAD).astype(BF16)
        w_vt = jnp.swapaxes(wkv[:, :, QK_NOPE:].reshape(KV_LORA, HEAD_PAD), 0, 1).astype(BF16)

        qn, kvn, kr_raw, u = _latent(x_cur, scale1, shift1, w_tf, w_kr_t,
                                     g_q_a[l].reshape(1, -1), g_kv_a[l].reshape(1, -1))
        q_nope, q_rope, k_nope, v_t, k_rope = _qkv_up(qn, kvn, kr_raw, pos_b, inv_row,
                                                      w_q, w_k, w_vt)
        attn, (w_cg, w_oa, w_ob, w_om) = _attention(
            q_nope, q_rope, k_nope, k_rope, v_t,
            [(w_tf, o_cb), (w_o_a[l], 0), (w_o_b[l], 0), (w_o[l], 0)])
        cm = _conv_branch(u, w_cg, w_conv[l])
        gates, (w_fi,) = _gates(u, w_cg, 3, [(w_ffn_in[l], 0)])
        merged, (w_fo,) = _merge(attn, cm, w_oa, w_ob, gates, [(w_ffn_out[l], 0)])
        x_cur = _mix_out(merged, w_om, x_cur, gate1,
                         ln1_g[l].reshape(1, -1), ln1_b[l].reshape(1, -1))

        x_cur = _ffn(x_cur, scale2, shift2, gate2, w_fi, w_fo,
                     ln2_g[l].reshape(1, -1), ln2_b[l].reshape(1, -1))

    return x_cur.reshape(bsz, seq, d)
```

```python
import functools
import math

import jax
import jax.numpy as jnp
from jax import lax
from jax.experimental import pallas as pl
from jax.experimental.pallas import tpu as pltpu

F32 = jnp.float32
BF16 = jnp.bfloat16

D_MODEL = 2048
SEQ = 2048
CHUNK = 64
N_HEADS = 16
QK_NOPE = 128
QK_ROPE = 64
V_HEAD = 128
Q_LORA = 512
KV_LORA = 512
ROPE_THETA = 10000.0
ATTN_SCALE = (QK_NOPE + QK_ROPE) ** -0.5
CONV_K = 3
D_FF = 5632
DEPTH = 1
DEEPNORM_ALPHA = (2.0 * DEPTH) ** 0.25
LN_EPS = 1e-5
RMS_EPS = 1e-6

LANES = 128
SUBLANES = 8
HALF = QK_ROPE // 2
HEAD_PAD = N_HEADS * LANES
ATTN_TQ = 256
ATTN_LOOKAHEAD = 4
ATTN_HEADS_PER_STEP = 2
CAST_PIECE_ROWS = 32
UP_HEAD_GROUP = 4
ROW_CHUNK = 256
COL_CHUNK = 512
MXU_WIDTH = 256
Q_PRESCALE = ATTN_SCALE * math.log2(math.e)

VMEM_LIMIT = 58 * 1024 * 1024
FFN_VMEM_LIMIT = 62 * 1024 * 1024


def _params(n_axes, vmem=VMEM_LIMIT):
    return pltpu.CompilerParams(
        dimension_semantics=("arbitrary",) * n_axes, vmem_limit_bytes=vmem)


def _resident(shape, index_map):
    return pl.BlockSpec(shape, index_map, pipeline_mode=pl.Buffered(1))


def _slab_specs(w, n_steps, step_of, row0=0):
    n_rows = w.shape[0] - row0
    cols = w.shape[1]
    bf16_rows = 2 * SUBLANES
    if n_rows % (n_steps * bf16_rows) == 0:
        rb = n_rows // n_steps
        in_spec = pl.BlockSpec(
            (pl.Element(rb), pl.Element(cols)),
            lambda *g: (pl.multiple_of(row0 + step_of(*g) * rb, SUBLANES), 0))
        out_spec = pl.BlockSpec((rb, cols), lambda *g: (step_of(*g), 0))
    else:
        assert row0 == 0 and n_rows % (n_steps // 2 * bf16_rows) == 0
        rb = n_rows // (n_steps // 2)
        in_spec = out_spec = pl.BlockSpec(
            (rb, cols // 2), lambda *g: (step_of(*g) // 2, step_of(*g) % 2))
    return in_spec, out_spec, jax.ShapeDtypeStruct((n_rows, cols), BF16)


def _split_refs(refs, n_in, n_cast):
    return refs[:n_in], refs[n_in:n_in + n_cast], refs[n_in + n_cast], refs[n_in + n_cast + 1:]


def _cast_pieces(cast_src, cast_dst):
    pieces = []
    for src, dst in zip(cast_src, cast_dst, strict=True):
        rows = src.shape[0]
        n_piece = next(n for n in range(max(1, rows // CAST_PIECE_ROWS), 0, -1)
                       if rows % n == 0 and (rows // n) % (2 * SUBLANES) == 0)
        pieces += [(src, dst, slice(p * rows // n_piece, (p + 1) * rows // n_piece))
                   for p in range(n_piece)]
    return pieces


def _convert_pieces(pieces, k, n):
    for src, dst, rows in pieces[k::n]:
        dst[rows, :] = src[rows, :].astype(BF16)


def _col_chunks(n, width):
    chunks = [(c0, width) for c0 in range(0, n - width, width)]
    c0, size = n - width, width
    while size > MXU_WIDTH:
        size //= 2
        chunks.append((c0, size))
        c0 += size
    chunks.append((c0, size))
    return chunks


def _dot(a, b):
    return jnp.dot(a, b, preferred_element_type=F32)


def _dot_nt(a, b_t):
    return lax.dot_general(a, b_t, (((1,), (1,)), ((), ())), preferred_element_type=F32)


def _modulate(x, scale, shift):
    return (x * (1.0 + scale) + shift).astype(BF16)


def _layer_norm(y, g, b):
    mu = jnp.mean(y, axis=-1, keepdims=True)
    yc = y - mu
    var = jnp.mean(yc * yc, axis=-1, keepdims=True)
    return yc * lax.rsqrt(var + LN_EPS) * g + b


def _rms_norm(y, g):
    return y * lax.rsqrt(jnp.mean(y * y, axis=-1, keepdims=True) + RMS_EPS) * g


def _ada_kernel(c_ref, w_ref, b_ref, o_ref):
    c = c_ref[...]
    c_act = (c * jax.nn.sigmoid(c)).astype(BF16)
    o_ref[...] = _dot(c_act, w_ref[...].astype(BF16)) + b_ref[...]


def _ada(c_pad, w_ada, b_ada, tn=1024):
    rows, d = c_pad.shape
    n = w_ada.shape[1]
    return pl.pallas_call(
        _ada_kernel,
        grid=(n // tn,),
        in_specs=[pl.BlockSpec((rows, d), lambda j: (0, 0)),
                  pl.BlockSpec((d, tn), lambda j: (0, j)),
                  pl.BlockSpec((1, tn), lambda j: (0, j))],
        out_specs=pl.BlockSpec((rows, tn), lambda j: (0, j)),
        out_shape=jax.ShapeDtypeStruct((rows, n), F32),
        compiler_params=_params(1),
        name="ada_mod",
    )(c_pad, w_ada, b_ada)


def _latent_kernel(x_ref, sc_ref, sh_ref, w_ref, wkr_ref, gq_ref, gkv_ref,
                   qn_ref, kvn_ref, kr_ref, u_ref, wbf_ref):
    @pl.when(pl.program_id(0) == 0)
    def _():
        wbf_ref[...] = w_ref[...].astype(BF16)

    u = _modulate(x_ref[...], sc_ref[0], sh_ref[0])
    u_ref[...] = u
    p = _dot_nt(u, wbf_ref[...])
    qn_ref[...] = _rms_norm(p[:, :Q_LORA], gq_ref[...]).astype(BF16)
    kvn_ref[...] = _rms_norm(p[:, Q_LORA:], gkv_ref[...]).astype(BF16)
    kr_ref[...] = _dot_nt(u, wkr_ref[...].astype(BF16))


def _latent(x2, scale, shift, w_t, w_kr_t, g_q, g_kv, bm=1024):
    m, d = x2.shape
    per_seq = SEQ // bm
    n_lat = Q_LORA + KV_LORA
    mod_spec = pl.BlockSpec((1, 1, d), lambda i: (i // per_seq, 0, 0))
    return pl.pallas_call(
        _latent_kernel,
        grid=(m // bm,),
        in_specs=[pl.BlockSpec((bm, d), lambda i: (i, 0)),
                  mod_spec, mod_spec,
                  _resident((n_lat, d), lambda i: (0, 0)),
                  pl.BlockSpec((2 * LANES, d), lambda i: (0, 0)),
                  pl.BlockSpec((1, Q_LORA), lambda i: (0, 0)),
                  pl.BlockSpec((1, KV_LORA), lambda i: (0, 0))],
        out_specs=[pl.BlockSpec((bm, Q_LORA), lambda i: (i, 0)),
                   pl.BlockSpec((bm, KV_LORA), lambda i: (i, 0)),
                   pl.BlockSpec((bm, 2 * LANES), lambda i: (i, 0)),
                   pl.BlockSpec((bm, d), lambda i: (i, 0))],
        out_shape=[jax.ShapeDtypeStruct((m, Q_LORA), BF16),
                   jax.ShapeDtypeStruct((m, KV_LORA), BF16),
                   jax.ShapeDtypeStruct((m, 2 * LANES), F32),
                   jax.ShapeDtypeStruct((m, d), BF16)],
        scratch_shapes=[pltpu.VMEM((n_lat, d), BF16)],
        compiler_params=_params(1),
        name="latent_proj",
    )(x2, scale, shift, w_t, w_kr_t, g_q, g_kv)


def _qkv_up_kernel(qn_ref, kvn_ref, krr_ref, pos_ref, inv_ref, wq_ref, wk_ref, wvt_ref,
                   qnope_ref, qrope_ref, knope_ref, vt_ref, kr_ref):
    ang = pos_ref[...].astype(F32) * inv_ref[...]
    cos = jnp.cos(ang)
    sin = jnp.sin(ang)
    qn = qn_ref[...]
    kvn = kvn_ref[...]
    grp = UP_HEAD_GROUP * LANES
    lane = lax.broadcasted_iota(jnp.int32, cos.shape, 1)
    cs = jnp.where(lane < QK_ROPE, cos, sin) * Q_PRESCALE
    for g in range(N_HEADS // UP_HEAD_GROUP):
        lo, hi = g * grp, (g + 1) * grp
        qnope_ref[:, lo:hi] = (_dot(qn, wq_ref[:, lo:hi]) * Q_PRESCALE).astype(BF16)
        p = _dot(qn, wq_ref[:, HEAD_PAD + lo:HEAD_PAD + hi])
        for h in range(UP_HEAD_GROUP):
            t = p[:, h * LANES:(h + 1) * LANES] * cs
            qrope_ref[:, lo + h * LANES:lo + (h + 1) * LANES] = (
                t + pltpu.roll(t, QK_ROPE, 1)).astype(BF16)
        knope_ref[:, lo:hi] = _dot(kvn, wk_ref[:, lo:hi]).astype(BF16)
        vt_ref[lo:hi, :] = _dot_nt(wvt_ref[lo:hi, :], kvn).astype(BF16)
    krr = krr_ref[...]
    kr_ref[...] = (krr[:, :LANES] * cos + krr[:, LANES:] * sin).astype(BF16)


def _qkv_up(qn, kvn, kr_raw, pos_b, inv_row, w_q, w_k, w_vt, bm=512):
    m = qn.shape[0]
    row = lambda n: pl.BlockSpec((bm, n), lambda i: (i, 0))
    const = lambda a: pl.BlockSpec(a.shape, lambda i: (0, 0))
    return pl.pallas_call(
        _qkv_up_kernel,
        grid=(m // bm,),
        in_specs=[row(Q_LORA), row(KV_LORA), row(2 * LANES), row(LANES),
                  const(inv_row), const(w_q), const(w_k), const(w_vt)],
        out_specs=[row(HEAD_PAD), row(HEAD_PAD), row(HEAD_PAD),
                   pl.BlockSpec((HEAD_PAD, bm), lambda i: (0, i)), row(LANES)],
        out_shape=[jax.ShapeDtypeStruct((m, HEAD_PAD), BF16)] * 3
        + [jax.ShapeDtypeStruct((HEAD_PAD, m), BF16), jax.ShapeDtypeStruct((m, LANES), BF16)],
        compiler_params=_params(1),
        name="qkv_up",
    )(qn, kvn, kr_raw, pos_b, inv_row, w_q, w_k, w_vt)


def _conv_kernel(u_ref, wb_ref, wc_ref, wx_ref, wconv_ref, o_ref, halo_ref, *, tiles_per_seq):
    i = pl.program_id(0)
    j = pl.program_id(1)
    u = u_ref[...]
    bm, tn = o_ref.shape

    @pl.when(i % tiles_per_seq == 0)
    def _():
        halo_ref[j] = jnp.zeros(halo_ref.shape[1:], F32)

    for c0, width in _col_chunks(tn, COL_CHUNK):
        cols = slice(c0, c0 + width)
        rows = lax.broadcasted_iota(jnp.int32, (bm, width), 0)
        cb = _dot_nt(u, wb_ref[cols, :])
        z = _dot_nt(u, wc_ref[cols, :]) * _dot_nt(u, wx_ref[cols, :])
        prev = halo_ref[j, :, cols]
        z1 = jnp.where(rows == 0, prev[SUBLANES - 1:SUBLANES, :], pltpu.roll(z, 1, 0))
        z2 = jnp.where(rows == 0, prev[SUBLANES - 2:SUBLANES - 1, :],
                       jnp.where(rows == 1, prev[SUBLANES - 1:SUBLANES, :], pltpu.roll(z, 2, 0)))
        halo_ref[j, :, cols] = z[bm - SUBLANES:, :]
        w = wconv_ref[:, cols]
        o_ref[:, cols] = (cb * (w[0:1, :] * z2 + w[1:2, :] * z1 + w[2:3, :] * z)).astype(BF16)


def _conv_branch(u, w_rows, w_conv, bm=1024, tn=1024):
    m, d = u.shape
    n = w_conv.shape[1]
    per_seq = SEQ // bm
    w_spec = lambda piece: pl.BlockSpec((tn, d), lambda i, j: (piece * (n // tn) + j, 0))
    return pl.pallas_call(
        functools.partial(_conv_kernel, tiles_per_seq=per_seq),
        grid=(m // bm, n // tn),
        in_specs=[pl.BlockSpec((bm, d), lambda i, j: (i, 0)),
                  w_spec(0), w_spec(1), w_spec(2),
                  pl.BlockSpec((CONV_K, tn), lambda i, j: (0, j))],
        out_specs=pl.BlockSpec((bm, tn), lambda i, j: (i, j)),
        out_shape=jax.ShapeDtypeStruct((m, n), BF16),
        scratch_shapes=[pltpu.VMEM((n // tn, SUBLANES, tn), F32)],
        compiler_params=_params(2),
        name="conv_branch",
    )(u, w_rows, w_rows, w_rows, w_conv)


def _gate_kernel(*refs, n_cast):
    (u_ref, wa_ref, wb_ref), cast_src, o_ref, cast_dst = _split_refs(refs, 3, n_cast)
    pieces = _cast_pieces(cast_src, cast_dst)
    u = u_ref[...]
    n = wa_ref.shape[0]
    passes = ([(wa_ref, 0, c0, 2 * COL_CHUNK) for c0 in range(0, n, 2 * COL_CHUNK)]
              + [(wb_ref, n, c0, width) for c0, width in _col_chunks(n, 2 * COL_CHUNK)])
    for k, (w_ref, out0, c0, width) in enumerate(passes):
        o_ref[:, out0 + c0:out0 + c0 + width] = jax.nn.sigmoid(
            _dot_nt(u, w_ref[c0:c0 + width, :])).astype(BF16)
        _convert_pieces(pieces, k, len(passes))


def _gates(u, w_rows, first_block, cast_jobs, bm=512):
    m, d = u.shape
    specs = [_slab_specs(w, m // bm, lambda i: i, row0) for w, row0 in cast_jobs]
    outs = pl.pallas_call(
        functools.partial(_gate_kernel, n_cast=len(cast_jobs)),
        grid=(m // bm,),
        in_specs=[pl.BlockSpec((bm, d), lambda i: (i, 0)),
                  _resident((d, d), lambda i: (first_block, 0)),
                  _resident((d, d), lambda i: (first_block + 1, 0))] + [s[0] for s in specs],
        out_specs=[pl.BlockSpec((bm, 2 * d), lambda i: (i, 0))] + [s[1] for s in specs],
        out_shape=[jax.ShapeDtypeStruct((m, 2 * d), BF16)] + [s[2] for s in specs],
        compiler_params=_params(1),
        name="merge_gates",
    )(u, w_rows, w_rows, *[w for w, _ in cast_jobs])
    return outs[0], outs[1:]


def _attn_kernel(*refs, n_cast):
    (qn_ref, qr_ref, kn_ref, kr_ref, vt_ref), cast_src, o_ref, cast_dst = _split_refs(refs, 5, n_cast)
    pieces = _cast_pieces(cast_src, cast_dst)
    kr = kr_ref[...]
    ks = [jnp.concatenate([kn_ref[:, a * LANES:(a + 1) * LANES], kr], axis=1)
          for a in range(ATTN_HEADS_PER_STEP)]
    key = lax.broadcasted_iota(jnp.int32, (ATTN_TQ, ATTN_TQ), 0)
    qry = lax.broadcasted_iota(jnp.int32, (ATTN_TQ, ATTN_TQ), 1)
    allowed = (key // CHUNK) <= (qry // CHUNK)
    n_blk = SEQ // ATTN_TQ

    def scores(task):
        a, i = task
        q0 = i * ATTN_TQ
        L = q0 + ATTN_TQ
        lanes = slice(a * LANES, (a + 1) * LANES)
        q = jnp.concatenate([qn_ref[q0:L, lanes], qr_ref[q0:L, lanes]], axis=1)
        return _dot_nt(ks[a][:L], q)

    tasks = [(a, i) for i in range(n_blk - 1, -1, -1) for a in range(ATTN_HEADS_PER_STEP)]
    pending = [scores(t) for t in tasks[:ATTN_LOOKAHEAD]]
    for pos, (a, i) in enumerate(tasks):
        q0 = i * ATTN_TQ
        L = q0 + ATTN_TQ
        s_t = pending.pop(0)
        if pos + ATTN_LOOKAHEAD < len(tasks):
            pending.append(scores(tasks[pos + ATTN_LOOKAHEAD]))
        heads = slice(a * V_HEAD, (a + 1) * V_HEAD)
        s_diag = jnp.where(allowed, s_t[q0:, :], -1e30)
        m_col = jnp.max(s_diag, axis=0, keepdims=True)
        if i > 0:
            s_full = s_t[:q0, :]
            m_col = jnp.maximum(m_col, jnp.max(s_full, axis=0, keepdims=True))
        e_diag = jnp.exp2(s_diag - m_col)
        denom = jnp.sum(e_diag, axis=0, keepdims=True)
        o_t = _dot(vt_ref[heads, q0:L], e_diag.astype(BF16))
        if i > 0:
            e_full = jnp.exp2(s_full - m_col)
            denom = denom + jnp.sum(e_full, axis=0, keepdims=True)
            o_t = o_t + _dot(vt_ref[heads, :q0], e_full.astype(BF16))
        o_ref[q0:L, a * LANES:(a + 1) * LANES] = (o_t / denom).T.astype(BF16)
        _convert_pieces(pieces, pos, len(tasks))


def _attention(q_nope, q_rope, k_nope, k_rope, v_t, cast_jobs):
    m = q_nope.shape[0]
    hps = ATTN_HEADS_PER_STEP
    grid = (m // SEQ, N_HEADS // hps)
    head = pl.BlockSpec((SEQ, hps * LANES), lambda b, h: (b, h))
    specs = [_slab_specs(w, grid[0] * grid[1], lambda b, h: b * grid[1] + h, row0)
             for w, row0 in cast_jobs]
    outs = pl.pallas_call(
        functools.partial(_attn_kernel, n_cast=len(cast_jobs)),
        grid=grid,
        in_specs=[head, head, head, pl.BlockSpec((SEQ, LANES), lambda b, h: (b, 0)),
                  pl.BlockSpec((hps * V_HEAD, SEQ), lambda b, h: (h, b))] + [s[0] for s in specs],
        out_specs=[head] + [s[1] for s in specs],
        out_shape=[jax.ShapeDtypeStruct((m, HEAD_PAD), BF16)] + [s[2] for s in specs],
        compiler_params=_params(2),
        name="mla_attention",
    )(q_nope, q_rope, k_nope, k_rope, v_t, *[w for w, _ in cast_jobs])
    return outs[0], outs[1:]


def _merge_kernel(*refs, n_cast):
    (a_ref, c_ref, wa_ref, wb_ref, g_ref), cast_src, o_ref, cast_dst = _split_refs(refs, 5, n_cast)
    pieces = _cast_pieces(cast_src, cast_dst)
    a = a_ref[...]
    cm = c_ref[...]
    n = o_ref.shape[1]
    chunks = _col_chunks(n, COL_CHUNK)
    for k, (c0, width) in enumerate(chunks):
        cols = slice(c0, c0 + width)
        y_a = _dot(a, wa_ref[:, cols])
        y_b = _dot(cm, wb_ref[:, cols])
        o_ref[:, cols] = (g_ref[:, cols].astype(F32) * y_a
                          + g_ref[:, n + c0:n + c0 + width].astype(F32) * y_b).astype(BF16)
        _convert_pieces(pieces, k, len(chunks))


def _merge(attn, cm, w_oa, w_ob, gates, cast_jobs, bm=512):
    m, d = attn.shape
    n = w_oa.shape[1]
    lhs = pl.BlockSpec((bm, d), lambda i: (i, 0))
    specs = [_slab_specs(w, m // bm, lambda i: i, row0) for w, row0 in cast_jobs]
    outs = pl.pallas_call(
        functools.partial(_merge_kernel, n_cast=len(cast_jobs)),
        grid=(m // bm,),
        in_specs=[lhs, lhs, _resident((d, n), lambda i: (0, 0)), _resident((d, n), lambda i: (0, 0)),
                  pl.BlockSpec((bm, 2 * n), lambda i: (i, 0))] + [s[0] for s in specs],
        out_specs=[pl.BlockSpec((bm, n), lambda i: (i, 0))] + [s[1] for s in specs],
        out_shape=[jax.ShapeDtypeStruct((m, n), BF16)] + [s[2] for s in specs],
        compiler_params=_params(1),
        name="branch_merge",
    )(attn, cm, w_oa, w_ob, gates, *[w for w, _ in cast_jobs])
    return outs[0], outs[1:]


def _mix_out_kernel(mg_ref, w_ref, x_ref, g1_ref, lg_ref, lb_ref, o_ref):
    n_chunks = mg_ref.shape[0] // ROW_CHUNK
    mix = _dot(mg_ref[0:ROW_CHUNK, :], w_ref[...])
    for r in range(n_chunks):
        rows = slice(r * ROW_CHUNK, (r + 1) * ROW_CHUNK)
        cur = mix
        if r + 1 < n_chunks:
            mix = _dot(mg_ref[(r + 1) * ROW_CHUNK:(r + 2) * ROW_CHUNK, :], w_ref[...])
        y = DEEPNORM_ALPHA * x_ref[rows, :] + g1_ref[0] * cur
        o_ref[rows, :] = _layer_norm(y, lg_ref[...], lb_ref[...])


def _mix_out(merged, w_o, x2, gate1, ln_g, ln_b, bm=512):
    m, d = x2.shape
    per_seq = SEQ // bm
    row = pl.BlockSpec((bm, d), lambda i: (i, 0))
    vec = pl.BlockSpec((1, d), lambda i: (0, 0))
    return pl.pallas_call(
        _mix_out_kernel,
        grid=(m // bm,),
        in_specs=[row, _resident((d, d), lambda i: (0, 0)), row,
                  pl.BlockSpec((1, 1, d), lambda i: (i // per_seq, 0, 0)), vec, vec],
        out_specs=row,
        out_shape=jax.ShapeDtypeStruct((m, d), F32),
        compiler_params=_params(1),
        name="mix_out_ln",
    )(merged, w_o, x2, gate1, ln_g, ln_b)


def _ffn_kernel(x_ref, sc_ref, sh_ref, g2_ref, wg_ref, wu_ref, wo_ref, lg_ref, lb_ref, o_ref,
                u_ref):
    acc_ref = o_ref
    j = pl.program_id(1)

    @pl.when(j == 0)
    def _():
        u_ref[...] = _modulate(x_ref[...], sc_ref[0], sh_ref[0])
        acc_ref[...] = jnp.zeros_like(acc_ref)

    u = u_ref[...]
    tf = wg_ref.shape[1]
    half = tf // 2
    hs = []
    for c0 in (0, half):
        hg = _dot(u, wg_ref[:, c0:c0 + half])
        hs.append((hg * jax.nn.sigmoid(hg) * _dot(u, wu_ref[:, c0:c0 + half])).astype(BF16))
    for c0 in range(0, acc_ref.shape[1], COL_CHUNK):
        cols = slice(c0, c0 + COL_CHUNK)
        acc_ref[:, cols] += _dot(hs[0], wo_ref[:half, cols]) + _dot(hs[1], wo_ref[half:, cols])

    @pl.when(j == pl.num_programs(1) - 1)
    def _():
        y = DEEPNORM_ALPHA * x_ref[...] + g2_ref[0] * acc_ref[...]
        o_ref[...] = _layer_norm(y, lg_ref[...], lb_ref[...])


def _ffn(x1, scale, shift, gate, w_in, w_out, ln_g, ln_b, bm=1024, tf=512):
    m, d = x1.shape
    nf = D_FF // tf
    per_seq = SEQ // bm
    row = pl.BlockSpec((bm, d), lambda i, j: (i, 0))
    mod_spec = pl.BlockSpec((1, 1, d), lambda i, j: (i // per_seq, 0, 0))
    vec = pl.BlockSpec((1, d), lambda i, j: (0, 0))
    return pl.pallas_call(
        _ffn_kernel,
        grid=(m // bm, nf),
        in_specs=[row, mod_spec, mod_spec, mod_spec,
                  pl.BlockSpec((d, tf), lambda i, j: (0, j)),
                  pl.BlockSpec((d, tf), lambda i, j: (0, j + nf)),
                  pl.BlockSpec((tf, d), lambda i, j: (j, 0)), vec, vec],
        out_specs=row,
        out_shape=jax.ShapeDtypeStruct((m, d), F32),
        scratch_shapes=[pltpu.VMEM((bm, d), BF16)],
        compiler_params=_params(2, vmem=FFN_VMEM_LIMIT),
        name="swiglu_ffn_ln",
    )(x1, scale, shift, gate, w_in, w_in, w_out, ln_g, ln_b)


def _rot_last(w):
    return jnp.concatenate([-w[..., HALF:], w[..., :HALF]], axis=-1)


def kernel(x, c, positions, w_ada, b_ada, w_in, g_q_a, w_q_b, g_kv_a, w_kv_b, w_o_a,
           w_conv, w_o_b, w_o, ln1_g, ln1_b, w_ffn_in, w_ffn_out, ln2_g, ln2_b):
    bsz, seq, d = x.shape
    assert (seq, d) == (SEQ, D_MODEL) and w_ada.shape[0] == DEPTH
    m = bsz * seq
    x_cur = x.reshape(m, d)

    inv_freq = 1.0 / (ROPE_THETA ** (jnp.arange(0, QK_ROPE, 2, dtype=F32) / QK_ROPE))
    inv_row = jnp.tile(inv_freq, LANES // HALF).reshape(1, LANES)
    pos_b = jnp.broadcast_to(positions.reshape(m, 1), (m, LANES))
    c_pad = jnp.pad(c, ((0, SUBLANES - bsz), (0, 0)))

    for l in range(DEPTH):
        mod = _ada(c_pad, w_ada[l], b_ada[l].reshape(1, -1))[:bsz]
        shift1, scale1, gate1, shift2, scale2, gate2 = [
            t.reshape(bsz, 1, d) for t in jnp.split(mod, 6, axis=-1)]

        o_kr = Q_LORA + KV_LORA
        o_cb = o_kr + QK_ROPE
        w_tf = jnp.swapaxes(w_in[l], 0, 1)
        w_kr_rows = w_tf[o_kr:o_cb]
        w_rot_rows = jnp.concatenate([-w_kr_rows[HALF:], w_kr_rows[:HALF]], axis=0)
        lane_pad = jnp.zeros((LANES - QK_ROPE, d), F32)
        w_kr_t = jnp.concatenate([w_kr_rows, lane_pad, w_rot_rows, lane_pad], axis=0)

        wq = w_q_b[l].reshape(Q_LORA, N_HEADS, QK_NOPE + QK_ROPE)
        wq_rope = wq[:, :, QK_NOPE:]
        w_q = jnp.concatenate([
            wq[:, :, :QK_NOPE].reshape(Q_LORA, HEAD_PAD),
            jnp.concatenate([wq_rope, _rot_last(wq_rope)], axis=-1).reshape(Q_LORA, HEAD_PAD)],
            axis=1).astype(BF16)
        wkv = w_kv_b[l].reshape(KV_LORA, N_HEADS, QK_NOPE + V_HEAD)
        w_k = wkv[:, :, :QK_NOPE].reshape(KV_LORA, HEAD_PAD).astype(BF16)
        w_vt = jnp.swapaxes(wkv[:, :, QK_NOPE:].reshape(KV_LORA, HEAD_PAD), 0, 1).astype(BF16)

        qn, kvn, kr_raw, u = _latent(x_cur, scale1, shift1, w_tf, w_kr_t,
                                     g_q_a[l].reshape(1, -1), g_kv_a[l].reshape(1, -1))
        q_nope, q_rope, k_nope, v_t, k_rope = _qkv_up(qn, kvn, kr_raw, pos_b, inv_row,
                                                      w_q, w_k, w_vt)
        attn, (w_cg, w_oa, w_ob, w_om) = _attention(
            q_nope, q_rope, k_nope, k_rope, v_t,
            [(w_tf, o_cb), (w_o_a[l], 0), (w_o_b[l], 0), (w_o[l], 0)])
        cm = _conv_branch(u, w_cg, w_conv[l])
        gates, (w_fi,) = _gates(u, w_cg, 3, [(w_ffn_in[l], 0)])
        merged, (w_fo,) = _merge(attn, cm, w_oa, w_ob, gates, [(w_ffn_out[l], 0)])
        x_cur = _mix_out(merged, w_om, x_cur, gate1,
                         ln1_g[l].reshape(1, -1), ln1_b[l].reshape(1, -1))

        x_cur = _ffn(x_cur, scale2, shift2, gate2, w_fi, w_fo,
                     ln2_g[l].reshape(1, -1), ln2_b[l].reshape(1, -1))

    return x_cur.reshape(bsz, seq, d)
```

```python
import functools
import math

import jax
import jax.numpy as jnp
from jax import lax
from jax.experimental import pallas as pl
from jax.experimental.pallas import tpu as pltpu

F32 = jnp.float32
BF16 = jnp.bfloat16

D_MODEL = 2048
SEQ = 2048
CHUNK = 64
N_HEADS = 16
QK_NOPE = 128
QK_ROPE = 64
V_HEAD = 128
Q_LORA = 512
KV_LORA = 512
ROPE_THETA = 10000.0
ATTN_SCALE = (QK_NOPE + QK_ROPE) ** -0.5
CONV_K = 3
D_FF = 5632
DEPTH = 1
DEEPNORM_ALPHA = (2.0 * DEPTH) ** 0.25
LN_EPS = 1e-5
RMS_EPS = 1e-6

LANES = 128
SUBLANES = 8
HALF = QK_ROPE // 2
HEAD_PAD = N_HEADS * LANES
ATTN_TQ = 256
ATTN_LOOKAHEAD = 4
ATTN_HEADS_PER_STEP = 2
CAST_PIECE_ROWS = 32
UP_HEAD_GROUP = 4
ROW_CHUNK = 256
COL_CHUNK = 512
MXU_WIDTH = 256
Q_PRESCALE = ATTN_SCALE * math.log2(math.e)

VMEM_LIMIT = 58 * 1024 * 1024
FFN_VMEM_LIMIT = 62 * 1024 * 1024


def _params(n_axes, vmem=VMEM_LIMIT):
    return pltpu.CompilerParams(
        dimension_semantics=("arbitrary",) * n_axes, vmem_limit_bytes=vmem)


def _resident(shape, index_map):
    return pl.BlockSpec(shape, index_map, pipeline_mode=pl.Buffered(1))


def _slab_specs(w, n_steps, step_of, row0=0):
    n_rows = w.shape[0] - row0
    cols = w.shape[1]
    bf16_rows = 2 * SUBLANES
    if n_rows % (n_steps * bf16_rows) == 0:
        rb = n_rows // n_steps
        in_spec = pl.BlockSpec(
            (pl.Element(rb), pl.Element(cols)),
            lambda *g: (pl.multiple_of(row0 + step_of(*g) * rb, SUBLANES), 0))
        out_spec = pl.BlockSpec((rb, cols), lambda *g: (step_of(*g), 0))
    else:
        assert row0 == 0 and n_rows % (n_steps // 2 * bf16_rows) == 0
        rb = n_rows // (n_steps // 2)
        in_spec = out_spec = pl.BlockSpec(
            (rb, cols // 2), lambda *g: (step_of(*g) // 2, step_of(*g) % 2))
    return in_spec, out_spec, jax.ShapeDtypeStruct((n_rows, cols), BF16)


def _split_refs(refs, n_in, n_cast):
    return refs[:n_in], refs[n_in:n_in + n_cast], refs[n_in + n_cast], refs[n_in + n_cast + 1:]


def _cast_pieces(cast_src, cast_dst):
    pieces = []
    for src, dst in zip(cast_src, cast_dst, strict=True):
        rows = src.shape[0]
        n_piece = next(n for n in range(max(1, rows // CAST_PIECE_ROWS), 0, -1)
                       if rows % n == 0 and (rows // n) % (2 * SUBLANES) == 0)
        pieces += [(src, dst, slice(p * rows // n_piece, (p + 1) * rows // n_piece))
                   for p in range(n_piece)]
    return pieces


def _convert_pieces(pieces, k, n):
    for src, dst, rows in pieces[k::n]:
        dst[rows, :] = src[rows, :].astype(BF16)


def _col_chunks(n, width):
    chunks = [(c0, width) for c0 in range(0, n - width, width)]
    c0, size = n - width, width
    while size > MXU_WIDTH:
        size //= 2
        chunks.append((c0, size))
        c0 += size
    chunks.append((c0, size))
    return chunks


def _dot(a, b):
    return jnp.dot(a, b, preferred_element_type=F32)


def _dot_nt(a, b_t):
    return lax.dot_general(a, b_t, (((1,), (1,)), ((), ())), preferred_element_type=F32)


def _modulate(x, scale, shift):
    return (x * (1.0 + scale) + shift).astype(BF16)


def _layer_norm(y, g, b):
    mu = jnp.mean(y, axis=-1, keepdims=True)
    yc = y - mu
    var = jnp.mean(yc * yc, axis=-1, keepdims=True)
    return yc * lax.rsqrt(var + LN_EPS) * g + b


def _rms_norm(y, g):
    return y * lax.rsqrt(jnp.mean(y * y, axis=-1, keepdims=True) + RMS_EPS) * g


def _ada_columns(c_act, w_ref, b_ref):
    return _dot(c_act, w_ref[...].astype(BF16)) + b_ref[...]


def _ada_kernel(c_ref, w_ref, b_ref, o_ref, cact_ref):
    c = c_ref[...]
    c_act = (c * jax.nn.sigmoid(c)).astype(BF16)
    cact_ref[...] = c_act
    o_ref[...] = _ada_columns(c_act, w_ref, b_ref)


def _ada_head(c_pad, w_ada, b_ada, n, tn=1024):
    rows, d = c_pad.shape
    return pl.pallas_call(
        _ada_kernel,
        grid=(n // tn,),
        in_specs=[pl.BlockSpec((rows, d), lambda j: (0, 0)),
                  pl.BlockSpec((d, tn), lambda j: (0, j)),
                  pl.BlockSpec((1, tn), lambda j: (0, j))],
        out_specs=[pl.BlockSpec((rows, tn), lambda j: (0, j)),
                   pl.BlockSpec((rows, d), lambda j: (0, 0))],
        out_shape=[jax.ShapeDtypeStruct((rows, n), F32), jax.ShapeDtypeStruct((rows, d), BF16)],
        compiler_params=_params(1),
        name="ada_mod",
    )(c_pad, w_ada, b_ada)


def _latent_kernel(x_ref, sc_ref, sh_ref, w_ref, wkr_ref, gq_ref, gkv_ref,
                   qn_ref, kvn_ref, kr_ref, u_ref, wbf_ref):
    @pl.when(pl.program_id(0) == 0)
    def _():
        wbf_ref[...] = w_ref[...].astype(BF16)

    u = _modulate(x_ref[...], sc_ref[0], sh_ref[0])
    u_ref[...] = u
    p = _dot_nt(u, wbf_ref[...])
    qn_ref[...] = _rms_norm(p[:, :Q_LORA], gq_ref[...]).astype(BF16)
    kvn_ref[...] = _rms_norm(p[:, Q_LORA:], gkv_ref[...]).astype(BF16)
    kr_ref[...] = _dot_nt(u, wkr_ref[...].astype(BF16))


def _latent(x2, scale, shift, w_t, w_kr_t, g_q, g_kv, bm=1024):
    m, d = x2.shape
    per_seq = SEQ // bm
    n_lat = Q_LORA + KV_LORA
    mod_spec = pl.BlockSpec((1, 1, d), lambda i: (i // per_seq, 0, 0))
    return pl.pallas_call(
        _latent_kernel,
        grid=(m // bm,),
        in_specs=[pl.BlockSpec((bm, d), lambda i: (i, 0)),
                  mod_spec, mod_spec,
                  _resident((n_lat, d), lambda i: (0, 0)),
                  pl.BlockSpec((2 * LANES, d), lambda i: (0, 0)),
                  pl.BlockSpec((1, Q_LORA), lambda i: (0, 0)),
                  pl.BlockSpec((1, KV_LORA), lambda i: (0, 0))],
        out_specs=[pl.BlockSpec((bm, Q_LORA), lambda i: (i, 0)),
                   pl.BlockSpec((bm, KV_LORA), lambda i: (i, 0)),
                   pl.BlockSpec((bm, 2 * LANES), lambda i: (i, 0)),
                   pl.BlockSpec((bm, d), lambda i: (i, 0))],
        out_shape=[jax.ShapeDtypeStruct((m, Q_LORA), BF16),
                   jax.ShapeDtypeStruct((m, KV_LORA), BF16),
                   jax.ShapeDtypeStruct((m, 2 * LANES), F32),
                   jax.ShapeDtypeStruct((m, d), BF16)],
        scratch_shapes=[pltpu.VMEM((n_lat, d), BF16)],
        compiler_params=_params(1),
        name="latent_proj",
    )(x2, scale, shift, w_t, w_kr_t, g_q, g_kv)


def _qkv_up_kernel(qn_ref, kvn_ref, krr_ref, pos_ref, inv_ref, wq_ref, wk_ref, wvt_ref,
                   qnope_ref, qrope_ref, knope_ref, vt_ref, kr_ref):
    ang = pos_ref[...].astype(F32) * inv_ref[...]
    cos = jnp.cos(ang)
    sin = jnp.sin(ang)
    qn = qn_ref[...]
    kvn = kvn_ref[...]
    grp = UP_HEAD_GROUP * LANES
    lane = lax.broadcasted_iota(jnp.int32, cos.shape, 1)
    cs = jnp.where(lane < QK_ROPE, cos, sin) * Q_PRESCALE
    for g in range(N_HEADS // UP_HEAD_GROUP):
        lo, hi = g * grp, (g + 1) * grp
        qnope_ref[:, lo:hi] = (_dot(qn, wq_ref[:, lo:hi]) * Q_PRESCALE).astype(BF16)
        p = _dot(qn, wq_ref[:, HEAD_PAD + lo:HEAD_PAD + hi])
        for h in range(UP_HEAD_GROUP):
            t = p[:, h * LANES:(h + 1) * LANES] * cs
            qrope_ref[:, lo + h * LANES:lo + (h + 1) * LANES] = (
                t + pltpu.roll(t, QK_ROPE, 1)).astype(BF16)
        knope_ref[:, lo:hi] = _dot(kvn, wk_ref[:, lo:hi]).astype(BF16)
        vt_ref[lo:hi, :] = _dot_nt(wvt_ref[lo:hi, :], kvn).astype(BF16)
    krr = krr_ref[...]
    kr_ref[...] = (krr[:, :LANES] * cos + krr[:, LANES:] * sin).astype(BF16)


def _qkv_up(qn, kvn, kr_raw, pos_b, inv_row, w_q, w_k, w_vt, bm=512):
    m = qn.shape[0]
    row = lambda n: pl.BlockSpec((bm, n), lambda i: (i, 0))
    const = lambda a: pl.BlockSpec(a.shape, lambda i: (0, 0))
    return pl.pallas_call(
        _qkv_up_kernel,
        grid=(m // bm,),
        in_specs=[row(Q_LORA), row(KV_LORA), row(2 * LANES), row(LANES),
                  const(inv_row), const(w_q), const(w_k), const(w_vt)],
        out_specs=[row(HEAD_PAD), row(HEAD_PAD), row(HEAD_PAD),
                   pl.BlockSpec((HEAD_PAD, bm), lambda i: (0, i)), row(LANES)],
        out_shape=[jax.ShapeDtypeStruct((m, HEAD_PAD), BF16)] * 3
        + [jax.ShapeDtypeStruct((HEAD_PAD, m), BF16), jax.ShapeDtypeStruct((m, LANES), BF16)],
        compiler_params=_params(1),
        name="qkv_up",
    )(qn, kvn, kr_raw, pos_b, inv_row, w_q, w_k, w_vt)


def _conv_kernel(u_ref, wb_ref, wc_ref, wx_ref, wconv_ref, cact_ref, wada_ref, bada_ref,
                 o_ref, mod_ref, halo_ref, *, tiles_per_seq):
    i = pl.program_id(0)
    j = pl.program_id(1)
    mod_ref[...] = _ada_columns(cact_ref[...], wada_ref, bada_ref)
    u = u_ref[...]
    bm, tn = o_ref.shape

    @pl.when(i % tiles_per_seq == 0)
    def _():
        halo_ref[j] = jnp.zeros(halo_ref.shape[1:], F32)

    for c0, width in _col_chunks(tn, COL_CHUNK):
        cols = slice(c0, c0 + width)
        rows = lax.broadcasted_iota(jnp.int32, (bm, width), 0)
        cb = _dot_nt(u, wb_ref[cols, :])
        z = _dot_nt(u, wc_ref[cols, :]) * _dot_nt(u, wx_ref[cols, :])
        prev = halo_ref[j, :, cols]
        z1 = jnp.where(rows == 0, prev[SUBLANES - 1:SUBLANES, :], pltpu.roll(z, 1, 0))
        z2 = jnp.where(rows == 0, prev[SUBLANES - 2:SUBLANES - 1, :],
                       jnp.where(rows == 1, prev[SUBLANES - 1:SUBLANES, :], pltpu.roll(z, 2, 0)))
        halo_ref[j, :, cols] = z[bm - SUBLANES:, :]
        w = wconv_ref[:, cols]
        o_ref[:, cols] = (cb * (w[0:1, :] * z2 + w[1:2, :] * z1 + w[2:3, :] * z)).astype(BF16)


def _conv_branch(u, w_rows, w_conv, c_act, w_ada, b_ada, ada_col0, bm=1024, tn=1024):
    m, d = u.shape
    n = w_conv.shape[1]
    per_seq = SEQ // bm
    nj = n // tn
    rows = c_act.shape[0]
    n_ada = w_ada.shape[1] - ada_col0
    ta = n_ada // (m // bm * nj)
    w_spec = lambda piece: pl.BlockSpec((tn, d), lambda i, j: (piece * nj + j, 0))
    return pl.pallas_call(
        functools.partial(_conv_kernel, tiles_per_seq=per_seq),
        grid=(m // bm, nj),
        in_specs=[pl.BlockSpec((bm, d), lambda i, j: (i, 0)),
                  w_spec(0), w_spec(1), w_spec(2),
                  pl.BlockSpec((CONV_K, tn), lambda i, j: (0, j)),
                  pl.BlockSpec((rows, d), lambda i, j: (0, 0)),
                  pl.BlockSpec((d, ta), lambda i, j: (0, ada_col0 // ta + i * nj + j)),
                  pl.BlockSpec((1, ta), lambda i, j: (0, ada_col0 // ta + i * nj + j))],
        out_specs=[pl.BlockSpec((bm, tn), lambda i, j: (i, j)),
                   pl.BlockSpec((rows, ta), lambda i, j: (0, i * nj + j))],
        out_shape=[jax.ShapeDtypeStruct((m, n), BF16), jax.ShapeDtypeStruct((rows, n_ada), F32)],
        scratch_shapes=[pltpu.VMEM((nj, SUBLANES, tn), F32)],
        compiler_params=_params(2),
        name="conv_branch",
    )(u, w_rows, w_rows, w_rows, w_conv, c_act, w_ada, b_ada)


def _gate_kernel(*refs, n_cast):
    (u_ref, wa_ref, wb_ref), cast_src, o_ref, cast_dst = _split_refs(refs, 3, n_cast)
    pieces = _cast_pieces(cast_src, cast_dst)
    u = u_ref[...]
    n = wa_ref.shape[0]
    passes = ([(wa_ref, 0, c0, 2 * COL_CHUNK) for c0 in range(0, n, 2 * COL_CHUNK)]
              + [(wb_ref, n, c0, width) for c0, width in _col_chunks(n, 2 * COL_CHUNK)])
    for k, (w_ref, out0, c0, width) in enumerate(passes):
        o_ref[:, out0 + c0:out0 + c0 + width] = jax.nn.sigmoid(
            _dot_nt(u, w_ref[c0:c0 + width, :])).astype(BF16)
        _convert_pieces(pieces, k, len(passes))


def _gates(u, w_rows, first_block, cast_jobs, bm=512):
    m, d = u.shape
    specs = [_slab_specs(w, m // bm, lambda i: i, row0) for w, row0 in cast_jobs]
    outs = pl.pallas_call(
        functools.partial(_gate_kernel, n_cast=len(cast_jobs)),
        grid=(m // bm,),
        in_specs=[pl.BlockSpec((bm, d), lambda i: (i, 0)),
                  _resident((d, d), lambda i: (first_block, 0)),
                  _resident((d, d), lambda i: (first_block + 1, 0))] + [s[0] for s in specs],
        out_specs=[pl.BlockSpec((bm, 2 * d), lambda i: (i, 0))] + [s[1] for s in specs],
        out_shape=[jax.ShapeDtypeStruct((m, 2 * d), BF16)] + [s[2] for s in specs],
        compiler_params=_params(1),
        name="merge_gates",
    )(u, w_rows, w_rows, *[w for w, _ in cast_jobs])
    return outs[0], outs[1:]


def _attn_kernel(*refs, n_cast):
    (qn_ref, qr_ref, kn_ref, kr_ref, vt_ref), cast_src, o_ref, cast_dst = _split_refs(refs, 5, n_cast)
    pieces = _cast_pieces(cast_src, cast_dst)
    kr = kr_ref[...]
    ks = [jnp.concatenate([kn_ref[:, a * LANES:(a + 1) * LANES], kr], axis=1)
          for a in range(ATTN_HEADS_PER_STEP)]
    key = lax.broadcasted_iota(jnp.int32, (ATTN_TQ, ATTN_TQ), 0)
    qry = lax.broadcasted_iota(jnp.int32, (ATTN_TQ, ATTN_TQ), 1)
    allowed = (key // CHUNK) <= (qry // CHUNK)
    n_blk = SEQ // ATTN_TQ

    def scores(task):
        a, i = task
        q0 = i * ATTN_TQ
        L = q0 + ATTN_TQ
        lanes = slice(a * LANES, (a + 1) * LANES)
        q = jnp.concatenate([qn_ref[q0:L, lanes], qr_ref[q0:L, lanes]], axis=1)
        return _dot_nt(ks[a][:L], q)

    tasks = [(a, i) for i in range(n_blk - 1, -1, -1) for a in range(ATTN_HEADS_PER_STEP)]
    pending = [scores(t) for t in tasks[:ATTN_LOOKAHEAD]]
    for pos, (a, i) in enumerate(tasks):
        q0 = i * ATTN_TQ
        L = q0 + ATTN_TQ
        s_t = pending.pop(0)
        if pos + ATTN_LOOKAHEAD < len(tasks):
            pending.append(scores(tasks[pos + ATTN_LOOKAHEAD]))
        heads = slice(a * V_HEAD, (a + 1) * V_HEAD)
        s_diag = jnp.where(allowed, s_t[q0:, :], -1e30)
        m_col = jnp.max(s_diag, axis=0, keepdims=True)
        if i > 0:
            s_full = s_t[:q0, :]
            m_col = jnp.maximum(m_col, jnp.max(s_full, axis=0, keepdims=True))
        e_diag = jnp.exp2(s_diag - m_col)
        denom = jnp.sum(e_diag, axis=0, keepdims=True)
        o_t = _dot(vt_ref[heads, q0:L], e_diag.astype(BF16))
        if i > 0:
            e_full = jnp.exp2(s_full - m_col)
            denom = denom + jnp.sum(e_full, axis=0, keepdims=True)
            o_t = o_t + _dot(vt_ref[heads, :q0], e_full.astype(BF16))
        o_ref[q0:L, a * LANES:(a + 1) * LANES] = (o_t / denom).T.astype(BF16)
        _convert_pieces(pieces, pos, len(tasks))


def _attention(q_nope, q_rope, k_nope, k_rope, v_t, cast_jobs):
    m = q_nope.shape[0]
    hps = ATTN_HEADS_PER_STEP
    grid = (m // SEQ, N_HEADS // hps)
    head = pl.BlockSpec((SEQ, hps * LANES), lambda b, h: (b, h))
    specs = [_slab_specs(w, grid[0] * grid[1], lambda b, h: b * grid[1] + h, row0)
             for w, row0 in cast_jobs]
    outs = pl.pallas_call(
        functools.partial(_attn_kernel, n_cast=len(cast_jobs)),
        grid=grid,
        in_specs=[head, head, head, pl.BlockSpec((SEQ, LANES), lambda b, h: (b, 0)),
                  pl.BlockSpec((hps * V_HEAD, SEQ), lambda b, h: (h, b))] + [s[0] for s in specs],
        out_specs=[head] + [s[1] for s in specs],
        out_shape=[jax.ShapeDtypeStruct((m, HEAD_PAD), BF16)] + [s[2] for s in specs],
        compiler_params=_params(2),
        name="mla_attention",
    )(q_nope, q_rope, k_nope, k_rope, v_t, *[w for w, _ in cast_jobs])
    return outs[0], outs[1:]


def _merge_kernel(*refs, n_cast):
    (a_ref, c_ref, wa_ref, wb_ref, g_ref), cast_src, o_ref, cast_dst = _split_refs(refs, 5, n_cast)
    pieces = _cast_pieces(cast_src, cast_dst)
    a = a_ref[...]
    cm = c_ref[...]
    n = o_ref.shape[1]
    chunks = _col_chunks(n, COL_CHUNK)
    for k, (c0, width) in enumerate(chunks):
        cols = slice(c0, c0 + width)
        y_a = _dot(a, wa_ref[:, cols])
        y_b = _dot(cm, wb_ref[:, cols])
        o_ref[:, cols] = (g_ref[:, cols].astype(F32) * y_a
                          + g_ref[:, n + c0:n + c0 + width].astype(F32) * y_b).astype(BF16)
        _convert_pieces(pieces, k, len(chunks))


def _merge(attn, cm, w_oa, w_ob, gates, cast_jobs, bm=512):
    m, d = attn.shape
    n = w_oa.shape[1]
    lhs = pl.BlockSpec((bm, d), lambda i: (i, 0))
    specs = [_slab_specs(w, m // bm, lambda i: i, row0) for w, row0 in cast_jobs]
    outs = pl.pallas_call(
        functools.partial(_merge_kernel, n_cast=len(cast_jobs)),
        grid=(m // bm,),
        in_specs=[lhs, lhs, _resident((d, n), lambda i: (0, 0)), _resident((d, n), lambda i: (0, 0)),
                  pl.BlockSpec((bm, 2 * n), lambda i: (i, 0))] + [s[0] for s in specs],
        out_specs=[pl.BlockSpec((bm, n), lambda i: (i, 0))] + [s[1] for s in specs],
        out_shape=[jax.ShapeDtypeStruct((m, n), BF16)] + [s[2] for s in specs],
        compiler_params=_params(1),
        name="branch_merge",
    )(attn, cm, w_oa, w_ob, gates, *[w for w, _ in cast_jobs])
    return outs[0], outs[1:]


def _mix_out_kernel(mg_ref, w_ref, x_ref, g1_ref, lg_ref, lb_ref, o_ref):
    n_chunks = mg_ref.shape[0] // ROW_CHUNK
    mix = _dot(mg_ref[0:ROW_CHUNK, :], w_ref[...])
    for r in range(n_chunks):
        rows = slice(r * ROW_CHUNK, (r + 1) * ROW_CHUNK)
        cur = mix
        if r + 1 < n_chunks:
            mix = _dot(mg_ref[(r + 1) * ROW_CHUNK:(r + 2) * ROW_CHUNK, :], w_ref[...])
        y = DEEPNORM_ALPHA * x_ref[rows, :] + g1_ref[0] * cur
        o_ref[rows, :] = _layer_norm(y, lg_ref[...], lb_ref[...])


def _mix_out(merged, w_o, x2, gate1, ln_g, ln_b, bm=512):
    m, d = x2.shape
    per_seq = SEQ // bm
    row = pl.BlockSpec((bm, d), lambda i: (i, 0))
    vec = pl.BlockSpec((1, d), lambda i: (0, 0))
    return pl.pallas_call(
        _mix_out_kernel,
        grid=(m // bm,),
        in_specs=[row, _resident((d, d), lambda i: (0, 0)), row,
                  pl.BlockSpec((1, 1, d), lambda i: (i // per_seq, 0, 0)), vec, vec],
        out_specs=row,
        out_shape=jax.ShapeDtypeStruct((m, d), F32),
        compiler_params=_params(1),
        name="mix_out_ln",
    )(merged, w_o, x2, gate1, ln_g, ln_b)


def _ffn_kernel(x_ref, sc_ref, sh_ref, g2_ref, wg_ref, wu_ref, wo_ref, lg_ref, lb_ref, o_ref,
                u_ref):
    acc_ref = o_ref
    j = pl.program_id(1)

    @pl.when(j == 0)
    def _():
        u_ref[...] = _modulate(x_ref[...], sc_ref[0], sh_ref[0])
        acc_ref[...] = jnp.zeros_like(acc_ref)

    u = u_ref[...]
    tf = wg_ref.shape[1]
    half = tf // 2
    hs = []
    for c0 in (0, half):
        hg = _dot(u, wg_ref[:, c0:c0 + half])
        hs.append((hg * jax.nn.sigmoid(hg) * _dot(u, wu_ref[:, c0:c0 + half])).astype(BF16))
    for c0 in range(0, acc_ref.shape[1], COL_CHUNK):
        cols = slice(c0, c0 + COL_CHUNK)
        acc_ref[:, cols] += _dot(hs[0], wo_ref[:half, cols]) + _dot(hs[1], wo_ref[half:, cols])

    @pl.when(j == pl.num_programs(1) - 1)
    def _():
        y = DEEPNORM_ALPHA * x_ref[...] + g2_ref[0] * acc_ref[...]
        o_ref[...] = _layer_norm(y, lg_ref[...], lb_ref[...])


def _ffn(x1, scale, shift, gate, w_in, w_out, ln_g, ln_b, bm=1024, tf=512):
    m, d = x1.shape
    nf = D_FF // tf
    per_seq = SEQ // bm
    row = pl.BlockSpec((bm, d), lambda i, j: (i, 0))
    mod_spec = pl.BlockSpec((1, 1, d), lambda i, j: (i // per_seq, 0, 0))
    vec = pl.BlockSpec((1, d), lambda i, j: (0, 0))
    return pl.pallas_call(
        _ffn_kernel,
        grid=(m // bm, nf),
        in_specs=[row, mod_spec, mod_spec, mod_spec,
                  pl.BlockSpec((d, tf), lambda i, j: (0, j)),
                  pl.BlockSpec((d, tf), lambda i, j: (0, j + nf)),
                  pl.BlockSpec((tf, d), lambda i, j: (j, 0)), vec, vec],
        out_specs=row,
        out_shape=jax.ShapeDtypeStruct((m, d), F32),
        scratch_shapes=[pltpu.VMEM((bm, d), BF16)],
        compiler_params=_params(2, vmem=FFN_VMEM_LIMIT),
        name="swiglu_ffn_ln",
    )(x1, scale, shift, gate, w_in, w_in, w_out, ln_g, ln_b)


def _rot_last(w):
    return jnp.concatenate([-w[..., HALF:], w[..., :HALF]], axis=-1)


def kernel(x, c, positions, w_ada, b_ada, w_in, g_q_a, w_q_b, g_kv_a, w_kv_b, w_o_a,
           w_conv, w_o_b, w_o, ln1_g, ln1_b, w_ffn_in, w_ffn_out, ln2_g, ln2_b):
    bsz, seq, d = x.shape
    assert (seq, d) == (SEQ, D_MODEL) and w_ada.shape[0] == DEPTH
    m = bsz * seq
    x_cur = x.reshape(m, d)

    inv_freq = 1.0 / (ROPE_THETA ** (jnp.arange(0, QK_ROPE, 2, dtype=F32) / QK_ROPE))
    inv_row = jnp.tile(inv_freq, LANES // HALF).reshape(1, LANES)
    pos_b = jnp.broadcast_to(positions.reshape(m, 1), (m, LANES))
    c_pad = jnp.pad(c, ((0, SUBLANES - bsz), (0, 0)))

    for l in range(DEPTH):
        b_ada_row = b_ada[l].reshape(1, -1)
        mod_head, c_act = _ada_head(c_pad, w_ada[l], b_ada_row, 2 * d)
        shift1, scale1 = [t.reshape(bsz, 1, d) for t in jnp.split(mod_head[:bsz], 2, axis=-1)]

        o_kr = Q_LORA + KV_LORA
        o_cb = o_kr + QK_ROPE
        w_tf = jnp.swapaxes(w_in[l], 0, 1)
        w_kr_rows = w_tf[o_kr:o_cb]
        w_rot_rows = jnp.concatenate([-w_kr_rows[HALF:], w_kr_rows[:HALF]], axis=0)
        lane_pad = jnp.zeros((LANES - QK_ROPE, d), F32)
        w_kr_t = jnp.concatenate([w_kr_rows, lane_pad, w_rot_rows, lane_pad], axis=0)

        wq = w_q_b[l].reshape(Q_LORA, N_HEADS, QK_NOPE + QK_ROPE)
        wq_rope = wq[:, :, QK_NOPE:]
        w_q = jnp.concatenate([
            wq[:, :, :QK_NOPE].reshape(Q_LORA, HEAD_PAD),
            jnp.concatenate([wq_rope, _rot_last(wq_rope)], axis=-1).reshape(Q_LORA, HEAD_PAD)],
            axis=1).astype(BF16)
        wkv = w_kv_b[l].reshape(KV_LORA, N_HEADS, QK_NOPE + V_HEAD)
        w_k = wkv[:, :, :QK_NOPE].reshape(KV_LORA, HEAD_PAD).astype(BF16)
        w_vt = jnp.swapaxes(wkv[:, :, QK_NOPE:].reshape(KV_LORA, HEAD_PAD), 0, 1).astype(BF16)

        qn, kvn, kr_raw, u = _latent(x_cur, scale1, shift1, w_tf, w_kr_t,
                                     g_q_a[l].reshape(1, -1), g_kv_a[l].reshape(1, -1))
        q_nope, q_rope, k_nope, v_t, k_rope = _qkv_up(qn, kvn, kr_raw, pos_b, inv_row,
                                                      w_q, w_k, w_vt)
        attn, (w_cg, w_oa, w_ob, w_om) = _attention(
            q_nope, q_rope, k_nope, k_rope, v_t,
            [(w_tf, o_cb), (w_o_a[l], 0), (w_o_b[l], 0), (w_o[l], 0)])
        cm, mod_tail = _conv_branch(u, w_cg, w_conv[l], c_act, w_ada[l], b_ada_row, 2 * d)
        gate1, shift2, scale2, gate2 = [
            t.reshape(bsz, 1, d) for t in jnp.split(mod_tail[:bsz], 4, axis=-1)]
        gates, (w_fi,) = _gates(u, w_cg, 3, [(w_ffn_in[l], 0)])
        merged, (w_fo,) = _merge(attn, cm, w_oa, w_ob, gates, [(w_ffn_out[l], 0)])
        x_cur = _mix_out(merged, w_om, x_cur, gate1,
                         ln1_g[l].reshape(1, -1), ln1_b[l].reshape(1, -1))

        x_cur = _ffn(x_cur, scale2, shift2, gate2, w_fi, w_fo,
                     ln2_g[l].reshape(1, -1), ln2_b[l].reshape(1, -1))

    return x_cur.reshape(bsz, seq, d)
```

```python
import functools
import math

import jax
import jax.numpy as jnp
from jax import lax
from jax.experimental import pallas as pl
from jax.experimental.pallas import tpu as pltpu

F32 = jnp.float32
BF16 = jnp.bfloat16

D_MODEL = 2048
SEQ = 2048
CHUNK = 64
N_HEADS = 16
QK_NOPE = 128
QK_ROPE = 64
V_HEAD = 128
Q_LORA = 512
KV_LORA = 512
ROPE_THETA = 10000.0
ATTN_SCALE = (QK_NOPE + QK_ROPE) ** -0.5
CONV_K = 3
D_FF = 5632
DEPTH = 1
DEEPNORM_ALPHA = (2.0 * DEPTH) ** 0.25
LN_EPS = 1e-5
RMS_EPS = 1e-6

LANES = 128
SUBLANES = 8
HALF = QK_ROPE // 2
HEAD_PAD = N_HEADS * LANES
ATTN_TQ = 256
ATTN_LOOKAHEAD = 4
ATTN_HEADS_PER_STEP = 2
CAST_PIECE_ROWS = 32
UP_HEAD_GROUP = 4
ROW_CHUNK = 256
COL_CHUNK = 512
MXU_WIDTH = 256
Q_PRESCALE = ATTN_SCALE * math.log2(math.e)

VMEM_LIMIT = 58 * 1024 * 1024
FFN_VMEM_LIMIT = 62 * 1024 * 1024


def _params(n_axes, vmem=VMEM_LIMIT):
    return pltpu.CompilerParams(
        dimension_semantics=("arbitrary",) * n_axes, vmem_limit_bytes=vmem)


def _resident(shape, index_map):
    return pl.BlockSpec(shape, index_map, pipeline_mode=pl.Buffered(1))


def _slab_specs(w, n_steps, step_of, row0=0):
    n_rows = w.shape[0] - row0
    cols = w.shape[1]
    bf16_rows = 2 * SUBLANES
    if n_rows % (n_steps * bf16_rows) == 0:
        rb = n_rows // n_steps
        in_spec = pl.BlockSpec(
            (pl.Element(rb), pl.Element(cols)),
            lambda *g: (pl.multiple_of(row0 + step_of(*g) * rb, SUBLANES), 0))
        out_spec = pl.BlockSpec((rb, cols), lambda *g: (step_of(*g), 0))
    else:
        assert row0 == 0 and n_rows % (n_steps // 2 * bf16_rows) == 0
        rb = n_rows // (n_steps // 2)
        in_spec = out_spec = pl.BlockSpec(
            (rb, cols // 2), lambda *g: (step_of(*g) // 2, step_of(*g) % 2))
    return in_spec, out_spec, jax.ShapeDtypeStruct((n_rows, cols), BF16)


def _split_refs(refs, n_in, n_cast):
    return refs[:n_in], refs[n_in:n_in + n_cast], refs[n_in + n_cast], refs[n_in + n_cast + 1:]


def _cast_pieces(cast_src, cast_dst):
    pieces = []
    for src, dst in zip(cast_src, cast_dst, strict=True):
        rows = src.shape[0]
        n_piece = next(n for n in range(max(1, rows // CAST_PIECE_ROWS), 0, -1)
                       if rows % n == 0 and (rows // n) % (2 * SUBLANES) == 0)
        pieces += [(src, dst, slice(p * rows // n_piece, (p + 1) * rows // n_piece))
                   for p in range(n_piece)]
    return pieces


def _convert_pieces(pieces, k, n):
    for src, dst, rows in pieces[k::n]:
        dst[rows, :] = src[rows, :].astype(BF16)


def _col_chunks(n, width):
    chunks = [(c0, width) for c0 in range(0, n - width, width)]
    c0, size = n - width, width
    while size > MXU_WIDTH:
        size //= 2
        chunks.append((c0, size))
        c0 += size
    chunks.append((c0, size))
    return chunks


def _dot(a, b):
    return jnp.dot(a, b, preferred_element_type=F32)


def _dot_nt(a, b_t):
    return lax.dot_general(a, b_t, (((1,), (1,)), ((), ())), preferred_element_type=F32)


def _modulate(x, scale, shift):
    return (x * (1.0 + scale) + shift).astype(BF16)


MOD_SHIFT1, MOD_SCALE1, MOD_GATE1, MOD_SHIFT2, MOD_SCALE2, MOD_GATE2 = range(6)


def _mod_spec(mod, which):
    return pl.BlockSpec((mod.shape[0], D_MODEL), lambda *g: (0, which))


def _batch_row(ref, tile, tiles_per_seq):
    return ref[pl.ds(tile // tiles_per_seq, 1), :]


def _layer_norm(y, g, b):
    mu = jnp.mean(y, axis=-1, keepdims=True)
    yc = y - mu
    var = jnp.mean(yc * yc, axis=-1, keepdims=True)
    return yc * lax.rsqrt(var + LN_EPS) * g + b


def _rms_norm(y, g):
    return y * lax.rsqrt(jnp.mean(y * y, axis=-1, keepdims=True) + RMS_EPS) * g


def _ada_kernel(c_ref, w_ref, b_ref, o_ref):
    c = c_ref[...]
    c_act = (c * jax.nn.sigmoid(c)).astype(BF16)
    o_ref[...] = _dot(c_act, w_ref[...].astype(BF16)) + b_ref[...]


def _ada(c_pad, w_ada, b_ada, tn=1024):
    rows, d = c_pad.shape
    n = w_ada.shape[1]
    return pl.pallas_call(
        _ada_kernel,
        grid=(n // tn,),
        in_specs=[pl.BlockSpec((rows, d), lambda j: (0, 0)),
                  pl.BlockSpec((d, tn), lambda j: (0, j)),
                  pl.BlockSpec((1, tn), lambda j: (0, j))],
        out_specs=pl.BlockSpec((rows, tn), lambda j: (0, j)),
        out_shape=jax.ShapeDtypeStruct((rows, n), F32),
        compiler_params=_params(1),
        name="ada_mod",
    )(c_pad, w_ada, b_ada)


def _latent_kernel(x_ref, sh_ref, sc_ref, w_ref, gq_ref, gkv_ref,
                   qn_ref, kvn_ref, kr_ref, u_ref, wbf_ref, wkr_ref, *, tiles_per_seq):
    n_lat = Q_LORA + KV_LORA

    @pl.when(pl.program_id(0) == 0)
    def _():
        wbf_ref[...] = w_ref[:n_lat, :].astype(BF16)
        wkr_ref[...] = jnp.zeros_like(wkr_ref)
        wkr_ref[0:QK_ROPE, :] = w_ref[n_lat:n_lat + QK_ROPE, :].astype(BF16)
        wkr_ref[LANES:LANES + HALF, :] = (-w_ref[n_lat + HALF:n_lat + QK_ROPE, :]).astype(BF16)
        wkr_ref[LANES + HALF:LANES + QK_ROPE, :] = w_ref[n_lat:n_lat + HALF, :].astype(BF16)

    tile = pl.program_id(0)
    u = _modulate(x_ref[...], _batch_row(sc_ref, tile, tiles_per_seq),
                  _batch_row(sh_ref, tile, tiles_per_seq))
    u_ref[...] = u
    p = _dot_nt(u, wbf_ref[...])
    qn_ref[...] = _rms_norm(p[:, :Q_LORA], gq_ref[...]).astype(BF16)
    kvn_ref[...] = _rms_norm(p[:, Q_LORA:], gkv_ref[...]).astype(BF16)
    kr_ref[...] = _dot_nt(u, wkr_ref[...])


def _latent(x2, mod, w_t, g_q, g_kv, bm=1024):
    m, d = x2.shape
    per_seq = SEQ // bm
    n_lat = Q_LORA + KV_LORA
    return pl.pallas_call(
        functools.partial(_latent_kernel, tiles_per_seq=per_seq),
        grid=(m // bm,),
        in_specs=[pl.BlockSpec((bm, d), lambda i: (i, 0)),
                  _mod_spec(mod, MOD_SHIFT1), _mod_spec(mod, MOD_SCALE1),
                  _resident((n_lat + QK_ROPE, d), lambda i: (0, 0)),
                  pl.BlockSpec((1, Q_LORA), lambda i: (0, 0)),
                  pl.BlockSpec((1, KV_LORA), lambda i: (0, 0))],
        out_specs=[pl.BlockSpec((bm, Q_LORA), lambda i: (i, 0)),
                   pl.BlockSpec((bm, KV_LORA), lambda i: (i, 0)),
                   pl.BlockSpec((bm, 2 * LANES), lambda i: (i, 0)),
                   pl.BlockSpec((bm, d), lambda i: (i, 0))],
        out_shape=[jax.ShapeDtypeStruct((m, Q_LORA), BF16),
                   jax.ShapeDtypeStruct((m, KV_LORA), BF16),
                   jax.ShapeDtypeStruct((m, 2 * LANES), F32),
                   jax.ShapeDtypeStruct((m, d), BF16)],
        scratch_shapes=[pltpu.VMEM((n_lat, d), BF16), pltpu.VMEM((2 * LANES, d), BF16)],
        compiler_params=_params(1),
        name="latent_proj",
    )(x2, mod, mod, w_t, g_q, g_kv)


def _qkv_up_kernel(qn_ref, kvn_ref, krr_ref, pos_ref, inv_ref, wq_raw, wkv_raw,
                   qnope_ref, qrope_ref, knope_ref, vt_ref, kr_ref, wq_ref, wk_ref, wvt_ref):
    @pl.when(pl.program_id(0) == 0)
    def _():
        for h in range(N_HEADS):
            q0 = h * (QK_NOPE + QK_ROPE)
            blk = slice(h * LANES, (h + 1) * LANES)
            wq_ref[:, blk] = wq_raw[:, q0:q0 + QK_NOPE].astype(BF16)
            rope = wq_raw[:, q0 + QK_NOPE:q0 + QK_NOPE + QK_ROPE]
            packed = jnp.concatenate([rope, -rope[:, HALF:], rope[:, :HALF]], axis=1)
            wq_ref[:, HEAD_PAD + h * LANES:HEAD_PAD + (h + 1) * LANES] = packed.astype(BF16)
            k0 = h * (QK_NOPE + V_HEAD)
            wk_ref[:, blk] = wkv_raw[:, k0:k0 + QK_NOPE].astype(BF16)
            wvt_ref[blk, :] = wkv_raw[:, k0 + QK_NOPE:k0 + QK_NOPE + V_HEAD].T.astype(BF16)

    ang = pos_ref[...].astype(F32) * inv_ref[...]
    cos = jnp.cos(ang)
    sin = jnp.sin(ang)
    qn = qn_ref[...]
    kvn = kvn_ref[...]
    grp = UP_HEAD_GROUP * LANES
    lane = lax.broadcasted_iota(jnp.int32, cos.shape, 1)
    cs = jnp.where(lane < QK_ROPE, cos, sin) * Q_PRESCALE
    for g in range(N_HEADS // UP_HEAD_GROUP):
        lo, hi = g * grp, (g + 1) * grp
        qnope_ref[:, lo:hi] = (_dot(qn, wq_ref[:, lo:hi]) * Q_PRESCALE).astype(BF16)
        p = _dot(qn, wq_ref[:, HEAD_PAD + lo:HEAD_PAD + hi])
        for h in range(UP_HEAD_GROUP):
            t = p[:, h * LANES:(h + 1) * LANES] * cs
            qrope_ref[:, lo + h * LANES:lo + (h + 1) * LANES] = (
                t + pltpu.roll(t, QK_ROPE, 1)).astype(BF16)
        knope_ref[:, lo:hi] = _dot(kvn, wk_ref[:, lo:hi]).astype(BF16)
        vt_ref[lo:hi, :] = _dot_nt(wvt_ref[lo:hi, :], kvn).astype(BF16)
    krr = krr_ref[...]
    kr_ref[...] = (krr[:, :LANES] * cos + krr[:, LANES:] * sin).astype(BF16)


def _qkv_up(qn, kvn, kr_raw, pos_col, inv_row, w_q_b, w_kv_b, bm=512):
    m = qn.shape[0]
    row = lambda n: pl.BlockSpec((bm, n), lambda i: (i, 0))
    const = lambda a: _resident(a.shape, lambda i: (0, 0))
    return pl.pallas_call(
        _qkv_up_kernel,
        grid=(m // bm,),
        in_specs=[row(Q_LORA), row(KV_LORA), row(2 * LANES), row(1),
                  const(inv_row), const(w_q_b), const(w_kv_b)],
        out_specs=[row(HEAD_PAD), row(HEAD_PAD), row(HEAD_PAD),
                   pl.BlockSpec((HEAD_PAD, bm), lambda i: (0, i)), row(LANES)],
        out_shape=[jax.ShapeDtypeStruct((m, HEAD_PAD), BF16)] * 3
        + [jax.ShapeDtypeStruct((HEAD_PAD, m), BF16), jax.ShapeDtypeStruct((m, LANES), BF16)],
        scratch_shapes=[pltpu.VMEM((Q_LORA, 2 * HEAD_PAD), BF16),
                        pltpu.VMEM((KV_LORA, HEAD_PAD), BF16),
                        pltpu.VMEM((HEAD_PAD, KV_LORA), BF16)],
        compiler_params=_params(1),
        name="qkv_up",
    )(qn, kvn, kr_raw, pos_col, inv_row, w_q_b, w_kv_b)


def _conv_kernel(u_ref, wb_ref, wc_ref, wx_ref, wconv_ref, o_ref, halo_ref, *, tiles_per_seq):
    i = pl.program_id(0)
    j = pl.program_id(1)
    u = u_ref[...]
    bm, tn = o_ref.shape

    @pl.when(i % tiles_per_seq == 0)
    def _():
        halo_ref[j] = jnp.zeros(halo_ref.shape[1:], F32)

    for c0, width in _col_chunks(tn, COL_CHUNK):
        cols = slice(c0, c0 + width)
        rows = lax.broadcasted_iota(jnp.int32, (bm, width), 0)
        cb = _dot_nt(u, wb_ref[cols, :])
        z = _dot_nt(u, wc_ref[cols, :]) * _dot_nt(u, wx_ref[cols, :])
        prev = halo_ref[j, :, cols]
        z1 = jnp.where(rows == 0, prev[SUBLANES - 1:SUBLANES, :], pltpu.roll(z, 1, 0))
        z2 = jnp.where(rows == 0, prev[SUBLANES - 2:SUBLANES - 1, :],
                       jnp.where(rows == 1, prev[SUBLANES - 1:SUBLANES, :], pltpu.roll(z, 2, 0)))
        halo_ref[j, :, cols] = z[bm - SUBLANES:, :]
        w = wconv_ref[:, cols]
        o_ref[:, cols] = (cb * (w[0:1, :] * z2 + w[1:2, :] * z1 + w[2:3, :] * z)).astype(BF16)


def _conv_branch(u, w_rows, w_conv, bm=1024, tn=1024):
    m, d = u.shape
    n = w_conv.shape[1]
    per_seq = SEQ // bm
    w_spec = lambda piece: pl.BlockSpec((tn, d), lambda i, j: (piece * (n // tn) + j, 0))
    return pl.pallas_call(
        functools.partial(_conv_kernel, tiles_per_seq=per_seq),
        grid=(m // bm, n // tn),
        in_specs=[pl.BlockSpec((bm, d), lambda i, j: (i, 0)),
                  w_spec(0), w_spec(1), w_spec(2),
                  pl.BlockSpec((CONV_K, tn), lambda i, j: (0, j))],
        out_specs=pl.BlockSpec((bm, tn), lambda i, j: (i, j)),
        out_shape=jax.ShapeDtypeStruct((m, n), BF16),
        scratch_shapes=[pltpu.VMEM((n // tn, SUBLANES, tn), F32)],
        compiler_params=_params(2),
        name="conv_branch",
    )(u, w_rows, w_rows, w_rows, w_conv)


def _gate_kernel(*refs, n_cast):
    (u_ref, wa_ref, wb_ref), cast_src, o_ref, cast_dst = _split_refs(refs, 3, n_cast)
    pieces = _cast_pieces(cast_src, cast_dst)
    u = u_ref[...]
    n = wa_ref.shape[0]
    passes = ([(wa_ref, 0, c0, 2 * COL_CHUNK) for c0 in range(0, n, 2 * COL_CHUNK)]
              + [(wb_ref, n, c0, width) for c0, width in _col_chunks(n, 2 * COL_CHUNK)])
    for k, (w_ref, out0, c0, width) in enumerate(passes):
        o_ref[:, out0 + c0:out0 + c0 + width] = jax.nn.sigmoid(
            _dot_nt(u, w_ref[c0:c0 + width, :])).astype(BF16)
        _convert_pieces(pieces, k, len(passes))


def _gates(u, w_rows, first_block, cast_jobs, bm=512):
    m, d = u.shape
    specs = [_slab_specs(w, m // bm, lambda i: i, row0) for w, row0 in cast_jobs]
    outs = pl.pallas_call(
        functools.partial(_gate_kernel, n_cast=len(cast_jobs)),
        grid=(m // bm,),
        in_specs=[pl.BlockSpec((bm, d), lambda i: (i, 0)),
                  _resident((d, d), lambda i: (first_block, 0)),
                  _resident((d, d), lambda i: (first_block + 1, 0))] + [s[0] for s in specs],
        out_specs=[pl.BlockSpec((bm, 2 * d), lambda i: (i, 0))] + [s[1] for s in specs],
        out_shape=[jax.ShapeDtypeStruct((m, 2 * d), BF16)] + [s[2] for s in specs],
        compiler_params=_params(1),
        name="merge_gates",
    )(u, w_rows, w_rows, *[w for w, _ in cast_jobs])
    return outs[0], outs[1:]


def _attn_kernel(*refs, n_cast):
    (qn_ref, qr_ref, kn_ref, kr_ref, vt_ref), cast_src, o_ref, cast_dst = _split_refs(refs, 5, n_cast)
    pieces = _cast_pieces(cast_src, cast_dst)
    kr = kr_ref[...]
    ks = [jnp.concatenate([kn_ref[:, a * LANES:(a + 1) * LANES], kr], axis=1)
          for a in range(ATTN_HEADS_PER_STEP)]
    key = lax.broadcasted_iota(jnp.int32, (ATTN_TQ, ATTN_TQ), 0)
    qry = lax.broadcasted_iota(jnp.int32, (ATTN_TQ, ATTN_TQ), 1)
    allowed = (key // CHUNK) <= (qry // CHUNK)
    n_blk = SEQ // ATTN_TQ

    def scores(task):
        a, i = task
        q0 = i * ATTN_TQ
        L = q0 + ATTN_TQ
        lanes = slice(a * LANES, (a + 1) * LANES)
        q = jnp.concatenate([qn_ref[q0:L, lanes], qr_ref[q0:L, lanes]], axis=1)
        return _dot_nt(ks[a][:L], q)

    tasks = [(a, i) for i in range(n_blk - 1, -1, -1) for a in range(ATTN_HEADS_PER_STEP)]
    pending = [scores(t) for t in tasks[:ATTN_LOOKAHEAD]]
    for pos, (a, i) in enumerate(tasks):
        q0 = i * ATTN_TQ
        L = q0 + ATTN_TQ
        s_t = pending.pop(0)
        if pos + ATTN_LOOKAHEAD < len(tasks):
            pending.append(scores(tasks[pos + ATTN_LOOKAHEAD]))
        heads = slice(a * V_HEAD, (a + 1) * V_HEAD)
        s_diag = jnp.where(allowed, s_t[q0:, :], -1e30)
        m_col = jnp.max(s_diag, axis=0, keepdims=True)
        if i > 0:
            s_full = s_t[:q0, :]
            m_col = jnp.maximum(m_col, jnp.max(s_full, axis=0, keepdims=True))
        e_diag = jnp.exp2(s_diag - m_col)
        denom = jnp.sum(e_diag, axis=0, keepdims=True)
        o_t = _dot(vt_ref[heads, q0:L], e_diag.astype(BF16))
        if i > 0:
            e_full = jnp.exp2(s_full - m_col)
            denom = denom + jnp.sum(e_full, axis=0, keepdims=True)
            o_t = o_t + _dot(vt_ref[heads, :q0], e_full.astype(BF16))
        o_ref[q0:L, a * LANES:(a + 1) * LANES] = (o_t / denom).T.astype(BF16)
        _convert_pieces(pieces, pos, len(tasks))


def _attention(q_nope, q_rope, k_nope, k_rope, v_t, cast_jobs):
    m = q_nope.shape[0]
    hps = ATTN_HEADS_PER_STEP
    grid = (m // SEQ, N_HEADS // hps)
    head = pl.BlockSpec((SEQ, hps * LANES), lambda b, h: (b, h))
    specs = [_slab_specs(w, grid[0] * grid[1], lambda b, h: b * grid[1] + h, row0)
             for w, row0 in cast_jobs]
    outs = pl.pallas_call(
        functools.partial(_attn_kernel, n_cast=len(cast_jobs)),
        grid=grid,
        in_specs=[head, head, head, pl.BlockSpec((SEQ, LANES), lambda b, h: (b, 0)),
                  pl.BlockSpec((hps * V_HEAD, SEQ), lambda b, h: (h, b))] + [s[0] for s in specs],
        out_specs=[head] + [s[1] for s in specs],
        out_shape=[jax.ShapeDtypeStruct((m, HEAD_PAD), BF16)] + [s[2] for s in specs],
        compiler_params=_params(2),
        name="mla_attention",
    )(q_nope, q_rope, k_nope, k_rope, v_t, *[w for w, _ in cast_jobs])
    return outs[0], outs[1:]


def _merge_kernel(*refs, n_cast):
    (a_ref, c_ref, wa_ref, wb_ref, g_ref), cast_src, o_ref, cast_dst = _split_refs(refs, 5, n_cast)
    pieces = _cast_pieces(cast_src, cast_dst)
    a = a_ref[...]
    cm = c_ref[...]
    n = o_ref.shape[1]
    chunks = _col_chunks(n, COL_CHUNK)
    for k, (c0, width) in enumerate(chunks):
        cols = slice(c0, c0 + width)
        y_a = _dot(a, wa_ref[:, cols])
        y_b = _dot(cm, wb_ref[:, cols])
        o_ref[:, cols] = (g_ref[:, cols].astype(F32) * y_a
                          + g_ref[:, n + c0:n + c0 + width].astype(F32) * y_b).astype(BF16)
        _convert_pieces(pieces, k, len(chunks))


def _merge(attn, cm, w_oa, w_ob, gates, cast_jobs, bm=512):
    m, d = attn.shape
    n = w_oa.shape[1]
    lhs = pl.BlockSpec((bm, d), lambda i: (i, 0))
    specs = [_slab_specs(w, m // bm, lambda i: i, row0) for w, row0 in cast_jobs]
    outs = pl.pallas_call(
        functools.partial(_merge_kernel, n_cast=len(cast_jobs)),
        grid=(m // bm,),
        in_specs=[lhs, lhs, _resident((d, n), lambda i: (0, 0)), _resident((d, n), lambda i: (0, 0)),
                  pl.BlockSpec((bm, 2 * n), lambda i: (i, 0))] + [s[0] for s in specs],
        out_specs=[pl.BlockSpec((bm, n), lambda i: (i, 0))] + [s[1] for s in specs],
        out_shape=[jax.ShapeDtypeStruct((m, n), BF16)] + [s[2] for s in specs],
        compiler_params=_params(1),
        name="branch_merge",
    )(attn, cm, w_oa, w_ob, gates, *[w for w, _ in cast_jobs])
    return outs[0], outs[1:]


def _mix_out_kernel(mg_ref, w_ref, x_ref, g1_ref, lg_ref, lb_ref, o_ref, *, tiles_per_seq):
    gate1 = _batch_row(g1_ref, pl.program_id(0), tiles_per_seq)
    n_chunks = mg_ref.shape[0] // ROW_CHUNK
    mix = _dot(mg_ref[0:ROW_CHUNK, :], w_ref[...])
    for r in range(n_chunks):
        rows = slice(r * ROW_CHUNK, (r + 1) * ROW_CHUNK)
        cur = mix
        if r + 1 < n_chunks:
            mix = _dot(mg_ref[(r + 1) * ROW_CHUNK:(r + 2) * ROW_CHUNK, :], w_ref[...])
        y = DEEPNORM_ALPHA * x_ref[rows, :] + gate1 * cur
        o_ref[rows, :] = _layer_norm(y, lg_ref[...], lb_ref[...])


def _mix_out(merged, w_o, x2, mod, ln_g, ln_b, bm=512):
    m, d = x2.shape
    per_seq = SEQ // bm
    row = pl.BlockSpec((bm, d), lambda i: (i, 0))
    vec = pl.BlockSpec((1, d), lambda i: (0, 0))
    return pl.pallas_call(
        functools.partial(_mix_out_kernel, tiles_per_seq=per_seq),
        grid=(m // bm,),
        in_specs=[row, _resident((d, d), lambda i: (0, 0)), row,
                  _mod_spec(mod, MOD_GATE1), vec, vec],
        out_specs=row,
        out_shape=jax.ShapeDtypeStruct((m, d), F32),
        compiler_params=_params(1),
        name="mix_out_ln",
    )(merged, w_o, x2, mod, ln_g, ln_b)


def _ffn_kernel(x_ref, sc_ref, sh_ref, g2_ref, wg_ref, wu_ref, wo_ref, lg_ref, lb_ref, o_ref,
                u_ref, *, tiles_per_seq):
    acc_ref = o_ref
    tile = pl.program_id(0)
    j = pl.program_id(1)

    @pl.when(j == 0)
    def _():
        u_ref[...] = _modulate(x_ref[...], _batch_row(sc_ref, tile, tiles_per_seq),
                               _batch_row(sh_ref, tile, tiles_per_seq))
        acc_ref[...] = jnp.zeros_like(acc_ref)

    u = u_ref[...]
    tf = wg_ref.shape[1]
    half = tf // 2
    hs = []
    for c0 in (0, half):
        hg = _dot(u, wg_ref[:, c0:c0 + half])
        hs.append((hg * jax.nn.sigmoid(hg) * _dot(u, wu_ref[:, c0:c0 + half])).astype(BF16))
    for c0 in range(0, acc_ref.shape[1], COL_CHUNK):
        cols = slice(c0, c0 + COL_CHUNK)
        acc_ref[:, cols] += _dot(hs[0], wo_ref[:half, cols]) + _dot(hs[1], wo_ref[half:, cols])

    @pl.when(j == pl.num_programs(1) - 1)
    def _():
        y = (DEEPNORM_ALPHA * x_ref[...]
             + _batch_row(g2_ref, tile, tiles_per_seq) * acc_ref[...])
        o_ref[...] = _layer_norm(y, lg_ref[...], lb_ref[...])


def _ffn(x1, mod, w_in, w_out, ln_g, ln_b, bm=1024, tf=512):
    m, d = x1.shape
    nf = D_FF // tf
    per_seq = SEQ // bm
    row = pl.BlockSpec((bm, d), lambda i, j: (i, 0))
    vec = pl.BlockSpec((1, d), lambda i, j: (0, 0))
    return pl.pallas_call(
        functools.partial(_ffn_kernel, tiles_per_seq=per_seq),
        grid=(m // bm, nf),
        in_specs=[row, _mod_spec(mod, MOD_SCALE2), _mod_spec(mod, MOD_SHIFT2),
                  _mod_spec(mod, MOD_GATE2),
                  pl.BlockSpec((d, tf), lambda i, j: (0, j)),
                  pl.BlockSpec((d, tf), lambda i, j: (0, j + nf)),
                  pl.BlockSpec((tf, d), lambda i, j: (j, 0)), vec, vec],
        out_specs=row,
        out_shape=jax.ShapeDtypeStruct((m, d), F32),
        scratch_shapes=[pltpu.VMEM((bm, d), BF16)],
        compiler_params=_params(2, vmem=FFN_VMEM_LIMIT),
        name="swiglu_ffn_ln",
    )(x1, mod, mod, mod, w_in, w_in, w_out, ln_g, ln_b)


def kernel(x, c, positions, w_ada, b_ada, w_in, g_q_a, w_q_b, g_kv_a, w_kv_b, w_o_a,
           w_conv, w_o_b, w_o, ln1_g, ln1_b, w_ffn_in, w_ffn_out, ln2_g, ln2_b):
    bsz, seq, d = x.shape
    assert (seq, d) == (SEQ, D_MODEL) and w_ada.shape[0] == DEPTH
    m = bsz * seq
    x_cur = x.reshape(m, d)

    inv_freq = 1.0 / (ROPE_THETA ** (jnp.arange(0, QK_ROPE, 2, dtype=F32) / QK_ROPE))
    inv_row = jnp.tile(inv_freq, LANES // HALF).reshape(1, LANES)
    pos_col = positions.reshape(m, 1)
    c_pad = jnp.pad(c, ((0, SUBLANES - bsz), (0, 0)))

    for l in range(DEPTH):
        mod = _ada(c_pad, w_ada[l], b_ada[l].reshape(1, -1))

        o_cb = Q_LORA + KV_LORA + QK_ROPE
        w_tf = jnp.swapaxes(w_in[l], 0, 1)

        qn, kvn, kr_raw, u = _latent(x_cur, mod, w_tf,
                                     g_q_a[l].reshape(1, -1), g_kv_a[l].reshape(1, -1))
        q_nope, q_rope, k_nope, v_t, k_rope = _qkv_up(qn, kvn, kr_raw, pos_col, inv_row,
                                                      w_q_b[l], w_kv_b[l])
        attn, (w_cg, w_oa, w_ob, w_om) = _attention(
            q_nope, q_rope, k_nope, k_rope, v_t,
            [(w_tf, o_cb), (w_o_a[l], 0), (w_o_b[l], 0), (w_o[l], 0)])
        cm = _conv_branch(u, w_cg, w_conv[l])
        gates, (w_fi,) = _gates(u, w_cg, 3, [(w_ffn_in[l], 0)])
        merged, (w_fo,) = _merge(attn, cm, w_oa, w_ob, gates, [(w_ffn_out[l], 0)])
        x_cur = _mix_out(merged, w_om, x_cur, mod,
                         ln1_g[l].reshape(1, -1), ln1_b[l].reshape(1, -1))

        x_cur = _ffn(x_cur, mod, w_fi, w_fo,
                     ln2_g[l].reshape(1, -1), ln2_b[l].reshape(1, -1))

    return x_cur.reshape(bsz, seq, d)
```

```python
import functools
import math

import jax
import jax.numpy as jnp
from jax import lax
from jax.experimental import pallas as pl
from jax.experimental.pallas import tpu as pltpu

F32 = jnp.float32
BF16 = jnp.bfloat16

D_MODEL = 2048
SEQ = 2048
CHUNK = 64
N_HEADS = 16
QK_NOPE = 128
QK_ROPE = 64
V_HEAD = 128
Q_LORA = 512
KV_LORA = 512
ROPE_THETA = 10000.0
ATTN_SCALE = (QK_NOPE + QK_ROPE) ** -0.5
CONV_K = 3
D_FF = 5632
DEPTH = 1
DEEPNORM_ALPHA = (2.0 * DEPTH) ** 0.25
LN_EPS = 1e-5
RMS_EPS = 1e-6

LANES = 128
SUBLANES = 8
HALF = QK_ROPE // 2
HEAD_PAD = N_HEADS * LANES
ATTN_TQ = 256
ATTN_LOOKAHEAD = 4
ATTN_HEADS_PER_STEP = 2
CAST_PIECE_ROWS = 32
UP_HEAD_GROUP = 4
ROW_CHUNK = 256
COL_CHUNK = 512
MXU_WIDTH = 256
Q_PRESCALE = ATTN_SCALE * math.log2(math.e)

VMEM_LIMIT = 58 * 1024 * 1024
FFN_VMEM_LIMIT = 62 * 1024 * 1024


def _params(n_axes, vmem=VMEM_LIMIT):
    return pltpu.CompilerParams(
        dimension_semantics=("arbitrary",) * n_axes, vmem_limit_bytes=vmem)


def _resident(shape, index_map):
    return pl.BlockSpec(shape, index_map, pipeline_mode=pl.Buffered(1))


def _slab_specs(w, n_steps, step_of, row0=0):
    n_rows = w.shape[0] - row0
    cols = w.shape[1]
    bf16_rows = 2 * SUBLANES
    if n_rows % (n_steps * bf16_rows) == 0:
        rb = n_rows // n_steps
        in_spec = pl.BlockSpec(
            (pl.Element(rb), pl.Element(cols)),
            lambda *g: (pl.multiple_of(row0 + step_of(*g) * rb, SUBLANES), 0))
        out_spec = pl.BlockSpec((rb, cols), lambda *g: (step_of(*g), 0))
    else:
        assert row0 == 0 and n_rows % (n_steps // 2 * bf16_rows) == 0
        rb = n_rows // (n_steps // 2)
        in_spec = out_spec = pl.BlockSpec(
            (rb, cols // 2), lambda *g: (step_of(*g) // 2, step_of(*g) % 2))
    return in_spec, out_spec, jax.ShapeDtypeStruct((n_rows, cols), BF16)


def _split_refs(refs, n_in, n_cast):
    return refs[:n_in], refs[n_in:n_in + n_cast], refs[n_in + n_cast], refs[n_in + n_cast + 1:]


def _cast_pieces(cast_src, cast_dst):
    pieces = []
    for src, dst in zip(cast_src, cast_dst, strict=True):
        rows = src.shape[0]
        n_piece = next(n for n in range(max(1, rows // CAST_PIECE_ROWS), 0, -1)
                       if rows % n == 0 and (rows // n) % (2 * SUBLANES) == 0)
        pieces += [(src, dst, slice(p * rows // n_piece, (p + 1) * rows // n_piece))
                   for p in range(n_piece)]
    return pieces


def _convert_pieces(pieces, k, n):
    for src, dst, rows in pieces[k::n]:
        dst[rows, :] = src[rows, :].astype(BF16)


def _col_chunks(n, width):
    chunks = [(c0, width) for c0 in range(0, n - width, width)]
    c0, size = n - width, width
    while size > MXU_WIDTH:
        size //= 2
        chunks.append((c0, size))
        c0 += size
    chunks.append((c0, size))
    return chunks


def _dot(a, b):
    return jnp.dot(a, b, preferred_element_type=F32)


def _dot_nt(a, b_t):
    return lax.dot_general(a, b_t, (((1,), (1,)), ((), ())), preferred_element_type=F32)


def _modulate(x, scale, shift):
    return (x * (1.0 + scale) + shift).astype(BF16)


MOD_SHIFT1, MOD_SCALE1, MOD_GATE1, MOD_SHIFT2, MOD_SCALE2, MOD_GATE2 = range(6)


def _mod_spec(mod, which):
    return pl.BlockSpec((mod.shape[0], D_MODEL), lambda *g: (0, which))


def _batch_row(ref, tile, tiles_per_seq):
    return ref[pl.ds(tile // tiles_per_seq, 1), :]


def _layer_norm(y, g, b):
    mu = jnp.mean(y, axis=-1, keepdims=True)
    yc = y - mu
    var = jnp.mean(yc * yc, axis=-1, keepdims=True)
    return yc * lax.rsqrt(var + LN_EPS) * g + b


def _rms_norm(y, g):
    return y * lax.rsqrt(jnp.mean(y * y, axis=-1, keepdims=True) + RMS_EPS) * g


def _ada_kernel(c_ref, w_ref, b_ref, o_ref):
    c = c_ref[...]
    c_act = (c * jax.nn.sigmoid(c)).astype(BF16)
    o_ref[...] = _dot(c_act, w_ref[...].astype(BF16)) + b_ref[...]


def _ada(c_pad, w_ada, b_ada, tn=1024):
    rows, d = c_pad.shape
    n = w_ada.shape[1]
    return pl.pallas_call(
        _ada_kernel,
        grid=(n // tn,),
        in_specs=[pl.BlockSpec((rows, d), lambda j: (0, 0)),
                  pl.BlockSpec((d, tn), lambda j: (0, j)),
                  pl.BlockSpec((1, tn), lambda j: (0, j))],
        out_specs=pl.BlockSpec((rows, tn), lambda j: (0, j)),
        out_shape=jax.ShapeDtypeStruct((rows, n), F32),
        compiler_params=_params(1),
        name="ada_mod",
    )(c_pad, w_ada, b_ada)


def _latent_kernel(x_ref, sh_ref, sc_ref, w_ref, gq_ref, gkv_ref,
                   qn_ref, kvn_ref, kr_ref, u_ref, wbf_ref, wkr_ref, *, tiles_per_seq):
    n_lat = Q_LORA + KV_LORA

    @pl.when(pl.program_id(0) == 0)
    def _():
        wbf_ref[...] = w_ref[:n_lat, :].astype(BF16)
        wkr_ref[...] = jnp.zeros_like(wkr_ref)
        wkr_ref[0:QK_ROPE, :] = w_ref[n_lat:n_lat + QK_ROPE, :].astype(BF16)
        wkr_ref[LANES:LANES + HALF, :] = (-w_ref[n_lat + HALF:n_lat + QK_ROPE, :]).astype(BF16)
        wkr_ref[LANES + HALF:LANES + QK_ROPE, :] = w_ref[n_lat:n_lat + HALF, :].astype(BF16)

    tile = pl.program_id(0)
    u = _modulate(x_ref[...], _batch_row(sc_ref, tile, tiles_per_seq),
                  _batch_row(sh_ref, tile, tiles_per_seq))
    u_ref[...] = u
    p = _dot_nt(u, wbf_ref[...])
    qn_ref[...] = _rms_norm(p[:, :Q_LORA], gq_ref[...]).astype(BF16)
    kvn_ref[...] = _rms_norm(p[:, Q_LORA:], gkv_ref[...]).astype(BF16)
    kr_ref[...] = _dot_nt(u, wkr_ref[...])


def _latent(x2, mod, w_t, g_q, g_kv, bm=1024):
    m, d = x2.shape
    per_seq = SEQ // bm
    n_lat = Q_LORA + KV_LORA
    return pl.pallas_call(
        functools.partial(_latent_kernel, tiles_per_seq=per_seq),
        grid=(m // bm,),
        in_specs=[pl.BlockSpec((bm, d), lambda i: (i, 0)),
                  _mod_spec(mod, MOD_SHIFT1), _mod_spec(mod, MOD_SCALE1),
                  _resident((n_lat + QK_ROPE, d), lambda i: (0, 0)),
                  pl.BlockSpec((1, Q_LORA), lambda i: (0, 0)),
                  pl.BlockSpec((1, KV_LORA), lambda i: (0, 0))],
        out_specs=[pl.BlockSpec((bm, Q_LORA), lambda i: (i, 0)),
                   pl.BlockSpec((bm, KV_LORA), lambda i: (i, 0)),
                   pl.BlockSpec((bm, 2 * LANES), lambda i: (i, 0)),
                   pl.BlockSpec((bm, d), lambda i: (i, 0))],
        out_shape=[jax.ShapeDtypeStruct((m, Q_LORA), BF16),
                   jax.ShapeDtypeStruct((m, KV_LORA), BF16),
                   jax.ShapeDtypeStruct((m, 2 * LANES), F32),
                   jax.ShapeDtypeStruct((m, d), BF16)],
        scratch_shapes=[pltpu.VMEM((n_lat, d), BF16), pltpu.VMEM((2 * LANES, d), BF16)],
        compiler_params=_params(1),
        name="latent_proj",
    )(x2, mod, mod, w_t, g_q, g_kv)


def _qkv_up_kernel(qn_ref, kvn_ref, krr_ref, pos_ref, inv_ref, wq_raw, wkv_raw,
                   qnope_ref, qrope_ref, knope_ref, vt_ref, kr_ref, wq_ref, wk_ref, wvt_ref):
    @pl.when(pl.program_id(0) == 0)
    def _():
        for h in range(N_HEADS):
            q0 = h * (QK_NOPE + QK_ROPE)
            blk = slice(h * LANES, (h + 1) * LANES)
            wq_ref[:, blk] = wq_raw[:, q0:q0 + QK_NOPE].astype(BF16)
            rope = wq_raw[:, q0 + QK_NOPE:q0 + QK_NOPE + QK_ROPE]
            packed = jnp.concatenate([rope, -rope[:, HALF:], rope[:, :HALF]], axis=1)
            wq_ref[:, HEAD_PAD + h * LANES:HEAD_PAD + (h + 1) * LANES] = packed.astype(BF16)
            k0 = h * (QK_NOPE + V_HEAD)
            wk_ref[:, blk] = wkv_raw[:, k0:k0 + QK_NOPE].astype(BF16)
            wvt_ref[blk, :] = wkv_raw[:, k0 + QK_NOPE:k0 + QK_NOPE + V_HEAD].T.astype(BF16)

    ang = pos_ref[...].astype(F32) * inv_ref[...]
    cos = jnp.cos(ang)
    sin = jnp.sin(ang)
    qn = qn_ref[...]
    kvn = kvn_ref[...]
    grp = UP_HEAD_GROUP * LANES
    lane = lax.broadcasted_iota(jnp.int32, cos.shape, 1)
    cs = jnp.where(lane < QK_ROPE, cos, sin) * Q_PRESCALE
    for g in range(N_HEADS // UP_HEAD_GROUP):
        lo, hi = g * grp, (g + 1) * grp
        qnope_ref[:, lo:hi] = (_dot(qn, wq_ref[:, lo:hi]) * Q_PRESCALE).astype(BF16)
        p = _dot(qn, wq_ref[:, HEAD_PAD + lo:HEAD_PAD + hi])
        for h in range(UP_HEAD_GROUP):
            t = p[:, h * LANES:(h + 1) * LANES] * cs
            qrope_ref[:, lo + h * LANES:lo + (h + 1) * LANES] = (
                t + pltpu.roll(t, QK_ROPE, 1)).astype(BF16)
        knope_ref[:, lo:hi] = _dot(kvn, wk_ref[:, lo:hi]).astype(BF16)
        vt_ref[lo:hi, :] = _dot_nt(wvt_ref[lo:hi, :], kvn).astype(BF16)
    krr = krr_ref[...]
    kr_ref[...] = (krr[:, :LANES] * cos + krr[:, LANES:] * sin).astype(BF16)


def _qkv_up(qn, kvn, kr_raw, pos_col, inv_row, w_q_b, w_kv_b, bm=512):
    m = qn.shape[0]
    row = lambda n: pl.BlockSpec((bm, n), lambda i: (i, 0))
    const = lambda a: _resident(a.shape, lambda i: (0, 0))
    return pl.pallas_call(
        _qkv_up_kernel,
        grid=(m // bm,),
        in_specs=[row(Q_LORA), row(KV_LORA), row(2 * LANES), row(1),
                  const(inv_row), const(w_q_b), const(w_kv_b)],
        out_specs=[row(HEAD_PAD), row(HEAD_PAD), row(HEAD_PAD),
                   pl.BlockSpec((HEAD_PAD, bm), lambda i: (0, i)), row(LANES)],
        out_shape=[jax.ShapeDtypeStruct((m, HEAD_PAD), BF16)] * 3
        + [jax.ShapeDtypeStruct((HEAD_PAD, m), BF16), jax.ShapeDtypeStruct((m, LANES), BF16)],
        scratch_shapes=[pltpu.VMEM((Q_LORA, 2 * HEAD_PAD), BF16),
                        pltpu.VMEM((KV_LORA, HEAD_PAD), BF16),
                        pltpu.VMEM((HEAD_PAD, KV_LORA), BF16)],
        compiler_params=_params(1),
        name="qkv_up",
    )(qn, kvn, kr_raw, pos_col, inv_row, w_q_b, w_kv_b)


def _conv_kernel(u_ref, wb_ref, wc_ref, wx_ref, wconv_ref, o_ref, halo_ref, *, tiles_per_seq):
    i = pl.program_id(0)
    j = pl.program_id(1)
    u = u_ref[...]
    bm, tn = o_ref.shape

    @pl.when(i % tiles_per_seq == 0)
    def _():
        halo_ref[j] = jnp.zeros(halo_ref.shape[1:], F32)

    for c0, width in _col_chunks(tn, MXU_WIDTH):
        cols = slice(c0, c0 + width)
        rows = lax.broadcasted_iota(jnp.int32, (bm, width), 0)
        cb = _dot_nt(u, wb_ref[cols, :])
        z = _dot_nt(u, wc_ref[cols, :]) * _dot_nt(u, wx_ref[cols, :])
        prev = halo_ref[j, :, cols]
        z1 = jnp.where(rows == 0, prev[SUBLANES - 1:SUBLANES, :], pltpu.roll(z, 1, 0))
        z2 = jnp.where(rows == 0, prev[SUBLANES - 2:SUBLANES - 1, :],
                       jnp.where(rows == 1, prev[SUBLANES - 1:SUBLANES, :], pltpu.roll(z, 2, 0)))
        halo_ref[j, :, cols] = z[bm - SUBLANES:, :]
        w = wconv_ref[:, cols]
        o_ref[:, cols] = (cb * (w[0:1, :] * z2 + w[1:2, :] * z1 + w[2:3, :] * z)).astype(BF16)


def _conv_branch(u, w_rows, w_conv, bm=1024, tn=1024):
    m, d = u.shape
    n = w_conv.shape[1]
    per_seq = SEQ // bm
    w_spec = lambda piece: pl.BlockSpec((tn, d), lambda i, j: (piece * (n // tn) + j, 0))
    return pl.pallas_call(
        functools.partial(_conv_kernel, tiles_per_seq=per_seq),
        grid=(m // bm, n // tn),
        in_specs=[pl.BlockSpec((bm, d), lambda i, j: (i, 0)),
                  w_spec(0), w_spec(1), w_spec(2),
                  pl.BlockSpec((CONV_K, tn), lambda i, j: (0, j))],
        out_specs=pl.BlockSpec((bm, tn), lambda i, j: (i, j)),
        out_shape=jax.ShapeDtypeStruct((m, n), BF16),
        scratch_shapes=[pltpu.VMEM((n // tn, SUBLANES, tn), F32)],
        compiler_params=_params(2),
        name="conv_branch",
    )(u, w_rows, w_rows, w_rows, w_conv)


def _gate_kernel(*refs, n_cast):
    (u_ref, wa_ref, wb_ref), cast_src, o_ref, cast_dst = _split_refs(refs, 3, n_cast)
    pieces = _cast_pieces(cast_src, cast_dst)
    u = u_ref[...]
    n = wa_ref.shape[0]
    passes = ([(wa_ref, 0, c0, 2 * COL_CHUNK) for c0 in range(0, n, 2 * COL_CHUNK)]
              + [(wb_ref, n, c0, width) for c0, width in _col_chunks(n, 2 * COL_CHUNK)])
    for k, (w_ref, out0, c0, width) in enumerate(passes):
        o_ref[:, out0 + c0:out0 + c0 + width] = jax.nn.sigmoid(
            _dot_nt(u, w_ref[c0:c0 + width, :])).astype(BF16)
        _convert_pieces(pieces, k, len(passes))


def _gates(u, w_rows, first_block, cast_jobs, bm=512):
    m, d = u.shape
    specs = [_slab_specs(w, m // bm, lambda i: i, row0) for w, row0 in cast_jobs]
    outs = pl.pallas_call(
        functools.partial(_gate_kernel, n_cast=len(cast_jobs)),
        grid=(m // bm,),
        in_specs=[pl.BlockSpec((bm, d), lambda i: (i, 0)),
                  _resident((d, d), lambda i: (first_block, 0)),
                  _resident((d, d), lambda i: (first_block + 1, 0))] + [s[0] for s in specs],
        out_specs=[pl.BlockSpec((bm, 2 * d), lambda i: (i, 0))] + [s[1] for s in specs],
        out_shape=[jax.ShapeDtypeStruct((m, 2 * d), BF16)] + [s[2] for s in specs],
        compiler_params=_params(1),
        name="merge_gates",
    )(u, w_rows, w_rows, *[w for w, _ in cast_jobs])
    return outs[0], outs[1:]


def _attn_kernel(*refs, n_cast):
    (qn_ref, qr_ref, kn_ref, kr_ref, vt_ref), cast_src, o_ref, cast_dst = _split_refs(refs, 5, n_cast)
    pieces = _cast_pieces(cast_src, cast_dst)
    kr = kr_ref[...]
    ks = [jnp.concatenate([kn_ref[:, a * LANES:(a + 1) * LANES], kr], axis=1)
          for a in range(ATTN_HEADS_PER_STEP)]
    key = lax.broadcasted_iota(jnp.int32, (ATTN_TQ, ATTN_TQ), 0)
    qry = lax.broadcasted_iota(jnp.int32, (ATTN_TQ, ATTN_TQ), 1)
    allowed = (key // CHUNK) <= (qry // CHUNK)
    n_blk = SEQ // ATTN_TQ

    def scores(task):
        a, i = task
        q0 = i * ATTN_TQ
        L = q0 + ATTN_TQ
        lanes = slice(a * LANES, (a + 1) * LANES)
        q = jnp.concatenate([qn_ref[q0:L, lanes], qr_ref[q0:L, lanes]], axis=1)
        return _dot_nt(ks[a][:L], q)

    tasks = [(a, i) for i in range(n_blk - 1, -1, -1) for a in range(ATTN_HEADS_PER_STEP)]
    pending = [scores(t) for t in tasks[:ATTN_LOOKAHEAD]]
    for pos, (a, i) in enumerate(tasks):
        q0 = i * ATTN_TQ
        L = q0 + ATTN_TQ
        s_t = pending.pop(0)
        if pos + ATTN_LOOKAHEAD < len(tasks):
            pending.append(scores(tasks[pos + ATTN_LOOKAHEAD]))
        heads = slice(a * V_HEAD, (a + 1) * V_HEAD)
        s_diag = jnp.where(allowed, s_t[q0:, :], -1e30)
        m_col = jnp.max(s_diag, axis=0, keepdims=True)
        if i > 0:
            s_full = s_t[:q0, :]
            m_col = jnp.maximum(m_col, jnp.max(s_full, axis=0, keepdims=True))
        e_diag = jnp.exp2(s_diag - m_col)
        denom = jnp.sum(e_diag, axis=0, keepdims=True)
        o_t = _dot(vt_ref[heads, q0:L], e_diag.astype(BF16))
        if i > 0:
            e_full = jnp.exp2(s_full - m_col)
            denom = denom + jnp.sum(e_full, axis=0, keepdims=True)
            o_t = o_t + _dot(vt_ref[heads, :q0], e_full.astype(BF16))
        o_ref[q0:L, a * LANES:(a + 1) * LANES] = (o_t / denom).T.astype(BF16)
        _convert_pieces(pieces, pos, len(tasks))


def _attention(q_nope, q_rope, k_nope, k_rope, v_t, cast_jobs):
    m = q_nope.shape[0]
    hps = ATTN_HEADS_PER_STEP
    grid = (m // SEQ, N_HEADS // hps)
    head = pl.BlockSpec((SEQ, hps * LANES), lambda b, h: (b, h))
    specs = [_slab_specs(w, grid[0] * grid[1], lambda b, h: b * grid[1] + h, row0)
             for w, row0 in cast_jobs]
    outs = pl.pallas_call(
        functools.partial(_attn_kernel, n_cast=len(cast_jobs)),
        grid=grid,
        in_specs=[head, head, head, pl.BlockSpec((SEQ, LANES), lambda b, h: (b, 0)),
                  pl.BlockSpec((hps * V_HEAD, SEQ), lambda b, h: (h, b))] + [s[0] for s in specs],
        out_specs=[head] + [s[1] for s in specs],
        out_shape=[jax.ShapeDtypeStruct((m, HEAD_PAD), BF16)] + [s[2] for s in specs],
        compiler_params=_params(2),
        name="mla_attention",
    )(q_nope, q_rope, k_nope, k_rope, v_t, *[w for w, _ in cast_jobs])
    return outs[0], outs[1:]


def _merge_kernel(*refs, n_cast):
    (a_ref, c_ref, wa_ref, wb_ref, g_ref), cast_src, o_ref, cast_dst = _split_refs(refs, 5, n_cast)
    pieces = _cast_pieces(cast_src, cast_dst)
    a = a_ref[...]
    cm = c_ref[...]
    n = o_ref.shape[1]
    chunks = _col_chunks(n, MXU_WIDTH)
    for k, (c0, width) in enumerate(chunks):
        cols = slice(c0, c0 + width)
        y_a = _dot(a, wa_ref[:, cols])
        y_b = _dot(cm, wb_ref[:, cols])
        o_ref[:, cols] = (g_ref[:, cols].astype(F32) * y_a
                          + g_ref[:, n + c0:n + c0 + width].astype(F32) * y_b).astype(BF16)
        _convert_pieces(pieces, k, len(chunks))


def _merge(attn, cm, w_oa, w_ob, gates, cast_jobs, bm=512):
    m, d = attn.shape
    n = w_oa.shape[1]
    lhs = pl.BlockSpec((bm, d), lambda i: (i, 0))
    specs = [_slab_specs(w, m // bm, lambda i: i, row0) for w, row0 in cast_jobs]
    outs = pl.pallas_call(
        functools.partial(_merge_kernel, n_cast=len(cast_jobs)),
        grid=(m // bm,),
        in_specs=[lhs, lhs, _resident((d, n), lambda i: (0, 0)), _resident((d, n), lambda i: (0, 0)),
                  pl.BlockSpec((bm, 2 * n), lambda i: (i, 0))] + [s[0] for s in specs],
        out_specs=[pl.BlockSpec((bm, n), lambda i: (i, 0))] + [s[1] for s in specs],
        out_shape=[jax.ShapeDtypeStruct((m, n), BF16)] + [s[2] for s in specs],
        compiler_params=_params(1),
        name="branch_merge",
    )(attn, cm, w_oa, w_ob, gates, *[w for w, _ in cast_jobs])
    return outs[0], outs[1:]


def _mix_out_kernel(mg_ref, w_ref, x_ref, g1_ref, lg_ref, lb_ref, o_ref, *, tiles_per_seq):
    gate1 = _batch_row(g1_ref, pl.program_id(0), tiles_per_seq)
    n_chunks = mg_ref.shape[0] // ROW_CHUNK
    mix = _dot(mg_ref[0:ROW_CHUNK, :], w_ref[...])
    for r in range(n_chunks):
        rows = slice(r * ROW_CHUNK, (r + 1) * ROW_CHUNK)
        cur = mix
        if r + 1 < n_chunks:
            mix = _dot(mg_ref[(r + 1) * ROW_CHUNK:(r + 2) * ROW_CHUNK, :], w_ref[...])
        y = DEEPNORM_ALPHA * x_ref[rows, :] + gate1 * cur
        o_ref[rows, :] = _layer_norm(y, lg_ref[...], lb_ref[...])


def _mix_out(merged, w_o, x2, mod, ln_g, ln_b, bm=512):
    m, d = x2.shape
    per_seq = SEQ // bm
    row = pl.BlockSpec((bm, d), lambda i: (i, 0))
    vec = pl.BlockSpec((1, d), lambda i: (0, 0))
    return pl.pallas_call(
        functools.partial(_mix_out_kernel, tiles_per_seq=per_seq),
        grid=(m // bm,),
        in_specs=[row, _resident((d, d), lambda i: (0, 0)), row,
                  _mod_spec(mod, MOD_GATE1), vec, vec],
        out_specs=row,
        out_shape=jax.ShapeDtypeStruct((m, d), F32),
        compiler_params=_params(1),
        name="mix_out_ln",
    )(merged, w_o, x2, mod, ln_g, ln_b)


def _ffn_kernel(x_ref, sc_ref, sh_ref, g2_ref, wg_ref, wu_ref, wo_ref, lg_ref, lb_ref, o_ref,
                u_ref, *, tiles_per_seq):
    acc_ref = o_ref
    tile = pl.program_id(0)
    j = pl.program_id(1)

    @pl.when(j == 0)
    def _():
        u_ref[...] = _modulate(x_ref[...], _batch_row(sc_ref, tile, tiles_per_seq),
                               _batch_row(sh_ref, tile, tiles_per_seq))
        acc_ref[...] = jnp.zeros_like(acc_ref)

    u = u_ref[...]
    tf = wg_ref.shape[1]
    half = tf // 2
    hs = []
    for c0 in (0, half):
        hg = _dot(u, wg_ref[:, c0:c0 + half])
        hs.append((hg * jax.nn.sigmoid(hg) * _dot(u, wu_ref[:, c0:c0 + half])).astype(BF16))
    for c0 in range(0, acc_ref.shape[1], COL_CHUNK):
        cols = slice(c0, c0 + COL_CHUNK)
        acc_ref[:, cols] += _dot(hs[0], wo_ref[:half, cols]) + _dot(hs[1], wo_ref[half:, cols])

    @pl.when(j == pl.num_programs(1) - 1)
    def _():
        y = (DEEPNORM_ALPHA * x_ref[...]
             + _batch_row(g2_ref, tile, tiles_per_seq) * acc_ref[...])
        o_ref[...] = _layer_norm(y, lg_ref[...], lb_ref[...])


def _ffn(x1, mod, w_in, w_out, ln_g, ln_b, bm=1024, tf=512):
    m, d = x1.shape
    nf = D_FF // tf
    per_seq = SEQ // bm
    row = pl.BlockSpec((bm, d), lambda i, j: (i, 0))
    vec = pl.BlockSpec((1, d), lambda i, j: (0, 0))
    return pl.pallas_call(
        functools.partial(_ffn_kernel, tiles_per_seq=per_seq),
        grid=(m // bm, nf),
        in_specs=[row, _mod_spec(mod, MOD_SCALE2), _mod_spec(mod, MOD_SHIFT2),
                  _mod_spec(mod, MOD_GATE2),
                  pl.BlockSpec((d, tf), lambda i, j: (0, j)),
                  pl.BlockSpec((d, tf), lambda i, j: (0, j + nf)),
                  pl.BlockSpec((tf, d), lambda i, j: (j, 0)), vec, vec],
        out_specs=row,
        out_shape=jax.ShapeDtypeStruct((m, d), F32),
        scratch_shapes=[pltpu.VMEM((bm, d), BF16)],
        compiler_params=_params(2, vmem=FFN_VMEM_LIMIT),
        name="swiglu_ffn_ln",
    )(x1, mod, mod, mod, w_in, w_in, w_out, ln_g, ln_b)


def kernel(x, c, positions, w_ada, b_ada, w_in, g_q_a, w_q_b, g_kv_a, w_kv_b, w_o_a,
           w_conv, w_o_b, w_o, ln1_g, ln1_b, w_ffn_in, w_ffn_out, ln2_g, ln2_b):
    bsz, seq, d = x.shape
    assert (seq, d) == (SEQ, D_MODEL) and w_ada.shape[0] == DEPTH
    m = bsz * seq
    x_cur = x.reshape(m, d)

    inv_freq = 1.0 / (ROPE_THETA ** (jnp.arange(0, QK_ROPE, 2, dtype=F32) / QK_ROPE))
    inv_row = jnp.tile(inv_freq, LANES // HALF).reshape(1, LANES)
    pos_col = positions.reshape(m, 1)
    c_pad = jnp.pad(c, ((0, SUBLANES - bsz), (0, 0)))

    for l in range(DEPTH):
        mod = _ada(c_pad, w_ada[l], b_ada[l].reshape(1, -1))

        o_cb = Q_LORA + KV_LORA + QK_ROPE
        w_tf = jnp.swapaxes(w_in[l], 0, 1)

        qn, kvn, kr_raw, u = _latent(x_cur, mod, w_tf,
                                     g_q_a[l].reshape(1, -1), g_kv_a[l].reshape(1, -1))
        q_nope, q_rope, k_nope, v_t, k_rope = _qkv_up(qn, kvn, kr_raw, pos_col, inv_row,
                                                      w_q_b[l], w_kv_b[l])
        attn, (w_cg, w_oa, w_ob, w_om) = _attention(
            q_nope, q_rope, k_nope, k_rope, v_t,
            [(w_tf, o_cb), (w_o_a[l], 0), (w_o_b[l], 0), (w_o[l], 0)])
        cm = _conv_branch(u, w_cg, w_conv[l])
        gates, (w_fi,) = _gates(u, w_cg, 3, [(w_ffn_in[l], 0)])
        merged, (w_fo,) = _merge(attn, cm, w_oa, w_ob, gates, [(w_ffn_out[l], 0)])
        x_cur = _mix_out(merged, w_om, x_cur, mod,
                         ln1_g[l].reshape(1, -1), ln1_b[l].reshape(1, -1))

        x_cur = _ffn(x_cur, mod, w_fi, w_fo,
                     ln2_g[l].reshape(1, -1), ln2_b[l].reshape(1, -1))

    return x_cur.reshape(bsz, seq, d)
```

```python
import functools
import math

import jax
import jax.numpy as jnp
from jax import lax
from jax.experimental import pallas as pl
from jax.experimental.pallas import tpu as pltpu

F32 = jnp.float32
BF16 = jnp.bfloat16

D_MODEL = 2048
SEQ = 2048
CHUNK = 64
N_HEADS = 16
QK_NOPE = 128
QK_ROPE = 64
V_HEAD = 128
Q_LORA = 512
KV_LORA = 512
ROPE_THETA = 10000.0
ATTN_SCALE = (QK_NOPE + QK_ROPE) ** -0.5
CONV_K = 3
D_FF = 5632
DEPTH = 1
DEEPNORM_ALPHA = (2.0 * DEPTH) ** 0.25
LN_EPS = 1e-5
RMS_EPS = 1e-6

LANES = 128
SUBLANES = 8
HALF = QK_ROPE // 2
HEAD_PAD = N_HEADS * LANES
ATTN_TQ = 256
ATTN_LOOKAHEAD = 4
ATTN_HEADS_PER_STEP = 2
CAST_PIECE_ROWS = 32
UP_HEAD_GROUP = 4
ROW_CHUNK = 256
COL_CHUNK = 512
MXU_WIDTH = 256
Q_PRESCALE = ATTN_SCALE * math.log2(math.e)

VMEM_LIMIT = 58 * 1024 * 1024
FFN_VMEM_LIMIT = 62 * 1024 * 1024


def _params(n_axes, vmem=VMEM_LIMIT):
    return pltpu.CompilerParams(
        dimension_semantics=("arbitrary",) * n_axes, vmem_limit_bytes=vmem)


def _resident(shape, index_map):
    return pl.BlockSpec(shape, index_map, pipeline_mode=pl.Buffered(1))


def _slab_specs(w, n_steps, step_of, row0=0):
    n_rows = w.shape[0] - row0
    cols = w.shape[1]
    bf16_rows = 2 * SUBLANES
    if n_rows % (n_steps * bf16_rows) == 0:
        rb = n_rows // n_steps
        in_spec = pl.BlockSpec(
            (pl.Element(rb), pl.Element(cols)),
            lambda *g: (pl.multiple_of(row0 + step_of(*g) * rb, SUBLANES), 0))
        out_spec = pl.BlockSpec((rb, cols), lambda *g: (step_of(*g), 0))
    else:
        assert row0 == 0 and n_rows % (n_steps // 2 * bf16_rows) == 0
        rb = n_rows // (n_steps // 2)
        in_spec = out_spec = pl.BlockSpec(
            (rb, cols // 2), lambda *g: (step_of(*g) // 2, step_of(*g) % 2))
    return in_spec, out_spec, jax.ShapeDtypeStruct((n_rows, cols), BF16)


def _split_refs(refs, n_in, n_cast):
    return refs[:n_in], refs[n_in:n_in + n_cast], refs[n_in + n_cast], refs[n_in + n_cast + 1:]


def _cast_pieces(cast_src, cast_dst):
    pieces = []
    for src, dst in zip(cast_src, cast_dst, strict=True):
        rows = src.shape[0]
        n_piece = next(n for n in range(max(1, rows // CAST_PIECE_ROWS), 0, -1)
                       if rows % n == 0 and (rows // n) % (2 * SUBLANES) == 0)
        pieces += [(src, dst, slice(p * rows // n_piece, (p + 1) * rows // n_piece))
                   for p in range(n_piece)]
    return pieces


def _convert_pieces(pieces, k, n):
    for src, dst, rows in pieces[k::n]:
        dst[rows, :] = src[rows, :].astype(BF16)


def _col_chunks(n, width):
    chunks = [(c0, width) for c0 in range(0, n - width, width)]
    c0, size = n - width, width
    while size > MXU_WIDTH:
        size //= 2
        chunks.append((c0, size))
        c0 += size
    chunks.append((c0, size))
    return chunks


def _dot(a, b):
    return jnp.dot(a, b, preferred_element_type=F32)


def _dot_nt(a, b_t):
    return lax.dot_general(a, b_t, (((1,), (1,)), ((), ())), preferred_element_type=F32)


def _modulate(x, scale, shift):
    return (x * (1.0 + scale) + shift).astype(BF16)


MOD_SHIFT1, MOD_SCALE1, MOD_GATE1, MOD_SHIFT2, MOD_SCALE2, MOD_GATE2 = range(6)


def _mod_spec(mod, which):
    return pl.BlockSpec((mod.shape[0], D_MODEL), lambda *g: (0, which))


def _batch_row(ref, tile, tiles_per_seq):
    return ref[pl.ds(tile // tiles_per_seq, 1), :]


def _layer_norm(y, g, b):
    mu = jnp.mean(y, axis=-1, keepdims=True)
    yc = y - mu
    var = jnp.mean(yc * yc, axis=-1, keepdims=True)
    return yc * lax.rsqrt(var + LN_EPS) * g + b


def _rms_norm(y, g):
    return y * lax.rsqrt(jnp.mean(y * y, axis=-1, keepdims=True) + RMS_EPS) * g


def _ada_kernel(c_ref, w_ref, b_ref, o_ref):
    c = c_ref[...]
    c_act = (c * jax.nn.sigmoid(c)).astype(BF16)
    o_ref[...] = _dot(c_act, w_ref[...].astype(BF16)) + b_ref[...]


def _ada(c_pad, w_ada, b_ada, tn=2048):
    rows, d = c_pad.shape
    n = w_ada.shape[1]
    return pl.pallas_call(
        _ada_kernel,
        grid=(n // tn,),
        in_specs=[pl.BlockSpec((rows, d), lambda j: (0, 0)),
                  pl.BlockSpec((d, tn), lambda j: (0, j)),
                  pl.BlockSpec((1, tn), lambda j: (0, j))],
        out_specs=pl.BlockSpec((rows, tn), lambda j: (0, j)),
        out_shape=jax.ShapeDtypeStruct((rows, n), F32),
        compiler_params=_params(1),
        name="ada_mod",
    )(c_pad, w_ada, b_ada)


def _latent_kernel(x_ref, sh_ref, sc_ref, w_ref, gq_ref, gkv_ref,
                   qn_ref, kvn_ref, kr_ref, u_ref, wbf_ref, wkr_ref, *, tiles_per_seq):
    n_lat = Q_LORA + KV_LORA

    @pl.when(pl.program_id(0) == 0)
    def _():
        wbf_ref[...] = w_ref[:n_lat, :].astype(BF16)
        wkr_ref[...] = jnp.zeros_like(wkr_ref)
        wkr_ref[0:QK_ROPE, :] = w_ref[n_lat:n_lat + QK_ROPE, :].astype(BF16)
        wkr_ref[LANES:LANES + HALF, :] = (-w_ref[n_lat + HALF:n_lat + QK_ROPE, :]).astype(BF16)
        wkr_ref[LANES + HALF:LANES + QK_ROPE, :] = w_ref[n_lat:n_lat + HALF, :].astype(BF16)

    tile = pl.program_id(0)
    u = _modulate(x_ref[...], _batch_row(sc_ref, tile, tiles_per_seq),
                  _batch_row(sh_ref, tile, tiles_per_seq))
    u_ref[...] = u
    p = _dot_nt(u, wbf_ref[...])
    qn_ref[...] = _rms_norm(p[:, :Q_LORA], gq_ref[...]).astype(BF16)
    kvn_ref[...] = _rms_norm(p[:, Q_LORA:], gkv_ref[...]).astype(BF16)
    kr_ref[...] = _dot_nt(u, wkr_ref[...])


def _latent(x2, mod, w_t, g_q, g_kv, bm=1024):
    m, d = x2.shape
    per_seq = SEQ // bm
    n_lat = Q_LORA + KV_LORA
    return pl.pallas_call(
        functools.partial(_latent_kernel, tiles_per_seq=per_seq),
        grid=(m // bm,),
        in_specs=[pl.BlockSpec((bm, d), lambda i: (i, 0)),
                  _mod_spec(mod, MOD_SHIFT1), _mod_spec(mod, MOD_SCALE1),
                  _resident((n_lat + QK_ROPE, d), lambda i: (0, 0)),
                  pl.BlockSpec((1, Q_LORA), lambda i: (0, 0)),
                  pl.BlockSpec((1, KV_LORA), lambda i: (0, 0))],
        out_specs=[pl.BlockSpec((bm, Q_LORA), lambda i: (i, 0)),
                   pl.BlockSpec((bm, KV_LORA), lambda i: (i, 0)),
                   pl.BlockSpec((bm, 2 * LANES), lambda i: (i, 0)),
                   pl.BlockSpec((bm, d), lambda i: (i, 0))],
        out_shape=[jax.ShapeDtypeStruct((m, Q_LORA), BF16),
                   jax.ShapeDtypeStruct((m, KV_LORA), BF16),
                   jax.ShapeDtypeStruct((m, 2 * LANES), F32),
                   jax.ShapeDtypeStruct((m, d), BF16)],
        scratch_shapes=[pltpu.VMEM((n_lat, d), BF16), pltpu.VMEM((2 * LANES, d), BF16)],
        compiler_params=_params(1),
        name="latent_proj",
    )(x2, mod, mod, w_t, g_q, g_kv)


def _qkv_up_kernel(qn_ref, kvn_ref, krr_ref, pos_ref, inv_ref, wq_raw, wkv_raw,
                   qnope_ref, qrope_ref, knope_ref, vt_ref, kr_ref, wq_ref, wk_ref, wvt_ref):
    @pl.when(pl.program_id(0) == 0)
    def _():
        for h in range(N_HEADS):
            q0 = h * (QK_NOPE + QK_ROPE)
            blk = slice(h * LANES, (h + 1) * LANES)
            wq_ref[:, blk] = wq_raw[:, q0:q0 + QK_NOPE].astype(BF16)
            rope = wq_raw[:, q0 + QK_NOPE:q0 + QK_NOPE + QK_ROPE]
            packed = jnp.concatenate([rope, -rope[:, HALF:], rope[:, :HALF]], axis=1)
            wq_ref[:, HEAD_PAD + h * LANES:HEAD_PAD + (h + 1) * LANES] = packed.astype(BF16)
            k0 = h * (QK_NOPE + V_HEAD)
            wk_ref[:, blk] = wkv_raw[:, k0:k0 + QK_NOPE].astype(BF16)
            wvt_ref[blk, :] = wkv_raw[:, k0 + QK_NOPE:k0 + QK_NOPE + V_HEAD].T.astype(BF16)

    ang = pos_ref[...].astype(F32) * inv_ref[...]
    cos = jnp.cos(ang)
    sin = jnp.sin(ang)
    qn = qn_ref[...]
    kvn = kvn_ref[...]
    grp = UP_HEAD_GROUP * LANES
    lane = lax.broadcasted_iota(jnp.int32, cos.shape, 1)
    cs = jnp.where(lane < QK_ROPE, cos, sin) * Q_PRESCALE
    for g in range(N_HEADS // UP_HEAD_GROUP):
        lo, hi = g * grp, (g + 1) * grp
        qnope_ref[:, lo:hi] = (_dot(qn, wq_ref[:, lo:hi]) * Q_PRESCALE).astype(BF16)
        p = _dot(qn, wq_ref[:, HEAD_PAD + lo:HEAD_PAD + hi])
        for h in range(UP_HEAD_GROUP):
            t = p[:, h * LANES:(h + 1) * LANES] * cs
            qrope_ref[:, lo + h * LANES:lo + (h + 1) * LANES] = (
                t + pltpu.roll(t, QK_ROPE, 1)).astype(BF16)
        knope_ref[:, lo:hi] = _dot(kvn, wk_ref[:, lo:hi]).astype(BF16)
        vt_ref[lo:hi, :] = _dot_nt(wvt_ref[lo:hi, :], kvn).astype(BF16)
    krr = krr_ref[...]
    kr_ref[...] = (krr[:, :LANES] * cos + krr[:, LANES:] * sin).astype(BF16)


def _qkv_up(qn, kvn, kr_raw, pos_col, inv_row, w_q_b, w_kv_b, bm=512):
    m = qn.shape[0]
    row = lambda n: pl.BlockSpec((bm, n), lambda i: (i, 0))
    const = lambda a: _resident(a.shape, lambda i: (0, 0))
    return pl.pallas_call(
        _qkv_up_kernel,
        grid=(m // bm,),
        in_specs=[row(Q_LORA), row(KV_LORA), row(2 * LANES), row(1),
                  const(inv_row), const(w_q_b), const(w_kv_b)],
        out_specs=[row(HEAD_PAD), row(HEAD_PAD), row(HEAD_PAD),
                   pl.BlockSpec((HEAD_PAD, bm), lambda i: (0, i)), row(LANES)],
        out_shape=[jax.ShapeDtypeStruct((m, HEAD_PAD), BF16)] * 3
        + [jax.ShapeDtypeStruct((HEAD_PAD, m), BF16), jax.ShapeDtypeStruct((m, LANES), BF16)],
        scratch_shapes=[pltpu.VMEM((Q_LORA, 2 * HEAD_PAD), BF16),
                        pltpu.VMEM((KV_LORA, HEAD_PAD), BF16),
                        pltpu.VMEM((HEAD_PAD, KV_LORA), BF16)],
        compiler_params=_params(1),
        name="qkv_up",
    )(qn, kvn, kr_raw, pos_col, inv_row, w_q_b, w_kv_b)


def _conv_kernel(*refs, n_cast, tiles_per_seq):
    (u_ref, wb_ref, wc_ref, wx_ref, wconv_ref), cast_src, o_ref, rest = _split_refs(refs, 5, n_cast)
    cast_dst, halo_ref = rest[:n_cast], rest[n_cast]
    pieces = _cast_pieces(cast_src, cast_dst)
    i = pl.program_id(0)
    j = pl.program_id(1)
    u = u_ref[...]
    bm, tn = o_ref.shape

    @pl.when(i % tiles_per_seq == 0)
    def _():
        halo_ref[j] = jnp.zeros(halo_ref.shape[1:], F32)

    chunks = _col_chunks(tn, MXU_WIDTH)
    for k, (c0, width) in enumerate(chunks):
        cols = slice(c0, c0 + width)
        rows = lax.broadcasted_iota(jnp.int32, (bm, width), 0)
        cb = _dot_nt(u, wb_ref[cols, :])
        z = _dot_nt(u, wc_ref[cols, :]) * _dot_nt(u, wx_ref[cols, :])
        prev = halo_ref[j, :, cols]
        z1 = jnp.where(rows == 0, prev[SUBLANES - 1:SUBLANES, :], pltpu.roll(z, 1, 0))
        z2 = jnp.where(rows == 0, prev[SUBLANES - 2:SUBLANES - 1, :],
                       jnp.where(rows == 1, prev[SUBLANES - 1:SUBLANES, :], pltpu.roll(z, 2, 0)))
        halo_ref[j, :, cols] = z[bm - SUBLANES:, :]
        w = wconv_ref[:, cols]
        o_ref[:, cols] = (cb * (w[0:1, :] * z2 + w[1:2, :] * z1 + w[2:3, :] * z)).astype(BF16)
        _convert_pieces(pieces, k, len(chunks))


def _conv_branch(u, w_rows, w_conv, cast_jobs, bm=1024, tn=1024):
    m, d = u.shape
    n = w_conv.shape[1]
    per_seq = SEQ // bm
    nj = n // tn
    w_spec = lambda piece: pl.BlockSpec((tn, d), lambda i, j: (piece * nj + j, 0))
    specs = [_slab_specs(w, m // bm * nj, lambda i, j: i * nj + j, row0) for w, row0 in cast_jobs]
    outs = pl.pallas_call(
        functools.partial(_conv_kernel, n_cast=len(cast_jobs), tiles_per_seq=per_seq),
        grid=(m // bm, nj),
        in_specs=[pl.BlockSpec((bm, d), lambda i, j: (i, 0)),
                  w_spec(0), w_spec(1), w_spec(2),
                  pl.BlockSpec((CONV_K, tn), lambda i, j: (0, j))] + [s[0] for s in specs],
        out_specs=[pl.BlockSpec((bm, tn), lambda i, j: (i, j))] + [s[1] for s in specs],
        out_shape=[jax.ShapeDtypeStruct((m, n), BF16)] + [s[2] for s in specs],
        scratch_shapes=[pltpu.VMEM((nj, SUBLANES, tn), F32)],
        compiler_params=_params(2),
        name="conv_branch",
    )(u, w_rows, w_rows, w_rows, w_conv, *[w for w, _ in cast_jobs])
    return outs[0], outs[1:]


def _gate_kernel(*refs, n_cast):
    (u_ref, wa_ref, wb_ref), cast_src, o_ref, cast_dst = _split_refs(refs, 3, n_cast)
    pieces = _cast_pieces(cast_src, cast_dst)
    u = u_ref[...]
    n = wa_ref.shape[0]
    passes = ([(wa_ref, 0, c0, 2 * COL_CHUNK) for c0 in range(0, n, 2 * COL_CHUNK)]
              + [(wb_ref, n, c0, width) for c0, width in _col_chunks(n, 2 * COL_CHUNK)])
    for k, (w_ref, out0, c0, width) in enumerate(passes):
        o_ref[:, out0 + c0:out0 + c0 + width] = jax.nn.sigmoid(
            _dot_nt(u, w_ref[c0:c0 + width, :])).astype(BF16)
        _convert_pieces(pieces, k, len(passes))


def _gates(u, w_rows, first_block, cast_jobs, bm=512):
    m, d = u.shape
    specs = [_slab_specs(w, m // bm, lambda i: i, row0) for w, row0 in cast_jobs]
    outs = pl.pallas_call(
        functools.partial(_gate_kernel, n_cast=len(cast_jobs)),
        grid=(m // bm,),
        in_specs=[pl.BlockSpec((bm, d), lambda i: (i, 0)),
                  _resident((d, d), lambda i: (first_block, 0)),
                  _resident((d, d), lambda i: (first_block + 1, 0))] + [s[0] for s in specs],
        out_specs=[pl.BlockSpec((bm, 2 * d), lambda i: (i, 0))] + [s[1] for s in specs],
        out_shape=[jax.ShapeDtypeStruct((m, 2 * d), BF16)] + [s[2] for s in specs],
        compiler_params=_params(1),
        name="merge_gates",
    )(u, w_rows, w_rows, *[w for w, _ in cast_jobs])
    return outs[0], outs[1:]


def _attn_kernel(*refs, n_cast):
    (qn_ref, qr_ref, kn_ref, kr_ref, vt_ref), cast_src, o_ref, cast_dst = _split_refs(refs, 5, n_cast)
    pieces = _cast_pieces(cast_src, cast_dst)
    kr = kr_ref[...]
    ks = [jnp.concatenate([kn_ref[:, a * LANES:(a + 1) * LANES], kr], axis=1)
          for a in range(ATTN_HEADS_PER_STEP)]
    key = lax.broadcasted_iota(jnp.int32, (ATTN_TQ, ATTN_TQ), 0)
    qry = lax.broadcasted_iota(jnp.int32, (ATTN_TQ, ATTN_TQ), 1)
    allowed = (key // CHUNK) <= (qry // CHUNK)
    n_blk = SEQ // ATTN_TQ

    def scores(task):
        a, i = task
        q0 = i * ATTN_TQ
        L = q0 + ATTN_TQ
        lanes = slice(a * LANES, (a + 1) * LANES)
        q = jnp.concatenate([qn_ref[q0:L, lanes], qr_ref[q0:L, lanes]], axis=1)
        return _dot_nt(ks[a][:L], q)

    tasks = [(a, i) for i in range(n_blk - 1, -1, -1) for a in range(ATTN_HEADS_PER_STEP)]
    pending = [scores(t) for t in tasks[:ATTN_LOOKAHEAD]]
    for pos, (a, i) in enumerate(tasks):
        q0 = i * ATTN_TQ
        L = q0 + ATTN_TQ
        s_t = pending.pop(0)
        if pos + ATTN_LOOKAHEAD < len(tasks):
            pending.append(scores(tasks[pos + ATTN_LOOKAHEAD]))
        heads = slice(a * V_HEAD, (a + 1) * V_HEAD)
        s_diag = jnp.where(allowed, s_t[q0:, :], -1e30)
        m_col = jnp.max(s_diag, axis=0, keepdims=True)
        if i > 0:
            s_full = s_t[:q0, :]
            m_col = jnp.maximum(m_col, jnp.max(s_full, axis=0, keepdims=True))
        e_diag = jnp.exp2(s_diag - m_col)
        denom = jnp.sum(e_diag, axis=0, keepdims=True)
        o_t = _dot(vt_ref[heads, q0:L], e_diag.astype(BF16))
        if i > 0:
            e_full = jnp.exp2(s_full - m_col)
            denom = denom + jnp.sum(e_full, axis=0, keepdims=True)
            o_t = o_t + _dot(vt_ref[heads, :q0], e_full.astype(BF16))
        o_ref[q0:L, a * LANES:(a + 1) * LANES] = (o_t / denom).T.astype(BF16)
        _convert_pieces(pieces, pos, len(tasks))


def _attention(q_nope, q_rope, k_nope, k_rope, v_t, cast_jobs):
    m = q_nope.shape[0]
    hps = ATTN_HEADS_PER_STEP
    grid = (m // SEQ, N_HEADS // hps)
    head = pl.BlockSpec((SEQ, hps * LANES), lambda b, h: (b, h))
    specs = [_slab_specs(w, grid[0] * grid[1], lambda b, h: b * grid[1] + h, row0)
             for w, row0 in cast_jobs]
    outs = pl.pallas_call(
        functools.partial(_attn_kernel, n_cast=len(cast_jobs)),
        grid=grid,
        in_specs=[head, head, head, pl.BlockSpec((SEQ, LANES), lambda b, h: (b, 0)),
                  pl.BlockSpec((hps * V_HEAD, SEQ), lambda b, h: (h, b))] + [s[0] for s in specs],
        out_specs=[head] + [s[1] for s in specs],
        out_shape=[jax.ShapeDtypeStruct((m, HEAD_PAD), BF16)] + [s[2] for s in specs],
        compiler_params=_params(2),
        name="mla_attention",
    )(q_nope, q_rope, k_nope, k_rope, v_t, *[w for w, _ in cast_jobs])
    return outs[0], outs[1:]


def _merge_kernel(*refs, n_cast):
    (a_ref, c_ref, wa_ref, wb_ref, g_ref), cast_src, o_ref, cast_dst = _split_refs(refs, 5, n_cast)
    pieces = _cast_pieces(cast_src, cast_dst)
    a = a_ref[...]
    cm = c_ref[...]
    n = o_ref.shape[1]
    chunks = _col_chunks(n, MXU_WIDTH)
    for k, (c0, width) in enumerate(chunks):
        cols = slice(c0, c0 + width)
        y_a = _dot(a, wa_ref[:, cols])
        y_b = _dot(cm, wb_ref[:, cols])
        o_ref[:, cols] = (g_ref[:, cols].astype(F32) * y_a
                          + g_ref[:, n + c0:n + c0 + width].astype(F32) * y_b).astype(BF16)
        _convert_pieces(pieces, k, len(chunks))


def _merge(attn, cm, w_oa, w_ob, gates, cast_jobs, bm=512):
    m, d = attn.shape
    n = w_oa.shape[1]
    lhs = pl.BlockSpec((bm, d), lambda i: (i, 0))
    specs = [_slab_specs(w, m // bm, lambda i: i, row0) for w, row0 in cast_jobs]
    outs = pl.pallas_call(
        functools.partial(_merge_kernel, n_cast=len(cast_jobs)),
        grid=(m // bm,),
        in_specs=[lhs, lhs, _resident((d, n), lambda i: (0, 0)), _resident((d, n), lambda i: (0, 0)),
                  pl.BlockSpec((bm, 2 * n), lambda i: (i, 0))] + [s[0] for s in specs],
        out_specs=[pl.BlockSpec((bm, n), lambda i: (i, 0))] + [s[1] for s in specs],
        out_shape=[jax.ShapeDtypeStruct((m, n), BF16)] + [s[2] for s in specs],
        compiler_params=_params(1),
        name="branch_merge",
    )(attn, cm, w_oa, w_ob, gates, *[w for w, _ in cast_jobs])
    return outs[0], outs[1:]


def _mix_out_kernel(mg_ref, w_ref, x_ref, g1_ref, lg_ref, lb_ref, o_ref, *, tiles_per_seq):
    gate1 = _batch_row(g1_ref, pl.program_id(0), tiles_per_seq)
    n_chunks = mg_ref.shape[0] // ROW_CHUNK
    mix = _dot(mg_ref[0:ROW_CHUNK, :], w_ref[...])
    for r in range(n_chunks):
        rows = slice(r * ROW_CHUNK, (r + 1) * ROW_CHUNK)
        cur = mix
        if r + 1 < n_chunks:
            mix = _dot(mg_ref[(r + 1) * ROW_CHUNK:(r + 2) * ROW_CHUNK, :], w_ref[...])
        y = DEEPNORM_ALPHA * x_ref[rows, :] + gate1 * cur
        o_ref[rows, :] = _layer_norm(y, lg_ref[...], lb_ref[...])


def _mix_out(merged, w_o, x2, mod, ln_g, ln_b, bm=512):
    m, d = x2.shape
    per_seq = SEQ // bm
    row = pl.BlockSpec((bm, d), lambda i: (i, 0))
    vec = pl.BlockSpec((1, d), lambda i: (0, 0))
    return pl.pallas_call(
        functools.partial(_mix_out_kernel, tiles_per_seq=per_seq),
        grid=(m // bm,),
        in_specs=[row, _resident((d, d), lambda i: (0, 0)), row,
                  _mod_spec(mod, MOD_GATE1), vec, vec],
        out_specs=row,
        out_shape=jax.ShapeDtypeStruct((m, d), F32),
        compiler_params=_params(1),
        name="mix_out_ln",
    )(merged, w_o, x2, mod, ln_g, ln_b)


def _ffn_kernel(x_ref, sc_ref, sh_ref, g2_ref, wg_ref, wu_ref, wo_ref, lg_ref, lb_ref, o_ref,
                u_ref, *, tiles_per_seq):
    acc_ref = o_ref
    tile = pl.program_id(0)
    j = pl.program_id(1)

    @pl.when(j == 0)
    def _():
        u_ref[...] = _modulate(x_ref[...], _batch_row(sc_ref, tile, tiles_per_seq),
                               _batch_row(sh_ref, tile, tiles_per_seq))
        acc_ref[...] = jnp.zeros_like(acc_ref)

    u = u_ref[...]
    tf = wg_ref.shape[1]
    half = tf // 2
    hs = []
    for c0 in (0, half):
        hg = _dot(u, wg_ref[:, c0:c0 + half])
        hs.append((hg * jax.nn.sigmoid(hg) * _dot(u, wu_ref[:, c0:c0 + half])).astype(BF16))
    for c0 in range(0, acc_ref.shape[1], COL_CHUNK):
        cols = slice(c0, c0 + COL_CHUNK)
        acc_ref[:, cols] += _dot(hs[0], wo_ref[:half, cols]) + _dot(hs[1], wo_ref[half:, cols])

    @pl.when(j == pl.num_programs(1) - 1)
    def _():
        y = (DEEPNORM_ALPHA * x_ref[...]
             + _batch_row(g2_ref, tile, tiles_per_seq) * acc_ref[...])
        o_ref[...] = _layer_norm(y, lg_ref[...], lb_ref[...])


def _ffn(x1, mod, w_in, w_out, ln_g, ln_b, bm=1024, tf=512):
    m, d = x1.shape
    nf = D_FF // tf
    per_seq = SEQ // bm
    row = pl.BlockSpec((bm, d), lambda i, j: (i, 0))
    vec = pl.BlockSpec((1, d), lambda i, j: (0, 0))
    return pl.pallas_call(
        functools.partial(_ffn_kernel, tiles_per_seq=per_seq),
        grid=(m // bm, nf),
        in_specs=[row, _mod_spec(mod, MOD_SCALE2), _mod_spec(mod, MOD_SHIFT2),
                  _mod_spec(mod, MOD_GATE2),
                  pl.BlockSpec((d, tf), lambda i, j: (0, j)),
                  pl.BlockSpec((d, tf), lambda i, j: (0, j + nf)),
                  pl.BlockSpec((tf, d), lambda i, j: (j, 0)), vec, vec],
        out_specs=row,
        out_shape=jax.ShapeDtypeStruct((m, d), F32),
        scratch_shapes=[pltpu.VMEM((bm, d), BF16)],
        compiler_params=_params(2, vmem=FFN_VMEM_LIMIT),
        name="swiglu_ffn_ln",
    )(x1, mod, mod, mod, w_in, w_in, w_out, ln_g, ln_b)


def kernel(x, c, positions, w_ada, b_ada, w_in, g_q_a, w_q_b, g_kv_a, w_kv_b, w_o_a,
           w_conv, w_o_b, w_o, ln1_g, ln1_b, w_ffn_in, w_ffn_out, ln2_g, ln2_b):
    bsz, seq, d = x.shape
    assert (seq, d) == (SEQ, D_MODEL) and w_ada.shape[0] == DEPTH
    m = bsz * seq
    x_cur = x.reshape(m, d)

    inv_freq = 1.0 / (ROPE_THETA ** (jnp.arange(0, QK_ROPE, 2, dtype=F32) / QK_ROPE))
    inv_row = jnp.tile(inv_freq, LANES // HALF).reshape(1, LANES)
    pos_col = positions.reshape(m, 1)
    c_pad = jnp.pad(c, ((0, SUBLANES - bsz), (0, 0)))

    for l in range(DEPTH):
        mod = _ada(c_pad, w_ada[l], b_ada[l].reshape(1, -1))

        o_cb = Q_LORA + KV_LORA + QK_ROPE
        w_tf = jnp.swapaxes(w_in[l], 0, 1)

        qn, kvn, kr_raw, u = _latent(x_cur, mod, w_tf,
                                     g_q_a[l].reshape(1, -1), g_kv_a[l].reshape(1, -1))
        q_nope, q_rope, k_nope, v_t, k_rope = _qkv_up(qn, kvn, kr_raw, pos_col, inv_row,
                                                      w_q_b[l], w_kv_b[l])
        attn, (w_cg,) = _attention(q_nope, q_rope, k_nope, k_rope, v_t, [(w_tf, o_cb)])
        cm, (w_oa, w_ob, w_om) = _conv_branch(
            u, w_cg, w_conv[l], [(w_o_a[l], 0), (w_o_b[l], 0), (w_o[l], 0)])
        gates, (w_fi,) = _gates(u, w_cg, 3, [(w_ffn_in[l], 0)])
        merged, (w_fo,) = _merge(attn, cm, w_oa, w_ob, gates, [(w_ffn_out[l], 0)])
        x_cur = _mix_out(merged, w_om, x_cur, mod,
                         ln1_g[l].reshape(1, -1), ln1_b[l].reshape(1, -1))

        x_cur = _ffn(x_cur, mod, w_fi, w_fo,
                     ln2_g[l].reshape(1, -1), ln2_b[l].reshape(1, -1))

    return x_cur.reshape(bsz, seq, d)
```

```python
import functools
import math

import jax
import jax.numpy as jnp
from jax import lax
from jax.experimental import pallas as pl
from jax.experimental.pallas import tpu as pltpu

F32 = jnp.float32
BF16 = jnp.bfloat16

D_MODEL = 2048
SEQ = 2048
CHUNK = 64
N_HEADS = 16
QK_NOPE = 128
QK_ROPE = 64
V_HEAD = 128
Q_LORA = 512
KV_LORA = 512
ROPE_THETA = 10000.0
ATTN_SCALE = (QK_NOPE + QK_ROPE) ** -0.5
CONV_K = 3
D_FF = 5632
DEPTH = 1
DEEPNORM_ALPHA = (2.0 * DEPTH) ** 0.25
LN_EPS = 1e-5
RMS_EPS = 1e-6

LANES = 128
SUBLANES = 8
HALF = QK_ROPE // 2
HEAD_PAD = N_HEADS * LANES
ATTN_TQ = 256
ATTN_LOOKAHEAD = 4
ATTN_HEADS_PER_STEP = 2
CAST_PIECE_ROWS = 32
UP_HEAD_GROUP = 4
ROW_CHUNK = 256
COL_CHUNK = 512
MXU_WIDTH = 256
Q_PRESCALE = ATTN_SCALE * math.log2(math.e)

VMEM_LIMIT = 58 * 1024 * 1024
FFN_VMEM_LIMIT = 62 * 1024 * 1024


def _params(n_axes, vmem=VMEM_LIMIT):
    return pltpu.CompilerParams(
        dimension_semantics=("arbitrary",) * n_axes, vmem_limit_bytes=vmem)


def _resident(shape, index_map):
    return pl.BlockSpec(shape, index_map, pipeline_mode=pl.Buffered(1))


def _slab_specs(w, n_steps, step_of, row0=0):
    n_rows = w.shape[0] - row0
    cols = w.shape[1]
    bf16_rows = 2 * SUBLANES
    if n_rows % (n_steps * bf16_rows) == 0:
        rb = n_rows // n_steps
        in_spec = pl.BlockSpec(
            (pl.Element(rb), pl.Element(cols)),
            lambda *g: (pl.multiple_of(row0 + step_of(*g) * rb, SUBLANES), 0))
        out_spec = pl.BlockSpec((rb, cols), lambda *g: (step_of(*g), 0))
    else:
        assert row0 == 0 and n_rows % (n_steps // 2 * bf16_rows) == 0
        rb = n_rows // (n_steps // 2)
        in_spec = out_spec = pl.BlockSpec(
            (rb, cols // 2), lambda *g: (step_of(*g) // 2, step_of(*g) % 2))
    return in_spec, out_spec, jax.ShapeDtypeStruct((n_rows, cols), BF16)


def _split_refs(refs, n_in, n_cast):
    return refs[:n_in], refs[n_in:n_in + n_cast], refs[n_in + n_cast], refs[n_in + n_cast + 1:]


def _cast_pieces(cast_src, cast_dst):
    pieces = []
    for src, dst in zip(cast_src, cast_dst, strict=True):
        rows = src.shape[0]
        n_piece = next(n for n in range(max(1, rows // CAST_PIECE_ROWS), 0, -1)
                       if rows % n == 0 and (rows // n) % (2 * SUBLANES) == 0)
        pieces += [(src, dst, slice(p * rows // n_piece, (p + 1) * rows // n_piece))
                   for p in range(n_piece)]
    return pieces


def _convert_pieces(pieces, k, n):
    for src, dst, rows in pieces[k::n]:
        dst[rows, :] = src[rows, :].astype(BF16)


def _col_chunks(n, width):
    chunks = [(c0, width) for c0 in range(0, n - width, width)]
    c0, size = n - width, width
    while size > MXU_WIDTH:
        size //= 2
        chunks.append((c0, size))
        c0 += size
    chunks.append((c0, size))
    return chunks


def _dot(a, b):
    return jnp.dot(a, b, preferred_element_type=F32)


def _dot_nt(a, b_t):
    return lax.dot_general(a, b_t, (((1,), (1,)), ((), ())), preferred_element_type=F32)


def _modulate(x, scale, shift):
    return (x * (1.0 + scale) + shift).astype(BF16)


MOD_SHIFT1, MOD_SCALE1, MOD_GATE1, MOD_SHIFT2, MOD_SCALE2, MOD_GATE2 = range(6)


def _mod_spec(mod, which):
    return pl.BlockSpec((mod.shape[0], D_MODEL), lambda *g: (0, which))


def _batch_row(ref, tile, tiles_per_seq):
    return ref[pl.ds(tile // tiles_per_seq, 1), :]


def _layer_norm(y, g, b):
    mu = jnp.mean(y, axis=-1, keepdims=True)
    yc = y - mu
    var = jnp.mean(yc * yc, axis=-1, keepdims=True)
    return yc * lax.rsqrt(var + LN_EPS) * g + b


def _rms_norm(y, g):
    return y * lax.rsqrt(jnp.mean(y * y, axis=-1, keepdims=True) + RMS_EPS) * g


def _ada_kernel(c_ref, w_ref, b_ref, o_ref):
    c = c_ref[...]
    c_act = (c * jax.nn.sigmoid(c)).astype(BF16)
    o_ref[...] = _dot(c_act, w_ref[...].astype(BF16)) + b_ref[...]


def _ada(c_pad, w_ada, b_ada, tn=1024):
    rows, d = c_pad.shape
    n = w_ada.shape[1]
    return pl.pallas_call(
        _ada_kernel,
        grid=(n // tn,),
        in_specs=[pl.BlockSpec((rows, d), lambda j: (0, 0)),
                  pl.BlockSpec((d, tn), lambda j: (0, j)),
                  pl.BlockSpec((1, tn), lambda j: (0, j))],
        out_specs=pl.BlockSpec((rows, tn), lambda j: (0, j)),
        out_shape=jax.ShapeDtypeStruct((rows, n), F32),
        compiler_params=_params(1),
        name="ada_mod",
    )(c_pad, w_ada, b_ada)


def _latent_kernel(x_ref, sh_ref, sc_ref, w_ref, gq_ref, gkv_ref,
                   qn_ref, kvn_ref, kr_ref, u_ref, wbf_ref, wkr_ref, *, tiles_per_seq):
    n_lat = Q_LORA + KV_LORA

    @pl.when(pl.program_id(0) == 0)
    def _():
        wbf_ref[...] = w_ref[:n_lat, :].astype(BF16)
        wkr_ref[...] = jnp.zeros_like(wkr_ref)
        wkr_ref[0:QK_ROPE, :] = w_ref[n_lat:n_lat + QK_ROPE, :].astype(BF16)
        wkr_ref[LANES:LANES + HALF, :] = (-w_ref[n_lat + HALF:n_lat + QK_ROPE, :]).astype(BF16)
        wkr_ref[LANES + HALF:LANES + QK_ROPE, :] = w_ref[n_lat:n_lat + HALF, :].astype(BF16)

    tile = pl.program_id(0)
    u = _modulate(x_ref[...], _batch_row(sc_ref, tile, tiles_per_seq),
                  _batch_row(sh_ref, tile, tiles_per_seq))
    u_ref[...] = u
    p = _dot_nt(u, wbf_ref[...])
    qn_ref[...] = _rms_norm(p[:, :Q_LORA], gq_ref[...]).astype(BF16)
    kvn_ref[...] = _rms_norm(p[:, Q_LORA:], gkv_ref[...]).astype(BF16)
    kr_ref[...] = _dot_nt(u, wkr_ref[...])


def _latent(x2, mod, w_t, g_q, g_kv, bm=1024):
    m, d = x2.shape
    per_seq = SEQ // bm
    n_lat = Q_LORA + KV_LORA
    return pl.pallas_call(
        functools.partial(_latent_kernel, tiles_per_seq=per_seq),
        grid=(m // bm,),
        in_specs=[pl.BlockSpec((bm, d), lambda i: (i, 0)),
                  _mod_spec(mod, MOD_SHIFT1), _mod_spec(mod, MOD_SCALE1),
                  _resident((n_lat + QK_ROPE, d), lambda i: (0, 0)),
                  pl.BlockSpec((1, Q_LORA), lambda i: (0, 0)),
                  pl.BlockSpec((1, KV_LORA), lambda i: (0, 0))],
        out_specs=[pl.BlockSpec((bm, Q_LORA), lambda i: (i, 0)),
                   pl.BlockSpec((bm, KV_LORA), lambda i: (i, 0)),
                   pl.BlockSpec((bm, 2 * LANES), lambda i: (i, 0)),
                   pl.BlockSpec((bm, d), lambda i: (i, 0))],
        out_shape=[jax.ShapeDtypeStruct((m, Q_LORA), BF16),
                   jax.ShapeDtypeStruct((m, KV_LORA), BF16),
                   jax.ShapeDtypeStruct((m, 2 * LANES), F32),
                   jax.ShapeDtypeStruct((m, d), BF16)],
        scratch_shapes=[pltpu.VMEM((n_lat, d), BF16), pltpu.VMEM((2 * LANES, d), BF16)],
        compiler_params=_params(1),
        name="latent_proj",
    )(x2, mod, mod, w_t, g_q, g_kv)


def _qkv_up_kernel(qn_ref, kvn_ref, krr_ref, pos_ref, inv_ref, wq_raw, wkv_raw,
                   qnope_ref, qrope_ref, knope_ref, vt_ref, kr_ref, wq_ref, wk_ref, wvt_ref):
    @pl.when(pl.program_id(0) == 0)
    def _():
        for h in range(N_HEADS):
            q0 = h * (QK_NOPE + QK_ROPE)
            blk = slice(h * LANES, (h + 1) * LANES)
            wq_ref[:, blk] = wq_raw[:, q0:q0 + QK_NOPE].astype(BF16)
            rope = wq_raw[:, q0 + QK_NOPE:q0 + QK_NOPE + QK_ROPE]
            packed = jnp.concatenate([rope, -rope[:, HALF:], rope[:, :HALF]], axis=1)
            wq_ref[:, HEAD_PAD + h * LANES:HEAD_PAD + (h + 1) * LANES] = packed.astype(BF16)
            k0 = h * (QK_NOPE + V_HEAD)
            wk_ref[:, blk] = wkv_raw[:, k0:k0 + QK_NOPE].astype(BF16)
            wvt_ref[blk, :] = wkv_raw[:, k0 + QK_NOPE:k0 + QK_NOPE + V_HEAD].T.astype(BF16)

    ang = pos_ref[...].astype(F32) * inv_ref[...]
    cos = jnp.cos(ang)
    sin = jnp.sin(ang)
    qn = qn_ref[...]
    kvn = kvn_ref[...]
    grp = UP_HEAD_GROUP * LANES
    lane = lax.broadcasted_iota(jnp.int32, cos.shape, 1)
    cs = jnp.where(lane < QK_ROPE, cos, sin) * Q_PRESCALE
    for g in range(N_HEADS // UP_HEAD_GROUP):
        lo, hi = g * grp, (g + 1) * grp
        qnope_ref[:, lo:hi] = (_dot(qn, wq_ref[:, lo:hi]) * Q_PRESCALE).astype(BF16)
        p = _dot(qn, wq_ref[:, HEAD_PAD + lo:HEAD_PAD + hi])
        for h in range(UP_HEAD_GROUP):
            t = p[:, h * LANES:(h + 1) * LANES] * cs
            qrope_ref[:, lo + h * LANES:lo + (h + 1) * LANES] = (
                t + pltpu.roll(t, QK_ROPE, 1)).astype(BF16)
        knope_ref[:, lo:hi] = _dot(kvn, wk_ref[:, lo:hi]).astype(BF16)
        vt_ref[lo:hi, :] = _dot_nt(wvt_ref[lo:hi, :], kvn).astype(BF16)
    krr = krr_ref[...]
    kr_ref[...] = (krr[:, :LANES] * cos + krr[:, LANES:] * sin).astype(BF16)


def _qkv_up(qn, kvn, kr_raw, pos_col, inv_row, w_q_b, w_kv_b, bm=512):
    m = qn.shape[0]
    row = lambda n: pl.BlockSpec((bm, n), lambda i: (i, 0))
    const = lambda a: _resident(a.shape, lambda i: (0, 0))
    return pl.pallas_call(
        _qkv_up_kernel,
        grid=(m // bm,),
        in_specs=[row(Q_LORA), row(KV_LORA), row(2 * LANES), row(1),
                  const(inv_row), const(w_q_b), const(w_kv_b)],
        out_specs=[row(HEAD_PAD), row(HEAD_PAD), row(HEAD_PAD),
                   pl.BlockSpec((HEAD_PAD, bm), lambda i: (0, i)), row(LANES)],
        out_shape=[jax.ShapeDtypeStruct((m, HEAD_PAD), BF16)] * 3
        + [jax.ShapeDtypeStruct((HEAD_PAD, m), BF16), jax.ShapeDtypeStruct((m, LANES), BF16)],
        scratch_shapes=[pltpu.VMEM((Q_LORA, 2 * HEAD_PAD), BF16),
                        pltpu.VMEM((KV_LORA, HEAD_PAD), BF16),
                        pltpu.VMEM((HEAD_PAD, KV_LORA), BF16)],
        compiler_params=_params(1),
        name="qkv_up",
    )(qn, kvn, kr_raw, pos_col, inv_row, w_q_b, w_kv_b)


def _conv_kernel(u_ref, wb_ref, wc_ref, wx_ref, wconv_ref, o_ref, halo_ref, *, tiles_per_seq):
    i = pl.program_id(0)
    j = pl.program_id(1)
    u = u_ref[...]
    bm, tn = o_ref.shape

    @pl.when(i % tiles_per_seq == 0)
    def _():
        halo_ref[j] = jnp.zeros(halo_ref.shape[1:], F32)

    for c0, width in _col_chunks(tn, MXU_WIDTH):
        cols = slice(c0, c0 + width)
        rows = lax.broadcasted_iota(jnp.int32, (bm, width), 0)
        cb = _dot_nt(u, wb_ref[cols, :])
        z = _dot_nt(u, wc_ref[cols, :]) * _dot_nt(u, wx_ref[cols, :])
        prev = halo_ref[j, :, cols]
        z1 = jnp.where(rows == 0, prev[SUBLANES - 1:SUBLANES, :], pltpu.roll(z, 1, 0))
        z2 = jnp.where(rows == 0, prev[SUBLANES - 2:SUBLANES - 1, :],
                       jnp.where(rows == 1, prev[SUBLANES - 1:SUBLANES, :], pltpu.roll(z, 2, 0)))
        halo_ref[j, :, cols] = z[bm - SUBLANES:, :]
        w = wconv_ref[:, cols]
        o_ref[:, cols] = (cb * (w[0:1, :] * z2 + w[1:2, :] * z1 + w[2:3, :] * z)).astype(BF16)


def _conv_branch(u, w_rows, w_conv, bm=1024, tn=1024):
    m, d = u.shape
    n = w_conv.shape[1]
    per_seq = SEQ // bm
    w_spec = lambda piece: pl.BlockSpec((tn, d), lambda i, j: (piece * (n // tn) + j, 0))
    return pl.pallas_call(
        functools.partial(_conv_kernel, tiles_per_seq=per_seq),
        grid=(m // bm, n // tn),
        in_specs=[pl.BlockSpec((bm, d), lambda i, j: (i, 0)),
                  w_spec(0), w_spec(1), w_spec(2),
                  pl.BlockSpec((CONV_K, tn), lambda i, j: (0, j))],
        out_specs=pl.BlockSpec((bm, tn), lambda i, j: (i, j)),
        out_shape=jax.ShapeDtypeStruct((m, n), BF16),
        scratch_shapes=[pltpu.VMEM((n // tn, SUBLANES, tn), F32)],
        compiler_params=_params(2),
        name="conv_branch",
    )(u, w_rows, w_rows, w_rows, w_conv)


def _gate_kernel(*refs, n_cast):
    (u_ref, wa_ref, wb_ref), cast_src, o_ref, cast_dst = _split_refs(refs, 3, n_cast)
    pieces = _cast_pieces(cast_src, cast_dst)
    u = u_ref[...]
    n = wa_ref.shape[0]
    passes = ([(wa_ref, 0, c0, 2 * COL_CHUNK) for c0 in range(0, n, 2 * COL_CHUNK)]
              + [(wb_ref, n, c0, width) for c0, width in _col_chunks(n, 2 * COL_CHUNK)])
    for k, (w_ref, out0, c0, width) in enumerate(passes):
        o_ref[:, out0 + c0:out0 + c0 + width] = jax.nn.sigmoid(
            _dot_nt(u, w_ref[c0:c0 + width, :])).astype(BF16)
        _convert_pieces(pieces, k, len(passes))


def _gates(u, w_rows, first_block, cast_jobs, bm=1024):
    m, d = u.shape
    specs = [_slab_specs(w, m // bm, lambda i: i, row0) for w, row0 in cast_jobs]
    outs = pl.pallas_call(
        functools.partial(_gate_kernel, n_cast=len(cast_jobs)),
        grid=(m // bm,),
        in_specs=[pl.BlockSpec((bm, d), lambda i: (i, 0)),
                  _resident((d, d), lambda i: (first_block, 0)),
                  _resident((d, d), lambda i: (first_block + 1, 0))] + [s[0] for s in specs],
        out_specs=[pl.BlockSpec((bm, 2 * d), lambda i: (i, 0))] + [s[1] for s in specs],
        out_shape=[jax.ShapeDtypeStruct((m, 2 * d), BF16)] + [s[2] for s in specs],
        compiler_params=_params(1),
        name="merge_gates",
    )(u, w_rows, w_rows, *[w for w, _ in cast_jobs])
    return outs[0], outs[1:]


def _attn_kernel(*refs, n_cast):
    (qn_ref, qr_ref, kn_ref, kr_ref, vt_ref), cast_src, o_ref, cast_dst = _split_refs(refs, 5, n_cast)
    pieces = _cast_pieces(cast_src, cast_dst)
    kr = kr_ref[...]
    ks = [jnp.concatenate([kn_ref[:, a * LANES:(a + 1) * LANES], kr], axis=1)
          for a in range(ATTN_HEADS_PER_STEP)]
    key = lax.broadcasted_iota(jnp.int32, (ATTN_TQ, ATTN_TQ), 0)
    qry = lax.broadcasted_iota(jnp.int32, (ATTN_TQ, ATTN_TQ), 1)
    allowed = (key // CHUNK) <= (qry // CHUNK)
    n_blk = SEQ // ATTN_TQ

    def scores(task):
        a, i = task
        q0 = i * ATTN_TQ
        L = q0 + ATTN_TQ
        lanes = slice(a * LANES, (a + 1) * LANES)
        q = jnp.concatenate([qn_ref[q0:L, lanes], qr_ref[q0:L, lanes]], axis=1)
        return _dot_nt(ks[a][:L], q)

    tasks = [(a, i) for i in range(n_blk - 1, -1, -1) for a in range(ATTN_HEADS_PER_STEP)]
    pending = [scores(t) for t in tasks[:ATTN_LOOKAHEAD]]
    for pos, (a, i) in enumerate(tasks):
        q0 = i * ATTN_TQ
        L = q0 + ATTN_TQ
        s_t = pending.pop(0)
        if pos + ATTN_LOOKAHEAD < len(tasks):
            pending.append(scores(tasks[pos + ATTN_LOOKAHEAD]))
        heads = slice(a * V_HEAD, (a + 1) * V_HEAD)
        s_diag = jnp.where(allowed, s_t[q0:, :], -1e30)
        m_col = jnp.max(s_diag, axis=0, keepdims=True)
        if i > 0:
            s_full = s_t[:q0, :]
            m_col = jnp.maximum(m_col, jnp.max(s_full, axis=0, keepdims=True))
        e_diag = jnp.exp2(s_diag - m_col)
        denom = jnp.sum(e_diag, axis=0, keepdims=True)
        o_t = _dot(vt_ref[heads, q0:L], e_diag.astype(BF16))
        if i > 0:
            e_full = jnp.exp2(s_full - m_col)
            denom = denom + jnp.sum(e_full, axis=0, keepdims=True)
            o_t = o_t + _dot(vt_ref[heads, :q0], e_full.astype(BF16))
        o_ref[q0:L, a * LANES:(a + 1) * LANES] = (o_t / denom).T.astype(BF16)
        _convert_pieces(pieces, pos, len(tasks))


def _attention(q_nope, q_rope, k_nope, k_rope, v_t, cast_jobs):
    m = q_nope.shape[0]
    hps = ATTN_HEADS_PER_STEP
    grid = (m // SEQ, N_HEADS // hps)
    head = pl.BlockSpec((SEQ, hps * LANES), lambda b, h: (b, h))
    specs = [_slab_specs(w, grid[0] * grid[1], lambda b, h: b * grid[1] + h, row0)
             for w, row0 in cast_jobs]
    outs = pl.pallas_call(
        functools.partial(_attn_kernel, n_cast=len(cast_jobs)),
        grid=grid,
        in_specs=[head, head, head, pl.BlockSpec((SEQ, LANES), lambda b, h: (b, 0)),
                  pl.BlockSpec((hps * V_HEAD, SEQ), lambda b, h: (h, b))] + [s[0] for s in specs],
        out_specs=[head] + [s[1] for s in specs],
        out_shape=[jax.ShapeDtypeStruct((m, HEAD_PAD), BF16)] + [s[2] for s in specs],
        compiler_params=_params(2),
        name="mla_attention",
    )(q_nope, q_rope, k_nope, k_rope, v_t, *[w for w, _ in cast_jobs])
    return outs[0], outs[1:]


def _merge_kernel(*refs, n_cast):
    (a_ref, c_ref, wa_ref, wb_ref, g_ref), cast_src, o_ref, cast_dst = _split_refs(refs, 5, n_cast)
    pieces = _cast_pieces(cast_src, cast_dst)
    a = a_ref[...]
    cm = c_ref[...]
    n = o_ref.shape[1]
    chunks = _col_chunks(n, MXU_WIDTH)
    for k, (c0, width) in enumerate(chunks):
        cols = slice(c0, c0 + width)
        y_a = _dot(a, wa_ref[:, cols])
        y_b = _dot(cm, wb_ref[:, cols])
        o_ref[:, cols] = (g_ref[:, cols].astype(F32) * y_a
                          + g_ref[:, n + c0:n + c0 + width].astype(F32) * y_b).astype(BF16)
        _convert_pieces(pieces, k, len(chunks))


def _merge(attn, cm, w_oa, w_ob, gates, cast_jobs, bm=512):
    m, d = attn.shape
    n = w_oa.shape[1]
    lhs = pl.BlockSpec((bm, d), lambda i: (i, 0))
    specs = [_slab_specs(w, m // bm, lambda i: i, row0) for w, row0 in cast_jobs]
    outs = pl.pallas_call(
        functools.partial(_merge_kernel, n_cast=len(cast_jobs)),
        grid=(m // bm,),
        in_specs=[lhs, lhs, _resident((d, n), lambda i: (0, 0)), _resident((d, n), lambda i: (0, 0)),
                  pl.BlockSpec((bm, 2 * n), lambda i: (i, 0))] + [s[0] for s in specs],
        out_specs=[pl.BlockSpec((bm, n), lambda i: (i, 0))] + [s[1] for s in specs],
        out_shape=[jax.ShapeDtypeStruct((m, n), BF16)] + [s[2] for s in specs],
        compiler_params=_params(1),
        name="branch_merge",
    )(attn, cm, w_oa, w_ob, gates, *[w for w, _ in cast_jobs])
    return outs[0], outs[1:]


def _mix_out_kernel(mg_ref, w_ref, x_ref, g1_ref, lg_ref, lb_ref, o_ref, *, tiles_per_seq):
    gate1 = _batch_row(g1_ref, pl.program_id(0), tiles_per_seq)
    n_chunks = mg_ref.shape[0] // ROW_CHUNK
    mix = _dot(mg_ref[0:ROW_CHUNK, :], w_ref[...])
    for r in range(n_chunks):
        rows = slice(r * ROW_CHUNK, (r + 1) * ROW_CHUNK)
        cur = mix
        if r + 1 < n_chunks:
            mix = _dot(mg_ref[(r + 1) * ROW_CHUNK:(r + 2) * ROW_CHUNK, :], w_ref[...])
        y = DEEPNORM_ALPHA * x_ref[rows, :] + gate1 * cur
        o_ref[rows, :] = _layer_norm(y, lg_ref[...], lb_ref[...])


def _mix_out(merged, w_o, x2, mod, ln_g, ln_b, bm=512):
    m, d = x2.shape
    per_seq = SEQ // bm
    row = pl.BlockSpec((bm, d), lambda i: (i, 0))
    vec = pl.BlockSpec((1, d), lambda i: (0, 0))
    return pl.pallas_call(
        functools.partial(_mix_out_kernel, tiles_per_seq=per_seq),
        grid=(m // bm,),
        in_specs=[row, _resident((d, d), lambda i: (0, 0)), row,
                  _mod_spec(mod, MOD_GATE1), vec, vec],
        out_specs=row,
        out_shape=jax.ShapeDtypeStruct((m, d), F32),
        compiler_params=_params(1),
        name="mix_out_ln",
    )(merged, w_o, x2, mod, ln_g, ln_b)


def _ffn_kernel(x_ref, sc_ref, sh_ref, g2_ref, wg_ref, wu_ref, wo_ref, lg_ref, lb_ref, o_ref,
                u_ref, *, tiles_per_seq):
    acc_ref = o_ref
    tile = pl.program_id(0)
    j = pl.program_id(1)

    @pl.when(j == 0)
    def _():
        u_ref[...] = _modulate(x_ref[...], _batch_row(sc_ref, tile, tiles_per_seq),
                               _batch_row(sh_ref, tile, tiles_per_seq))
        acc_ref[...] = jnp.zeros_like(acc_ref)

    u = u_ref[...]
    tf = wg_ref.shape[1]
    half = tf // 2
    hs = []
    for c0 in (0, half):
        hg = _dot(u, wg_ref[:, c0:c0 + half])
        hs.append((hg * jax.nn.sigmoid(hg) * _dot(u, wu_ref[:, c0:c0 + half])).astype(BF16))
    for c0 in range(0, acc_ref.shape[1], COL_CHUNK):
        cols = slice(c0, c0 + COL_CHUNK)
        acc_ref[:, cols] += _dot(hs[0], wo_ref[:half, cols]) + _dot(hs[1], wo_ref[half:, cols])

    @pl.when(j == pl.num_programs(1) - 1)
    def _():
        y = (DEEPNORM_ALPHA * x_ref[...]
             + _batch_row(g2_ref, tile, tiles_per_seq) * acc_ref[...])
        o_ref[...] = _layer_norm(y, lg_ref[...], lb_ref[...])


def _ffn(x1, mod, w_in, w_out, ln_g, ln_b, bm=1024, tf=512):
    m, d = x1.shape
    nf = D_FF // tf
    per_seq = SEQ // bm
    row = pl.BlockSpec((bm, d), lambda i, j: (i, 0))
    vec = pl.BlockSpec((1, d), lambda i, j: (0, 0))
    return pl.pallas_call(
        functools.partial(_ffn_kernel, tiles_per_seq=per_seq),
        grid=(m // bm, nf),
        in_specs=[row, _mod_spec(mod, MOD_SCALE2), _mod_spec(mod, MOD_SHIFT2),
                  _mod_spec(mod, MOD_GATE2),
                  pl.BlockSpec((d, tf), lambda i, j: (0, j)),
                  pl.BlockSpec((d, tf), lambda i, j: (0, j + nf)),
                  pl.BlockSpec((tf, d), lambda i, j: (j, 0)), vec, vec],
        out_specs=row,
        out_shape=jax.ShapeDtypeStruct((m, d), F32),
        scratch_shapes=[pltpu.VMEM((bm, d), BF16)],
        compiler_params=_params(2, vmem=FFN_VMEM_LIMIT),
        name="swiglu_ffn_ln",
    )(x1, mod, mod, mod, w_in, w_in, w_out, ln_g, ln_b)


def kernel(x, c, positions, w_ada, b_ada, w_in, g_q_a, w_q_b, g_kv_a, w_kv_b, w_o_a,
           w_conv, w_o_b, w_o, ln1_g, ln1_b, w_ffn_in, w_ffn_out, ln2_g, ln2_b):
    bsz, seq, d = x.shape
    assert (seq, d) == (SEQ, D_MODEL) and w_ada.shape[0] == DEPTH
    m = bsz * seq
    x_cur = x.reshape(m, d)

    inv_freq = 1.0 / (ROPE_THETA ** (jnp.arange(0, QK_ROPE, 2, dtype=F32) / QK_ROPE))
    inv_row = jnp.tile(inv_freq, LANES // HALF).reshape(1, LANES)
    pos_col = positions.reshape(m, 1)
    c_pad = jnp.pad(c, ((0, SUBLANES - bsz), (0, 0)))

    for l in range(DEPTH):
        mod = _ada(c_pad, w_ada[l], b_ada[l].reshape(1, -1))

        o_cb = Q_LORA + KV_LORA + QK_ROPE
        w_tf = jnp.swapaxes(w_in[l], 0, 1)

        qn, kvn, kr_raw, u = _latent(x_cur, mod, w_tf,
                                     g_q_a[l].reshape(1, -1), g_kv_a[l].reshape(1, -1))
        q_nope, q_rope, k_nope, v_t, k_rope = _qkv_up(qn, kvn, kr_raw, pos_col, inv_row,
                                                      w_q_b[l], w_kv_b[l])
        attn, (w_cg, w_oa, w_ob, w_om, w_fi) = _attention(
            q_nope, q_rope, k_nope, k_rope, v_t,
            [(w_tf, o_cb), (w_o_a[l], 0), (w_o_b[l], 0), (w_o[l], 0), (w_ffn_in[l], 0)])
        cm = _conv_branch(u, w_cg, w_conv[l])
        gates, _ = _gates(u, w_cg, 3, [])
        merged, (w_fo,) = _merge(attn, cm, w_oa, w_ob, gates, [(w_ffn_out[l], 0)])
        x_cur = _mix_out(merged, w_om, x_cur, mod,
                         ln1_g[l].reshape(1, -1), ln1_b[l].reshape(1, -1))

        x_cur = _ffn(x_cur, mod, w_fi, w_fo,
                     ln2_g[l].reshape(1, -1), ln2_b[l].reshape(1, -1))

    return x_cur.reshape(bsz, seq, d)
```

```python
import functools
import math

import jax
import jax.numpy as jnp
from jax import lax
from jax.experimental import pallas as pl
from jax.experimental.pallas import tpu as pltpu

F32 = jnp.float32
BF16 = jnp.bfloat16

D_MODEL = 2048
SEQ = 2048
CHUNK = 64
N_HEADS = 16
QK_NOPE = 128
QK_ROPE = 64
V_HEAD = 128
Q_LORA = 512
KV_LORA = 512
ROPE_THETA = 10000.0
ATTN_SCALE = (QK_NOPE + QK_ROPE) ** -0.5
CONV_K = 3
D_FF = 5632
DEPTH = 1
DEEPNORM_ALPHA = (2.0 * DEPTH) ** 0.25
LN_EPS = 1e-5
RMS_EPS = 1e-6

LANES = 128
SUBLANES = 8
HALF = QK_ROPE // 2
HEAD_PAD = N_HEADS * LANES
ATTN_TQ = 256
ATTN_LOOKAHEAD = 4
ATTN_HEADS_PER_STEP = 2
CAST_PIECE_ROWS = 32
UP_HEAD_GROUP = 4
ROW_CHUNK = 256
COL_CHUNK = 512
MXU_WIDTH = 256
Q_PRESCALE = ATTN_SCALE * math.log2(math.e)

VMEM_LIMIT = 58 * 1024 * 1024
FFN_VMEM_LIMIT = 62 * 1024 * 1024


def _params(n_axes, vmem=VMEM_LIMIT):
    return pltpu.CompilerParams(
        dimension_semantics=("arbitrary",) * n_axes, vmem_limit_bytes=vmem)


def _resident(shape, index_map):
    return pl.BlockSpec(shape, index_map, pipeline_mode=pl.Buffered(1))


def _slab_specs(w, n_steps, step_of, row0=0):
    n_rows = w.shape[0] - row0
    cols = w.shape[1]
    bf16_rows = 2 * SUBLANES
    if n_rows % (n_steps * bf16_rows) == 0:
        rb = n_rows // n_steps
        in_spec = pl.BlockSpec(
            (pl.Element(rb), pl.Element(cols)),
            lambda *g: (pl.multiple_of(row0 + step_of(*g) * rb, SUBLANES), 0))
        out_spec = pl.BlockSpec((rb, cols), lambda *g: (step_of(*g), 0))
    else:
        assert row0 == 0 and n_rows % (n_steps // 2 * bf16_rows) == 0
        rb = n_rows // (n_steps // 2)
        in_spec = out_spec = pl.BlockSpec(
            (rb, cols // 2), lambda *g: (step_of(*g) // 2, step_of(*g) % 2))
    return in_spec, out_spec, jax.ShapeDtypeStruct((n_rows, cols), BF16)


def _split_refs(refs, n_in, n_cast):
    return refs[:n_in], refs[n_in:n_in + n_cast], refs[n_in + n_cast], refs[n_in + n_cast + 1:]


def _cast_pieces(cast_src, cast_dst):
    pieces = []
    for src, dst in zip(cast_src, cast_dst, strict=True):
        rows = src.shape[0]
        n_piece = next(n for n in range(max(1, rows // CAST_PIECE_ROWS), 0, -1)
                       if rows % n == 0 and (rows // n) % (2 * SUBLANES) == 0)
        pieces += [(src, dst, slice(p * rows // n_piece, (p + 1) * rows // n_piece))
                   for p in range(n_piece)]
    return pieces


def _convert_pieces(pieces, k, n):
    for src, dst, rows in pieces[k::n]:
        dst[rows, :] = src[rows, :].astype(BF16)


def _col_chunks(n, width):
    chunks = [(c0, width) for c0 in range(0, n - width, width)]
    c0, size = n - width, width
    while size > MXU_WIDTH:
        size //= 2
        chunks.append((c0, size))
        c0 += size
    chunks.append((c0, size))
    return chunks


def _dot(a, b):
    return jnp.dot(a, b, preferred_element_type=F32)


def _dot_nt(a, b_t):
    return lax.dot_general(a, b_t, (((1,), (1,)), ((), ())), preferred_element_type=F32)


def _modulate(x, scale, shift):
    return (x * (1.0 + scale) + shift).astype(BF16)


MOD_SHIFT1, MOD_SCALE1, MOD_GATE1, MOD_SHIFT2, MOD_SCALE2, MOD_GATE2 = range(6)


def _mod_spec(mod, which):
    return pl.BlockSpec((mod.shape[0], D_MODEL), lambda *g: (0, which))


def _batch_row(ref, tile, tiles_per_seq):
    return ref[pl.ds(tile // tiles_per_seq, 1), :]


def _layer_norm(y, g, b):
    mu = jnp.mean(y, axis=-1, keepdims=True)
    yc = y - mu
    var = jnp.mean(yc * yc, axis=-1, keepdims=True)
    return yc * lax.rsqrt(var + LN_EPS) * g + b


def _rms_norm(y, g):
    return y * lax.rsqrt(jnp.mean(y * y, axis=-1, keepdims=True) + RMS_EPS) * g


def _ada_kernel(c_ref, w_ref, b_ref, o_ref):
    c = c_ref[...]
    c_act = (c * jax.nn.sigmoid(c)).astype(BF16)
    o_ref[...] = _dot(c_act, w_ref[...].astype(BF16)) + b_ref[...]


def _ada(c_pad, w_ada, b_ada, tn=1024):
    rows, d = c_pad.shape
    n = w_ada.shape[1]
    return pl.pallas_call(
        _ada_kernel,
        grid=(n // tn,),
        in_specs=[pl.BlockSpec((rows, d), lambda j: (0, 0)),
                  pl.BlockSpec((d, tn), lambda j: (0, j)),
                  pl.BlockSpec((1, tn), lambda j: (0, j))],
        out_specs=pl.BlockSpec((rows, tn), lambda j: (0, j)),
        out_shape=jax.ShapeDtypeStruct((rows, n), F32),
        compiler_params=_params(1),
        name="ada_mod",
    )(c_pad, w_ada, b_ada)


def _latent_kernel(x_ref, sh_ref, sc_ref, w_ref, gq_ref, gkv_ref,
                   qn_ref, kvn_ref, kr_ref, u_ref, wbf_ref, wkr_ref, *, tiles_per_seq):
    n_lat = Q_LORA + KV_LORA

    @pl.when(pl.program_id(0) == 0)
    def _():
        wbf_ref[...] = w_ref[:n_lat, :].astype(BF16)
        wkr_ref[...] = jnp.zeros_like(wkr_ref)
        wkr_ref[0:QK_ROPE, :] = w_ref[n_lat:n_lat + QK_ROPE, :].astype(BF16)
        wkr_ref[LANES:LANES + HALF, :] = (-w_ref[n_lat + HALF:n_lat + QK_ROPE, :]).astype(BF16)
        wkr_ref[LANES + HALF:LANES + QK_ROPE, :] = w_ref[n_lat:n_lat + HALF, :].astype(BF16)

    tile = pl.program_id(0)
    u = _modulate(x_ref[...], _batch_row(sc_ref, tile, tiles_per_seq),
                  _batch_row(sh_ref, tile, tiles_per_seq))
    u_ref[...] = u
    p = _dot_nt(u, wbf_ref[...])
    qn_ref[...] = _rms_norm(p[:, :Q_LORA], gq_ref[...]).astype(BF16)
    kvn_ref[...] = _rms_norm(p[:, Q_LORA:], gkv_ref[...]).astype(BF16)
    kr_ref[...] = _dot_nt(u, wkr_ref[...])


def _latent(x2, mod, w_t, g_q, g_kv, bm=1024):
    m, d = x2.shape
    per_seq = SEQ // bm
    n_lat = Q_LORA + KV_LORA
    return pl.pallas_call(
        functools.partial(_latent_kernel, tiles_per_seq=per_seq),
        grid=(m // bm,),
        in_specs=[pl.BlockSpec((bm, d), lambda i: (i, 0)),
                  _mod_spec(mod, MOD_SHIFT1), _mod_spec(mod, MOD_SCALE1),
                  _resident((n_lat + QK_ROPE, d), lambda i: (0, 0)),
                  pl.BlockSpec((1, Q_LORA), lambda i: (0, 0)),
                  pl.BlockSpec((1, KV_LORA), lambda i: (0, 0))],
        out_specs=[pl.BlockSpec((bm, Q_LORA), lambda i: (i, 0)),
                   pl.BlockSpec((bm, KV_LORA), lambda i: (i, 0)),
                   pl.BlockSpec((bm, 2 * LANES), lambda i: (i, 0)),
                   pl.BlockSpec((bm, d), lambda i: (i, 0))],
        out_shape=[jax.ShapeDtypeStruct((m, Q_LORA), BF16),
                   jax.ShapeDtypeStruct((m, KV_LORA), BF16),
                   jax.ShapeDtypeStruct((m, 2 * LANES), F32),
                   jax.ShapeDtypeStruct((m, d), BF16)],
        scratch_shapes=[pltpu.VMEM((n_lat, d), BF16), pltpu.VMEM((2 * LANES, d), BF16)],
        compiler_params=_params(1),
        name="latent_proj",
    )(x2, mod, mod, w_t, g_q, g_kv)


def _qkv_up_kernel(qn_ref, kvn_ref, krr_ref, pos_ref, inv_ref, wq_raw, wkv_raw,
                   qnope_ref, qrope_ref, knope_ref, vt_ref, kr_ref, wq_ref, wk_ref, wvt_ref,
                   cos_ref, sin_ref):
    @pl.when(pl.program_id(0) == 0)
    def _():
        for h in range(N_HEADS):
            q0 = h * (QK_NOPE + QK_ROPE)
            blk = slice(h * LANES, (h + 1) * LANES)
            wq_ref[:, blk] = wq_raw[:, q0:q0 + QK_NOPE].astype(BF16)
            rope = wq_raw[:, q0 + QK_NOPE:q0 + QK_NOPE + QK_ROPE]
            packed = jnp.concatenate([rope, -rope[:, HALF:], rope[:, :HALF]], axis=1)
            wq_ref[:, HEAD_PAD + h * LANES:HEAD_PAD + (h + 1) * LANES] = packed.astype(BF16)
            k0 = h * (QK_NOPE + V_HEAD)
            wk_ref[:, blk] = wkv_raw[:, k0:k0 + QK_NOPE].astype(BF16)
            wvt_ref[blk, :] = wkv_raw[:, k0 + QK_NOPE:k0 + QK_NOPE + V_HEAD].T.astype(BF16)

    ang = pos_ref[...].astype(F32) * inv_ref[...]
    dense_lane_group = lax.broadcasted_iota(jnp.int32, ang.shape, 1) // HALF
    for name_ref, table in ((cos_ref, jnp.cos(ang)), (sin_ref, jnp.sin(ang))):
        for g in range(LANES // HALF):
            own = jnp.where(dense_lane_group == g, table, 0.0)
            spread = own
            for k in range(1, LANES // HALF):
                spread = spread + pltpu.roll(own, k * HALF, 1)
            name_ref[pl.ds(g, ang.shape[0], stride=LANES // HALF), :] = spread
    cos = cos_ref[...]
    sin = sin_ref[...]
    qn = qn_ref[...]
    kvn = kvn_ref[...]
    grp = UP_HEAD_GROUP * LANES
    lane = lax.broadcasted_iota(jnp.int32, cos.shape, 1)
    cs = jnp.where(lane < QK_ROPE, cos, sin) * Q_PRESCALE
    for g in range(N_HEADS // UP_HEAD_GROUP):
        lo, hi = g * grp, (g + 1) * grp
        qnope_ref[:, lo:hi] = (_dot(qn, wq_ref[:, lo:hi]) * Q_PRESCALE).astype(BF16)
        p = _dot(qn, wq_ref[:, HEAD_PAD + lo:HEAD_PAD + hi])
        for h in range(UP_HEAD_GROUP):
            t = p[:, h * LANES:(h + 1) * LANES] * cs
            qrope_ref[:, lo + h * LANES:lo + (h + 1) * LANES] = (
                t + pltpu.roll(t, QK_ROPE, 1)).astype(BF16)
        knope_ref[:, lo:hi] = _dot(kvn, wk_ref[:, lo:hi]).astype(BF16)
        vt_ref[lo:hi, :] = _dot_nt(wvt_ref[lo:hi, :], kvn).astype(BF16)
    krr = krr_ref[...]
    kr_ref[...] = (krr[:, :LANES] * cos + krr[:, LANES:] * sin).astype(BF16)


def _qkv_up(qn, kvn, kr_raw, pos_dense, inv_row, w_q_b, w_kv_b, bm=512):
    m = qn.shape[0]
    row = lambda n: pl.BlockSpec((bm, n), lambda i: (i, 0))
    const = lambda a: _resident(a.shape, lambda i: (0, 0))
    return pl.pallas_call(
        _qkv_up_kernel,
        grid=(m // bm,),
        in_specs=[row(Q_LORA), row(KV_LORA), row(2 * LANES),
                  pl.BlockSpec((bm // (LANES // HALF), LANES), lambda i: (i, 0)),
                  const(inv_row), const(w_q_b), const(w_kv_b)],
        out_specs=[row(HEAD_PAD), row(HEAD_PAD), row(HEAD_PAD),
                   pl.BlockSpec((HEAD_PAD, bm), lambda i: (0, i)), row(LANES)],
        out_shape=[jax.ShapeDtypeStruct((m, HEAD_PAD), BF16)] * 3
        + [jax.ShapeDtypeStruct((HEAD_PAD, m), BF16), jax.ShapeDtypeStruct((m, LANES), BF16)],
        scratch_shapes=[pltpu.VMEM((Q_LORA, 2 * HEAD_PAD), BF16),
                        pltpu.VMEM((KV_LORA, HEAD_PAD), BF16),
                        pltpu.VMEM((HEAD_PAD, KV_LORA), BF16),
                        pltpu.VMEM((bm, LANES), F32), pltpu.VMEM((bm, LANES), F32)],
        compiler_params=_params(1),
        name="qkv_up",
    )(qn, kvn, kr_raw, pos_dense, inv_row, w_q_b, w_kv_b)


def _conv_kernel(u_ref, wb_ref, wc_ref, wx_ref, wconv_ref, o_ref, halo_ref, *, tiles_per_seq):
    i = pl.program_id(0)
    j = pl.program_id(1)
    u = u_ref[...]
    bm, tn = o_ref.shape

    @pl.when(i % tiles_per_seq == 0)
    def _():
        halo_ref[j] = jnp.zeros(halo_ref.shape[1:], F32)

    for c0, width in _col_chunks(tn, MXU_WIDTH):
        cols = slice(c0, c0 + width)
        rows = lax.broadcasted_iota(jnp.int32, (bm, width), 0)
        cb = _dot_nt(u, wb_ref[cols, :])
        z = _dot_nt(u, wc_ref[cols, :]) * _dot_nt(u, wx_ref[cols, :])
        prev = halo_ref[j, :, cols]
        z1 = jnp.where(rows == 0, prev[SUBLANES - 1:SUBLANES, :], pltpu.roll(z, 1, 0))
        z2 = jnp.where(rows == 0, prev[SUBLANES - 2:SUBLANES - 1, :],
                       jnp.where(rows == 1, prev[SUBLANES - 1:SUBLANES, :], pltpu.roll(z, 2, 0)))
        halo_ref[j, :, cols] = z[bm - SUBLANES:, :]
        w = wconv_ref[:, cols]
        o_ref[:, cols] = (cb * (w[0:1, :] * z2 + w[1:2, :] * z1 + w[2:3, :] * z)).astype(BF16)


def _conv_branch(u, w_rows, w_conv, bm=1024, tn=1024):
    m, d = u.shape
    n = w_conv.shape[1]
    per_seq = SEQ // bm
    w_spec = lambda piece: pl.BlockSpec((tn, d), lambda i, j: (piece * (n // tn) + j, 0))
    return pl.pallas_call(
        functools.partial(_conv_kernel, tiles_per_seq=per_seq),
        grid=(m // bm, n // tn),
        in_specs=[pl.BlockSpec((bm, d), lambda i, j: (i, 0)),
                  w_spec(0), w_spec(1), w_spec(2),
                  pl.BlockSpec((CONV_K, tn), lambda i, j: (0, j))],
        out_specs=pl.BlockSpec((bm, tn), lambda i, j: (i, j)),
        out_shape=jax.ShapeDtypeStruct((m, n), BF16),
        scratch_shapes=[pltpu.VMEM((n // tn, SUBLANES, tn), F32)],
        compiler_params=_params(2),
        name="conv_branch",
    )(u, w_rows, w_rows, w_rows, w_conv)


def _gate_kernel(*refs, n_cast):
    (u_ref, wa_ref, wb_ref), cast_src, o_ref, cast_dst = _split_refs(refs, 3, n_cast)
    pieces = _cast_pieces(cast_src, cast_dst)
    u = u_ref[...]
    n = wa_ref.shape[0]
    passes = ([(wa_ref, 0, c0, 2 * COL_CHUNK) for c0 in range(0, n, 2 * COL_CHUNK)]
              + [(wb_ref, n, c0, width) for c0, width in _col_chunks(n, 2 * COL_CHUNK)])
    for k, (w_ref, out0, c0, width) in enumerate(passes):
        o_ref[:, out0 + c0:out0 + c0 + width] = jax.nn.sigmoid(
            _dot_nt(u, w_ref[c0:c0 + width, :])).astype(BF16)
        _convert_pieces(pieces, k, len(passes))


def _gates(u, w_rows, first_block, cast_jobs, bm=1024):
    m, d = u.shape
    specs = [_slab_specs(w, m // bm, lambda i: i, row0) for w, row0 in cast_jobs]
    outs = pl.pallas_call(
        functools.partial(_gate_kernel, n_cast=len(cast_jobs)),
        grid=(m // bm,),
        in_specs=[pl.BlockSpec((bm, d), lambda i: (i, 0)),
                  _resident((d, d), lambda i: (first_block, 0)),
                  _resident((d, d), lambda i: (first_block + 1, 0))] + [s[0] for s in specs],
        out_specs=[pl.BlockSpec((bm, 2 * d), lambda i: (i, 0))] + [s[1] for s in specs],
        out_shape=[jax.ShapeDtypeStruct((m, 2 * d), BF16)] + [s[2] for s in specs],
        compiler_params=_params(1),
        name="merge_gates",
    )(u, w_rows, w_rows, *[w for w, _ in cast_jobs])
    return outs[0], outs[1:]


def _attn_kernel(*refs, n_cast):
    (qn_ref, qr_ref, kn_ref, kr_ref, vt_ref), cast_src, o_ref, cast_dst = _split_refs(refs, 5, n_cast)
    pieces = _cast_pieces(cast_src, cast_dst)
    kr = kr_ref[...]
    ks = [jnp.concatenate([kn_ref[:, a * LANES:(a + 1) * LANES], kr], axis=1)
          for a in range(ATTN_HEADS_PER_STEP)]
    key = lax.broadcasted_iota(jnp.int32, (ATTN_TQ, ATTN_TQ), 0)
    qry = lax.broadcasted_iota(jnp.int32, (ATTN_TQ, ATTN_TQ), 1)
    allowed = (key // CHUNK) <= (qry // CHUNK)
    n_blk = SEQ // ATTN_TQ

    def scores(task):
        a, i = task
        q0 = i * ATTN_TQ
        L = q0 + ATTN_TQ
        lanes = slice(a * LANES, (a + 1) * LANES)
        q = jnp.concatenate([qn_ref[q0:L, lanes], qr_ref[q0:L, lanes]], axis=1)
        return _dot_nt(ks[a][:L], q)

    tasks = [(a, i) for i in range(n_blk - 1, -1, -1) for a in range(ATTN_HEADS_PER_STEP)]
    pending = [scores(t) for t in tasks[:ATTN_LOOKAHEAD]]
    for pos, (a, i) in enumerate(tasks):
        q0 = i * ATTN_TQ
        L = q0 + ATTN_TQ
        s_t = pending.pop(0)
        if pos + ATTN_LOOKAHEAD < len(tasks):
            pending.append(scores(tasks[pos + ATTN_LOOKAHEAD]))
        heads = slice(a * V_HEAD, (a + 1) * V_HEAD)
        s_diag = jnp.where(allowed, s_t[q0:, :], -1e30)
        m_col = jnp.max(s_diag, axis=0, keepdims=True)
        if i > 0:
            s_full = s_t[:q0, :]
            m_col = jnp.maximum(m_col, jnp.max(s_full, axis=0, keepdims=True))
        e_diag = jnp.exp2(s_diag - m_col)
        denom = jnp.sum(e_diag, axis=0, keepdims=True)
        o_t = _dot(vt_ref[heads, q0:L], e_diag.astype(BF16))
        if i > 0:
            e_full = jnp.exp2(s_full - m_col)
            denom = denom + jnp.sum(e_full, axis=0, keepdims=True)
            o_t = o_t + _dot(vt_ref[heads, :q0], e_full.astype(BF16))
        o_ref[q0:L, a * LANES:(a + 1) * LANES] = (o_t / denom).T.astype(BF16)
        _convert_pieces(pieces, pos, len(tasks))


def _attention(q_nope, q_rope, k_nope, k_rope, v_t, cast_jobs):
    m = q_nope.shape[0]
    hps = ATTN_HEADS_PER_STEP
    grid = (m // SEQ, N_HEADS // hps)
    head = pl.BlockSpec((SEQ, hps * LANES), lambda b, h: (b, h))
    specs = [_slab_specs(w, grid[0] * grid[1], lambda b, h: b * grid[1] + h, row0)
             for w, row0 in cast_jobs]
    outs = pl.pallas_call(
        functools.partial(_attn_kernel, n_cast=len(cast_jobs)),
        grid=grid,
        in_specs=[head, head, head, pl.BlockSpec((SEQ, LANES), lambda b, h: (b, 0)),
                  pl.BlockSpec((hps * V_HEAD, SEQ), lambda b, h: (h, b))] + [s[0] for s in specs],
        out_specs=[head] + [s[1] for s in specs],
        out_shape=[jax.ShapeDtypeStruct((m, HEAD_PAD), BF16)] + [s[2] for s in specs],
        compiler_params=_params(2),
        name="mla_attention",
    )(q_nope, q_rope, k_nope, k_rope, v_t, *[w for w, _ in cast_jobs])
    return outs[0], outs[1:]


def _merge_kernel(*refs, n_cast):
    (a_ref, c_ref, wa_ref, wb_ref, g_ref), cast_src, o_ref, cast_dst = _split_refs(refs, 5, n_cast)
    pieces = _cast_pieces(cast_src, cast_dst)
    a = a_ref[...]
    cm = c_ref[...]
    n = o_ref.shape[1]
    chunks = _col_chunks(n, MXU_WIDTH)
    for k, (c0, width) in enumerate(chunks):
        cols = slice(c0, c0 + width)
        y_a = _dot(a, wa_ref[:, cols])
        y_b = _dot(cm, wb_ref[:, cols])
        o_ref[:, cols] = (g_ref[:, cols].astype(F32) * y_a
                          + g_ref[:, n + c0:n + c0 + width].astype(F32) * y_b).astype(BF16)
        _convert_pieces(pieces, k, len(chunks))


def _merge(attn, cm, w_oa, w_ob, gates, cast_jobs, bm=512):
    m, d = attn.shape
    n = w_oa.shape[1]
    lhs = pl.BlockSpec((bm, d), lambda i: (i, 0))
    specs = [_slab_specs(w, m // bm, lambda i: i, row0) for w, row0 in cast_jobs]
    outs = pl.pallas_call(
        functools.partial(_merge_kernel, n_cast=len(cast_jobs)),
        grid=(m // bm,),
        in_specs=[lhs, lhs, _resident((d, n), lambda i: (0, 0)), _resident((d, n), lambda i: (0, 0)),
                  pl.BlockSpec((bm, 2 * n), lambda i: (i, 0))] + [s[0] for s in specs],
        out_specs=[pl.BlockSpec((bm, n), lambda i: (i, 0))] + [s[1] for s in specs],
        out_shape=[jax.ShapeDtypeStruct((m, n), BF16)] + [s[2] for s in specs],
        compiler_params=_params(1),
        name="branch_merge",
    )(attn, cm, w_oa, w_ob, gates, *[w for w, _ in cast_jobs])
    return outs[0], outs[1:]


def _mix_out_kernel(mg_ref, w_ref, x_ref, g1_ref, lg_ref, lb_ref, o_ref, *, tiles_per_seq):
    gate1 = _batch_row(g1_ref, pl.program_id(0), tiles_per_seq)
    n_chunks = mg_ref.shape[0] // ROW_CHUNK
    mix = _dot(mg_ref[0:ROW_CHUNK, :], w_ref[...])
    for r in range(n_chunks):
        rows = slice(r * ROW_CHUNK, (r + 1) * ROW_CHUNK)
        cur = mix
        if r + 1 < n_chunks:
            mix = _dot(mg_ref[(r + 1) * ROW_CHUNK:(r + 2) * ROW_CHUNK, :], w_ref[...])
        y = DEEPNORM_ALPHA * x_ref[rows, :] + gate1 * cur
        o_ref[rows, :] = _layer_norm(y, lg_ref[...], lb_ref[...])


def _mix_out(merged, w_o, x2, mod, ln_g, ln_b, bm=512):
    m, d = x2.shape
    per_seq = SEQ // bm
    row = pl.BlockSpec((bm, d), lambda i: (i, 0))
    vec = pl.BlockSpec((1, d), lambda i: (0, 0))
    return pl.pallas_call(
        functools.partial(_mix_out_kernel, tiles_per_seq=per_seq),
        grid=(m // bm,),
        in_specs=[row, _resident((d, d), lambda i: (0, 0)), row,
                  _mod_spec(mod, MOD_GATE1), vec, vec],
        out_specs=row,
        out_shape=jax.ShapeDtypeStruct((m, d), F32),
        compiler_params=_params(1),
        name="mix_out_ln",
    )(merged, w_o, x2, mod, ln_g, ln_b)


def _ffn_kernel(x_ref, sc_ref, sh_ref, g2_ref, wg_ref, wu_ref, wo_ref, lg_ref, lb_ref, o_ref,
                u_ref, *, tiles_per_seq):
    acc_ref = o_ref
    tile = pl.program_id(0)
    j = pl.program_id(1)

    @pl.when(j == 0)
    def _():
        u_ref[...] = _modulate(x_ref[...], _batch_row(sc_ref, tile, tiles_per_seq),
                               _batch_row(sh_ref, tile, tiles_per_seq))
        acc_ref[...] = jnp.zeros_like(acc_ref)

    u = u_ref[...]
    tf = wg_ref.shape[1]
    half = tf // 2
    hs = []
    for c0 in (0, half):
        hg = _dot(u, wg_ref[:, c0:c0 + half])
        hs.append((hg * jax.nn.sigmoid(hg) * _dot(u, wu_ref[:, c0:c0 + half])).astype(BF16))
    for c0 in range(0, acc_ref.shape[1], COL_CHUNK):
        cols = slice(c0, c0 + COL_CHUNK)
        acc_ref[:, cols] += _dot(hs[0], wo_ref[:half, cols]) + _dot(hs[1], wo_ref[half:, cols])

    @pl.when(j == pl.num_programs(1) - 1)
    def _():
        y = (DEEPNORM_ALPHA * x_ref[...]
             + _batch_row(g2_ref, tile, tiles_per_seq) * acc_ref[...])
        o_ref[...] = _layer_norm(y, lg_ref[...], lb_ref[...])


def _ffn(x1, mod, w_in, w_out, ln_g, ln_b, bm=1024, tf=512):
    m, d = x1.shape
    nf = D_FF // tf
    per_seq = SEQ // bm
    row = pl.BlockSpec((bm, d), lambda i, j: (i, 0))
    vec = pl.BlockSpec((1, d), lambda i, j: (0, 0))
    return pl.pallas_call(
        functools.partial(_ffn_kernel, tiles_per_seq=per_seq),
        grid=(m // bm, nf),
        in_specs=[row, _mod_spec(mod, MOD_SCALE2), _mod_spec(mod, MOD_SHIFT2),
                  _mod_spec(mod, MOD_GATE2),
                  pl.BlockSpec((d, tf), lambda i, j: (0, j)),
                  pl.BlockSpec((d, tf), lambda i, j: (0, j + nf)),
                  pl.BlockSpec((tf, d), lambda i, j: (j, 0)), vec, vec],
        out_specs=row,
        out_shape=jax.ShapeDtypeStruct((m, d), F32),
        scratch_shapes=[pltpu.VMEM((bm, d), BF16)],
        compiler_params=_params(2, vmem=FFN_VMEM_LIMIT),
        name="swiglu_ffn_ln",
    )(x1, mod, mod, mod, w_in, w_in, w_out, ln_g, ln_b)


def kernel(x, c, positions, w_ada, b_ada, w_in, g_q_a, w_q_b, g_kv_a, w_kv_b, w_o_a,
           w_conv, w_o_b, w_o, ln1_g, ln1_b, w_ffn_in, w_ffn_out, ln2_g, ln2_b):
    bsz, seq, d = x.shape
    assert (seq, d) == (SEQ, D_MODEL) and w_ada.shape[0] == DEPTH
    m = bsz * seq
    x_cur = x.reshape(m, d)

    inv_freq = 1.0 / (ROPE_THETA ** (jnp.arange(0, QK_ROPE, 2, dtype=F32) / QK_ROPE))
    inv_row = jnp.tile(inv_freq, LANES // HALF).reshape(1, LANES)
    groups = LANES // HALF
    pos_dense = jnp.repeat(positions.reshape(m // groups, groups), HALF, axis=1)
    c_pad = jnp.pad(c, ((0, SUBLANES - bsz), (0, 0)))

    for l in range(DEPTH):
        mod = _ada(c_pad, w_ada[l], b_ada[l].reshape(1, -1))

        o_cb = Q_LORA + KV_LORA + QK_ROPE
        w_tf = jnp.swapaxes(w_in[l], 0, 1)

        qn, kvn, kr_raw, u = _latent(x_cur, mod, w_tf,
                                     g_q_a[l].reshape(1, -1), g_kv_a[l].reshape(1, -1))
        q_nope, q_rope, k_nope, v_t, k_rope = _qkv_up(qn, kvn, kr_raw, pos_dense, inv_row,
                                                      w_q_b[l], w_kv_b[l])
        attn, (w_cg, w_oa, w_ob, w_om, w_fi) = _attention(
            q_nope, q_rope, k_nope, k_rope, v_t,
            [(w_tf, o_cb), (w_o_a[l], 0), (w_o_b[l], 0), (w_o[l], 0), (w_ffn_in[l], 0)])
        cm = _conv_branch(u, w_cg, w_conv[l])
        gates, _ = _gates(u, w_cg, 3, [])
        merged, (w_fo,) = _merge(attn, cm, w_oa, w_ob, gates, [(w_ffn_out[l], 0)])
        x_cur = _mix_out(merged, w_om, x_cur, mod,
                         ln1_g[l].reshape(1, -1), ln1_b[l].reshape(1, -1))

        x_cur = _ffn(x_cur, mod, w_fi, w_fo,
                     ln2_g[l].reshape(1, -1), ln2_b[l].reshape(1, -1))

    return x_cur.reshape(bsz, seq, d)
```

```python
import functools
import math

import jax
import jax.numpy as jnp
from jax import lax
from jax.experimental import pallas as pl
from jax.experimental.pallas import tpu as pltpu

F32 = jnp.float32
BF16 = jnp.bfloat16

D_MODEL = 2048
SEQ = 2048
CHUNK = 64
N_HEADS = 16
QK_NOPE = 128
QK_ROPE = 64
V_HEAD = 128
Q_LORA = 512
KV_LORA = 512
ROPE_THETA = 10000.0
ATTN_SCALE = (QK_NOPE + QK_ROPE) ** -0.5
CONV_K = 3
D_FF = 5632
DEPTH = 1
DEEPNORM_ALPHA = (2.0 * DEPTH) ** 0.25
LN_EPS = 1e-5
RMS_EPS = 1e-6

LANES = 128
SUBLANES = 8
HALF = QK_ROPE // 2
HEAD_PAD = N_HEADS * LANES
ATTN_TQ = 256
ATTN_LOOKAHEAD = 4
ATTN_HEADS_PER_STEP = 2
CAST_PIECE_ROWS = 32
UP_HEAD_GROUP = 4
ROW_CHUNK = 256
COL_CHUNK = 512
MXU_WIDTH = 256
Q_PRESCALE = ATTN_SCALE * math.log2(math.e)

VMEM_LIMIT = 58 * 1024 * 1024
FFN_VMEM_LIMIT = 62 * 1024 * 1024


def _params(n_axes, vmem=VMEM_LIMIT):
    return pltpu.CompilerParams(
        dimension_semantics=("arbitrary",) * n_axes, vmem_limit_bytes=vmem)


def _resident(shape, index_map):
    return pl.BlockSpec(shape, index_map, pipeline_mode=pl.Buffered(1))


def _slab_specs(w, n_steps, step_of, row0=0):
    n_rows = w.shape[0] - row0
    cols = w.shape[1]
    bf16_rows = 2 * SUBLANES
    if n_rows % (n_steps * bf16_rows) == 0:
        rb = n_rows // n_steps
        in_spec = pl.BlockSpec(
            (pl.Element(rb), pl.Element(cols)),
            lambda *g: (pl.multiple_of(row0 + step_of(*g) * rb, SUBLANES), 0))
        out_spec = pl.BlockSpec((rb, cols), lambda *g: (step_of(*g), 0))
    else:
        assert row0 == 0 and n_rows % (n_steps // 2 * bf16_rows) == 0
        rb = n_rows // (n_steps // 2)
        in_spec = out_spec = pl.BlockSpec(
            (rb, cols // 2), lambda *g: (step_of(*g) // 2, step_of(*g) % 2))
    return in_spec, out_spec, jax.ShapeDtypeStruct((n_rows, cols), BF16)


def _split_refs(refs, n_in, n_cast):
    return refs[:n_in], refs[n_in:n_in + n_cast], refs[n_in + n_cast], refs[n_in + n_cast + 1:]


def _cast_pieces(cast_src, cast_dst):
    pieces = []
    for src, dst in zip(cast_src, cast_dst, strict=True):
        rows = src.shape[0]
        n_piece = next(n for n in range(max(1, rows // CAST_PIECE_ROWS), 0, -1)
                       if rows % n == 0 and (rows // n) % (2 * SUBLANES) == 0)
        pieces += [(src, dst, slice(p * rows // n_piece, (p + 1) * rows // n_piece))
                   for p in range(n_piece)]
    return pieces


def _convert_pieces(pieces, k, n):
    for src, dst, rows in pieces[k::n]:
        dst[rows, :] = src[rows, :].astype(BF16)


def _col_chunks(n, width):
    chunks = [(c0, width) for c0 in range(0, n - width, width)]
    c0, size = n - width, width
    while size > MXU_WIDTH:
        size //= 2
        chunks.append((c0, size))
        c0 += size
    chunks.append((c0, size))
    return chunks


def _dot(a, b):
    return jnp.dot(a, b, preferred_element_type=F32)


def _dot_nt(a, b_t):
    return lax.dot_general(a, b_t, (((1,), (1,)), ((), ())), preferred_element_type=F32)


def _modulate(x, scale, shift):
    return (x * (1.0 + scale) + shift).astype(BF16)


MOD_SHIFT1, MOD_SCALE1 = range(2)
MOD_GATE1, MOD_SHIFT2, MOD_SCALE2, MOD_GATE2 = range(4)
N_MOD_EARLY = 2


def _mod_spec(mod, which):
    return pl.BlockSpec((mod.shape[0], D_MODEL), lambda *g: (0, which))


def _batch_row(ref, tile, tiles_per_seq):
    return ref[pl.ds(tile // tiles_per_seq, 1), :]


def _layer_norm(y, g, b):
    mu = jnp.mean(y, axis=-1, keepdims=True)
    yc = y - mu
    var = jnp.mean(yc * yc, axis=-1, keepdims=True)
    return yc * lax.rsqrt(var + LN_EPS) * g + b


def _rms_norm(y, g):
    return y * lax.rsqrt(jnp.mean(y * y, axis=-1, keepdims=True) + RMS_EPS) * g


def _ada_kernel(c_ref, w_ref, b_ref, o_ref):
    c = c_ref[...]
    c_act = (c * jax.nn.sigmoid(c)).astype(BF16)
    o_ref[...] = _dot(c_act, w_ref[...].astype(BF16)) + b_ref[...]


def _ada(c_pad, w_ada, b_ada, n, tn=1024):
    rows, d = c_pad.shape
    return pl.pallas_call(
        _ada_kernel,
        grid=(n // tn,),
        in_specs=[pl.BlockSpec((rows, d), lambda j: (0, 0)),
                  pl.BlockSpec((d, tn), lambda j: (0, j)),
                  pl.BlockSpec((1, tn), lambda j: (0, j))],
        out_specs=pl.BlockSpec((rows, tn), lambda j: (0, j)),
        out_shape=jax.ShapeDtypeStruct((rows, n), F32),
        compiler_params=_params(1),
        name="ada_mod",
    )(c_pad, w_ada, b_ada)


def _latent_kernel(x_ref, sh_ref, sc_ref, w_ref, gq_ref, gkv_ref,
                   qn_ref, kvn_ref, kr_ref, u_ref, wbf_ref, wkr_ref, *, tiles_per_seq):
    n_lat = Q_LORA + KV_LORA

    @pl.when(pl.program_id(0) == 0)
    def _():
        wbf_ref[...] = w_ref[:n_lat, :].astype(BF16)
        wkr_ref[...] = jnp.zeros_like(wkr_ref)
        wkr_ref[0:QK_ROPE, :] = w_ref[n_lat:n_lat + QK_ROPE, :].astype(BF16)
        wkr_ref[LANES:LANES + HALF, :] = (-w_ref[n_lat + HALF:n_lat + QK_ROPE, :]).astype(BF16)
        wkr_ref[LANES + HALF:LANES + QK_ROPE, :] = w_ref[n_lat:n_lat + HALF, :].astype(BF16)

    tile = pl.program_id(0)
    u = _modulate(x_ref[...], _batch_row(sc_ref, tile, tiles_per_seq),
                  _batch_row(sh_ref, tile, tiles_per_seq))
    u_ref[...] = u
    p = _dot_nt(u, wbf_ref[...])
    qn_ref[...] = _rms_norm(p[:, :Q_LORA], gq_ref[...]).astype(BF16)
    kvn_ref[...] = _rms_norm(p[:, Q_LORA:], gkv_ref[...]).astype(BF16)
    kr_ref[...] = _dot_nt(u, wkr_ref[...])


def _latent(x2, mod, w_t, g_q, g_kv, bm=1024):
    m, d = x2.shape
    per_seq = SEQ // bm
    n_lat = Q_LORA + KV_LORA
    return pl.pallas_call(
        functools.partial(_latent_kernel, tiles_per_seq=per_seq),
        grid=(m // bm,),
        in_specs=[pl.BlockSpec((bm, d), lambda i: (i, 0)),
                  _mod_spec(mod, MOD_SHIFT1), _mod_spec(mod, MOD_SCALE1),
                  _resident((n_lat + QK_ROPE, d), lambda i: (0, 0)),
                  pl.BlockSpec((1, Q_LORA), lambda i: (0, 0)),
                  pl.BlockSpec((1, KV_LORA), lambda i: (0, 0))],
        out_specs=[pl.BlockSpec((bm, Q_LORA), lambda i: (i, 0)),
                   pl.BlockSpec((bm, KV_LORA), lambda i: (i, 0)),
                   pl.BlockSpec((bm, 2 * LANES), lambda i: (i, 0)),
                   pl.BlockSpec((bm, d), lambda i: (i, 0))],
        out_shape=[jax.ShapeDtypeStruct((m, Q_LORA), BF16),
                   jax.ShapeDtypeStruct((m, KV_LORA), BF16),
                   jax.ShapeDtypeStruct((m, 2 * LANES), F32),
                   jax.ShapeDtypeStruct((m, d), BF16)],
        scratch_shapes=[pltpu.VMEM((n_lat, d), BF16), pltpu.VMEM((2 * LANES, d), BF16)],
        compiler_params=_params(1),
        name="latent_proj",
    )(x2, mod, mod, w_t, g_q, g_kv)


def _qkv_up_kernel(qn_ref, kvn_ref, krr_ref, pos_ref, inv_ref, wq_raw, wkv_raw,
                   qnope_ref, qrope_ref, knope_ref, vt_ref, kr_ref, wq_ref, wk_ref, wvt_ref,
                   cos_ref, sin_ref):
    @pl.when(pl.program_id(0) == 0)
    def _():
        for h in range(N_HEADS):
            q0 = h * (QK_NOPE + QK_ROPE)
            blk = slice(h * LANES, (h + 1) * LANES)
            wq_ref[:, blk] = wq_raw[:, q0:q0 + QK_NOPE].astype(BF16)
            rope = wq_raw[:, q0 + QK_NOPE:q0 + QK_NOPE + QK_ROPE]
            packed = jnp.concatenate([rope, -rope[:, HALF:], rope[:, :HALF]], axis=1)
            wq_ref[:, HEAD_PAD + h * LANES:HEAD_PAD + (h + 1) * LANES] = packed.astype(BF16)
            k0 = h * (QK_NOPE + V_HEAD)
            wk_ref[:, blk] = wkv_raw[:, k0:k0 + QK_NOPE].astype(BF16)
            wvt_ref[blk, :] = wkv_raw[:, k0 + QK_NOPE:k0 + QK_NOPE + V_HEAD].T.astype(BF16)

    ang = pos_ref[...].astype(F32) * inv_ref[...]
    dense_lane_group = lax.broadcasted_iota(jnp.int32, ang.shape, 1) // HALF
    for name_ref, table in ((cos_ref, jnp.cos(ang)), (sin_ref, jnp.sin(ang))):
        for g in range(LANES // HALF):
            own = jnp.where(dense_lane_group == g, table, 0.0)
            spread = own
            for k in range(1, LANES // HALF):
                spread = spread + pltpu.roll(own, k * HALF, 1)
            name_ref[pl.ds(g, ang.shape[0], stride=LANES // HALF), :] = spread
    cos = cos_ref[...]
    sin = sin_ref[...]
    qn = qn_ref[...]
    kvn = kvn_ref[...]
    grp = UP_HEAD_GROUP * LANES
    lane = lax.broadcasted_iota(jnp.int32, cos.shape, 1)
    cs = jnp.where(lane < QK_ROPE, cos, sin) * Q_PRESCALE
    for g in range(N_HEADS // UP_HEAD_GROUP):
        lo, hi = g * grp, (g + 1) * grp
        qnope_ref[:, lo:hi] = (_dot(qn, wq_ref[:, lo:hi]) * Q_PRESCALE).astype(BF16)
        p = _dot(qn, wq_ref[:, HEAD_PAD + lo:HEAD_PAD + hi])
        for h in range(UP_HEAD_GROUP):
            t = p[:, h * LANES:(h + 1) * LANES] * cs
            qrope_ref[:, lo + h * LANES:lo + (h + 1) * LANES] = (
                t + pltpu.roll(t, QK_ROPE, 1)).astype(BF16)
        knope_ref[:, lo:hi] = _dot(kvn, wk_ref[:, lo:hi]).astype(BF16)
        vt_ref[lo:hi, :] = _dot_nt(wvt_ref[lo:hi, :], kvn).astype(BF16)
    krr = krr_ref[...]
    kr_ref[...] = (krr[:, :LANES] * cos + krr[:, LANES:] * sin).astype(BF16)


def _qkv_up(qn, kvn, kr_raw, pos_dense, inv_row, w_q_b, w_kv_b, bm=512):
    m = qn.shape[0]
    row = lambda n: pl.BlockSpec((bm, n), lambda i: (i, 0))
    const = lambda a: _resident(a.shape, lambda i: (0, 0))
    return pl.pallas_call(
        _qkv_up_kernel,
        grid=(m // bm,),
        in_specs=[row(Q_LORA), row(KV_LORA), row(2 * LANES),
                  pl.BlockSpec((bm // (LANES // HALF), LANES), lambda i: (i, 0)),
                  const(inv_row), const(w_q_b), const(w_kv_b)],
        out_specs=[row(HEAD_PAD), row(HEAD_PAD), row(HEAD_PAD),
                   pl.BlockSpec((HEAD_PAD, bm), lambda i: (0, i)), row(LANES)],
        out_shape=[jax.ShapeDtypeStruct((m, HEAD_PAD), BF16)] * 3
        + [jax.ShapeDtypeStruct((HEAD_PAD, m), BF16), jax.ShapeDtypeStruct((m, LANES), BF16)],
        scratch_shapes=[pltpu.VMEM((Q_LORA, 2 * HEAD_PAD), BF16),
                        pltpu.VMEM((KV_LORA, HEAD_PAD), BF16),
                        pltpu.VMEM((HEAD_PAD, KV_LORA), BF16),
                        pltpu.VMEM((bm, LANES), F32), pltpu.VMEM((bm, LANES), F32)],
        compiler_params=_params(1),
        name="qkv_up",
    )(qn, kvn, kr_raw, pos_dense, inv_row, w_q_b, w_kv_b)


def _conv_kernel(u_ref, wb_ref, wc_ref, wx_ref, wconv_ref, o_ref, halo_ref, *, tiles_per_seq):
    i = pl.program_id(0)
    j = pl.program_id(1)
    u = u_ref[...]
    bm, tn = o_ref.shape

    @pl.when(i % tiles_per_seq == 0)
    def _():
        halo_ref[j] = jnp.zeros(halo_ref.shape[1:], F32)

    for c0, width in _col_chunks(tn, MXU_WIDTH):
        cols = slice(c0, c0 + width)
        rows = lax.broadcasted_iota(jnp.int32, (bm, width), 0)
        cb = _dot_nt(u, wb_ref[cols, :])
        z = _dot_nt(u, wc_ref[cols, :]) * _dot_nt(u, wx_ref[cols, :])
        prev = halo_ref[j, :, cols]
        z1 = jnp.where(rows == 0, prev[SUBLANES - 1:SUBLANES, :], pltpu.roll(z, 1, 0))
        z2 = jnp.where(rows == 0, prev[SUBLANES - 2:SUBLANES - 1, :],
                       jnp.where(rows == 1, prev[SUBLANES - 1:SUBLANES, :], pltpu.roll(z, 2, 0)))
        halo_ref[j, :, cols] = z[bm - SUBLANES:, :]
        w = wconv_ref[:, cols]
        o_ref[:, cols] = (cb * (w[0:1, :] * z2 + w[1:2, :] * z1 + w[2:3, :] * z)).astype(BF16)


def _conv_branch(u, w_rows, w_conv, bm=1024, tn=1024):
    m, d = u.shape
    n = w_conv.shape[1]
    per_seq = SEQ // bm
    w_spec = lambda piece: pl.BlockSpec((tn, d), lambda i, j: (piece * (n // tn) + j, 0))
    return pl.pallas_call(
        functools.partial(_conv_kernel, tiles_per_seq=per_seq),
        grid=(m // bm, n // tn),
        in_specs=[pl.BlockSpec((bm, d), lambda i, j: (i, 0)),
                  w_spec(0), w_spec(1), w_spec(2),
                  pl.BlockSpec((CONV_K, tn), lambda i, j: (0, j))],
        out_specs=pl.BlockSpec((bm, tn), lambda i, j: (i, j)),
        out_shape=jax.ShapeDtypeStruct((m, n), BF16),
        scratch_shapes=[pltpu.VMEM((n // tn, SUBLANES, tn), F32)],
        compiler_params=_params(2),
        name="conv_branch",
    )(u, w_rows, w_rows, w_rows, w_conv)


def _gate_kernel(*refs, n_cast):
    (u_ref, wa_ref, wb_ref), cast_src, o_ref, cast_dst = _split_refs(refs, 3, n_cast)
    pieces = _cast_pieces(cast_src, cast_dst)
    u = u_ref[...]
    n = wa_ref.shape[0]
    passes = ([(wa_ref, 0, c0, 2 * COL_CHUNK) for c0 in range(0, n, 2 * COL_CHUNK)]
              + [(wb_ref, n, c0, width) for c0, width in _col_chunks(n, 2 * COL_CHUNK)])
    for k, (w_ref, out0, c0, width) in enumerate(passes):
        o_ref[:, out0 + c0:out0 + c0 + width] = jax.nn.sigmoid(
            _dot_nt(u, w_ref[c0:c0 + width, :])).astype(BF16)
        _convert_pieces(pieces, k, len(passes))


def _gates(u, w_rows, first_block, cast_jobs, bm=1024):
    m, d = u.shape
    specs = [_slab_specs(w, m // bm, lambda i: i, row0) for w, row0 in cast_jobs]
    outs = pl.pallas_call(
        functools.partial(_gate_kernel, n_cast=len(cast_jobs)),
        grid=(m // bm,),
        in_specs=[pl.BlockSpec((bm, d), lambda i: (i, 0)),
                  _resident((d, d), lambda i: (first_block, 0)),
                  _resident((d, d), lambda i: (first_block + 1, 0))] + [s[0] for s in specs],
        out_specs=[pl.BlockSpec((bm, 2 * d), lambda i: (i, 0))] + [s[1] for s in specs],
        out_shape=[jax.ShapeDtypeStruct((m, 2 * d), BF16)] + [s[2] for s in specs],
        compiler_params=_params(1),
        name="merge_gates",
    )(u, w_rows, w_rows, *[w for w, _ in cast_jobs])
    return outs[0], outs[1:]


def _attn_kernel(*refs, n_cast):
    (qn_ref, qr_ref, kn_ref, kr_ref, vt_ref), cast_src, o_ref, cast_dst = _split_refs(refs, 5, n_cast)
    pieces = _cast_pieces(cast_src, cast_dst)
    kr = kr_ref[...]
    ks = [jnp.concatenate([kn_ref[:, a * LANES:(a + 1) * LANES], kr], axis=1)
          for a in range(ATTN_HEADS_PER_STEP)]
    key = lax.broadcasted_iota(jnp.int32, (ATTN_TQ, ATTN_TQ), 0)
    qry = lax.broadcasted_iota(jnp.int32, (ATTN_TQ, ATTN_TQ), 1)
    allowed = (key // CHUNK) <= (qry // CHUNK)
    n_blk = SEQ // ATTN_TQ

    def scores(task):
        a, i = task
        q0 = i * ATTN_TQ
        L = q0 + ATTN_TQ
        lanes = slice(a * LANES, (a + 1) * LANES)
        q = jnp.concatenate([qn_ref[q0:L, lanes], qr_ref[q0:L, lanes]], axis=1)
        return _dot_nt(ks[a][:L], q)

    tasks = [(a, i) for i in range(n_blk - 1, -1, -1) for a in range(ATTN_HEADS_PER_STEP)]
    pending = [scores(t) for t in tasks[:ATTN_LOOKAHEAD]]
    for pos, (a, i) in enumerate(tasks):
        q0 = i * ATTN_TQ
        L = q0 + ATTN_TQ
        s_t = pending.pop(0)
        if pos + ATTN_LOOKAHEAD < len(tasks):
            pending.append(scores(tasks[pos + ATTN_LOOKAHEAD]))
        heads = slice(a * V_HEAD, (a + 1) * V_HEAD)
        s_diag = jnp.where(allowed, s_t[q0:, :], -1e30)
        m_col = jnp.max(s_diag, axis=0, keepdims=True)
        if i > 0:
            s_full = s_t[:q0, :]
            m_col = jnp.maximum(m_col, jnp.max(s_full, axis=0, keepdims=True))
        e_diag = jnp.exp2(s_diag - m_col)
        denom = jnp.sum(e_diag, axis=0, keepdims=True)
        o_t = _dot(vt_ref[heads, q0:L], e_diag.astype(BF16))
        if i > 0:
            e_full = jnp.exp2(s_full - m_col)
            denom = denom + jnp.sum(e_full, axis=0, keepdims=True)
            o_t = o_t + _dot(vt_ref[heads, :q0], e_full.astype(BF16))
        o_ref[q0:L, a * LANES:(a + 1) * LANES] = (o_t / denom).T.astype(BF16)
        _convert_pieces(pieces, pos, len(tasks))


def _attention(q_nope, q_rope, k_nope, k_rope, v_t, cast_jobs):
    m = q_nope.shape[0]
    hps = ATTN_HEADS_PER_STEP
    grid = (m // SEQ, N_HEADS // hps)
    head = pl.BlockSpec((SEQ, hps * LANES), lambda b, h: (b, h))
    specs = [_slab_specs(w, grid[0] * grid[1], lambda b, h: b * grid[1] + h, row0)
             for w, row0 in cast_jobs]
    outs = pl.pallas_call(
        functools.partial(_attn_kernel, n_cast=len(cast_jobs)),
        grid=grid,
        in_specs=[head, head, head, pl.BlockSpec((SEQ, LANES), lambda b, h: (b, 0)),
                  pl.BlockSpec((hps * V_HEAD, SEQ), lambda b, h: (h, b))] + [s[0] for s in specs],
        out_specs=[head] + [s[1] for s in specs],
        out_shape=[jax.ShapeDtypeStruct((m, HEAD_PAD), BF16)] + [s[2] for s in specs],
        compiler_params=_params(2),
        name="mla_attention",
    )(q_nope, q_rope, k_nope, k_rope, v_t, *[w for w, _ in cast_jobs])
    return outs[0], outs[1:]


def _merge_kernel(*refs, n_cast, bsz):
    *refs, mod_ref, cb_ref = refs
    (a_ref, c_ref, wa_ref, wb_ref, g_ref, ct_ref, wada_ref, bada_ref), cast_src, o_ref, cast_dst = (
        _split_refs(refs, 8, n_cast))
    pieces = _cast_pieces(cast_src, cast_dst)
    d = ct_ref.shape[0]

    @pl.when(pl.program_id(0) == 0)
    def _():
        ct = ct_ref[...]
        act = ct * jax.nn.sigmoid(ct)
        for b in range(bsz):
            cb_ref[b * d:(b + 1) * d, :] = jnp.broadcast_to(act[:, b:b + 1], (d, LANES))

    a = a_ref[...]
    cm = c_ref[...]
    n = o_ref.shape[1]
    chunks = _col_chunks(n, MXU_WIDTH)
    lane_groups = wada_ref.shape[1] // LANES
    acc = [[jnp.zeros((SUBLANES, LANES), F32) for _ in range(lane_groups)] for _ in range(bsz)]
    row_groups = d // SUBLANES
    for k, (c0, width) in enumerate(chunks):
        cols = slice(c0, c0 + width)
        y_a = _dot(a, wa_ref[:, cols])
        y_b = _dot(cm, wb_ref[:, cols])
        o_ref[:, cols] = (g_ref[:, cols].astype(F32) * y_a
                          + g_ref[:, n + c0:n + c0 + width].astype(F32) * y_b).astype(BF16)
        _convert_pieces(pieces, k, len(chunks))
        for rg in range(k * row_groups // len(chunks), (k + 1) * row_groups // len(chunks)):
            r0 = rg * SUBLANES
            w_rows = wada_ref[r0:r0 + SUBLANES, :]
            for b in range(bsz):
                c_rows = cb_ref[b * d + r0:b * d + r0 + SUBLANES, :]
                for g in range(lane_groups):
                    acc[b][g] = acc[b][g] + c_rows * w_rows[:, g * LANES:(g + 1) * LANES]
    mod_ref[...] = jnp.zeros_like(mod_ref)
    for b in range(bsz):
        mod_ref[b:b + 1, :] = (jnp.sum(jnp.concatenate(acc[b], axis=1), axis=0, keepdims=True)
                               + bada_ref[...])


def _merge(attn, cm, w_oa, w_ob, gates, cast_jobs, c_t, w_ada, b_ada, mod_col0, bm=512):
    m, d = attn.shape
    n = w_oa.shape[1]
    steps = m // bm
    bsz = m // SEQ
    mod_cols = (w_ada.shape[1] - mod_col0) // steps
    assert mod_col0 % mod_cols == 0 and mod_cols % LANES == 0 and bsz <= SUBLANES
    lhs = pl.BlockSpec((bm, d), lambda i: (i, 0))
    specs = [_slab_specs(w, steps, lambda i: i, row0) for w, row0 in cast_jobs]
    mod_slab = lambda rows: pl.BlockSpec((rows, mod_cols), lambda i: (0, mod_col0 // mod_cols + i))
    outs = pl.pallas_call(
        functools.partial(_merge_kernel, n_cast=len(cast_jobs), bsz=bsz),
        grid=(steps,),
        in_specs=[lhs, lhs, _resident((d, n), lambda i: (0, 0)), _resident((d, n), lambda i: (0, 0)),
                  pl.BlockSpec((bm, 2 * n), lambda i: (i, 0)),
                  _resident(c_t.shape, lambda i: (0, 0)), mod_slab(w_ada.shape[0]), mod_slab(1)]
                 + [s[0] for s in specs],
        out_specs=[pl.BlockSpec((bm, n), lambda i: (i, 0))] + [s[1] for s in specs]
                  + [pl.BlockSpec((SUBLANES, mod_cols), lambda i: (0, i))],
        out_shape=[jax.ShapeDtypeStruct((m, n), BF16)] + [s[2] for s in specs]
                  + [jax.ShapeDtypeStruct((SUBLANES, w_ada.shape[1] - mod_col0), F32)],
        scratch_shapes=[pltpu.VMEM((bsz * c_t.shape[0], LANES), F32)],
        compiler_params=_params(1),
        name="branch_merge",
    )(attn, cm, w_oa, w_ob, gates, c_t, w_ada, b_ada, *[w for w, _ in cast_jobs])
    return outs[0], outs[1:-1], outs[-1]


def _mix_out_kernel(*refs, n_cast, tiles_per_seq):
    (mg_ref, w_ref, x_ref, g1_ref, lg_ref, lb_ref), cast_src, o_ref, cast_dst = (
        _split_refs(refs, 6, n_cast))
    pieces = _cast_pieces(cast_src, cast_dst)
    gate1 = _batch_row(g1_ref, pl.program_id(0), tiles_per_seq)
    n_chunks = mg_ref.shape[0] // ROW_CHUNK
    mix = _dot(mg_ref[0:ROW_CHUNK, :], w_ref[...])
    for r in range(n_chunks):
        rows = slice(r * ROW_CHUNK, (r + 1) * ROW_CHUNK)
        cur = mix
        if r + 1 < n_chunks:
            mix = _dot(mg_ref[(r + 1) * ROW_CHUNK:(r + 2) * ROW_CHUNK, :], w_ref[...])
        _convert_pieces(pieces, r, n_chunks)
        y = DEEPNORM_ALPHA * x_ref[rows, :] + gate1 * cur
        o_ref[rows, :] = _layer_norm(y, lg_ref[...], lb_ref[...])


def _mix_out(merged, w_o, x2, mod, ln_g, ln_b, cast_jobs, bm=512):
    m, d = x2.shape
    per_seq = SEQ // bm
    row = pl.BlockSpec((bm, d), lambda i: (i, 0))
    vec = pl.BlockSpec((1, d), lambda i: (0, 0))
    specs = [_slab_specs(w, m // bm, lambda i: i, row0) for w, row0 in cast_jobs]
    outs = pl.pallas_call(
        functools.partial(_mix_out_kernel, n_cast=len(cast_jobs), tiles_per_seq=per_seq),
        grid=(m // bm,),
        in_specs=[row, _resident((d, d), lambda i: (0, 0)), row,
                  _mod_spec(mod, MOD_GATE1), vec, vec] + [s[0] for s in specs],
        out_specs=[row] + [s[1] for s in specs],
        out_shape=[jax.ShapeDtypeStruct((m, d), F32)] + [s[2] for s in specs],
        compiler_params=_params(1),
        name="mix_out_ln",
    )(merged, w_o, x2, mod, ln_g, ln_b, *[w for w, _ in cast_jobs])
    return outs[0], outs[1:]


def _ffn_kernel(x_ref, sc_ref, sh_ref, g2_ref, wg_ref, wu_ref, wo_ref, lg_ref, lb_ref, o_ref,
                u_ref, *, tiles_per_seq):
    acc_ref = o_ref
    tile = pl.program_id(0)
    j = pl.program_id(1)

    @pl.when(j == 0)
    def _():
        u_ref[...] = _modulate(x_ref[...], _batch_row(sc_ref, tile, tiles_per_seq),
                               _batch_row(sh_ref, tile, tiles_per_seq))
        acc_ref[...] = jnp.zeros_like(acc_ref)

    u = u_ref[...]
    tf = wg_ref.shape[1]
    half = tf // 2
    hs = []
    for c0 in (0, half):
        hg = _dot(u, wg_ref[:, c0:c0 + half])
        hs.append((hg * jax.nn.sigmoid(hg) * _dot(u, wu_ref[:, c0:c0 + half])).astype(BF16))
    for c0 in range(0, acc_ref.shape[1], COL_CHUNK):
        cols = slice(c0, c0 + COL_CHUNK)
        acc_ref[:, cols] += _dot(hs[0], wo_ref[:half, cols]) + _dot(hs[1], wo_ref[half:, cols])

    @pl.when(j == pl.num_programs(1) - 1)
    def _():
        y = (DEEPNORM_ALPHA * x_ref[...]
             + _batch_row(g2_ref, tile, tiles_per_seq) * acc_ref[...])
        o_ref[...] = _layer_norm(y, lg_ref[...], lb_ref[...])


def _ffn(x1, mod, w_in, w_out, ln_g, ln_b, bm=1024, tf=512):
    m, d = x1.shape
    nf = D_FF // tf
    per_seq = SEQ // bm
    row = pl.BlockSpec((bm, d), lambda i, j: (i, 0))
    vec = pl.BlockSpec((1, d), lambda i, j: (0, 0))
    return pl.pallas_call(
        functools.partial(_ffn_kernel, tiles_per_seq=per_seq),
        grid=(m // bm, nf),
        in_specs=[row, _mod_spec(mod, MOD_SCALE2), _mod_spec(mod, MOD_SHIFT2),
                  _mod_spec(mod, MOD_GATE2),
                  pl.BlockSpec((d, tf), lambda i, j: (0, j)),
                  pl.BlockSpec((d, tf), lambda i, j: (0, j + nf)),
                  pl.BlockSpec((tf, d), lambda i, j: (j, 0)), vec, vec],
        out_specs=row,
        out_shape=jax.ShapeDtypeStruct((m, d), F32),
        scratch_shapes=[pltpu.VMEM((bm, d), BF16)],
        compiler_params=_params(2, vmem=FFN_VMEM_LIMIT),
        name="swiglu_ffn_ln",
    )(x1, mod, mod, mod, w_in, w_in, w_out, ln_g, ln_b)


def kernel(x, c, positions, w_ada, b_ada, w_in, g_q_a, w_q_b, g_kv_a, w_kv_b, w_o_a,
           w_conv, w_o_b, w_o, ln1_g, ln1_b, w_ffn_in, w_ffn_out, ln2_g, ln2_b):
    bsz, seq, d = x.shape
    assert (seq, d) == (SEQ, D_MODEL) and w_ada.shape[0] == DEPTH
    m = bsz * seq
    x_cur = x.reshape(m, d)

    inv_freq = 1.0 / (ROPE_THETA ** (jnp.arange(0, QK_ROPE, 2, dtype=F32) / QK_ROPE))
    inv_row = jnp.tile(inv_freq, LANES // HALF).reshape(1, LANES)
    groups = LANES // HALF
    pos_dense = jnp.repeat(positions.reshape(m // groups, groups), HALF, axis=1)
    c_pad = jnp.pad(c, ((0, SUBLANES - bsz), (0, 0)))
    c_t = jnp.swapaxes(c_pad, 0, 1)

    for l in range(DEPTH):
        b_ada_row = b_ada[l].reshape(1, -1)
        mod = _ada(c_pad, w_ada[l], b_ada_row, N_MOD_EARLY * d)

        o_cb = Q_LORA + KV_LORA + QK_ROPE
        w_tf = jnp.swapaxes(w_in[l], 0, 1)

        qn, kvn, kr_raw, u = _latent(x_cur, mod, w_tf,
                                     g_q_a[l].reshape(1, -1), g_kv_a[l].reshape(1, -1))
        q_nope, q_rope, k_nope, v_t, k_rope = _qkv_up(qn, kvn, kr_raw, pos_dense, inv_row,
                                                      w_q_b[l], w_kv_b[l])
        attn, (w_cg, w_oa, w_ob, w_om, w_fi) = _attention(
            q_nope, q_rope, k_nope, k_rope, v_t,
            [(w_tf, o_cb), (w_o_a[l], 0), (w_o_b[l], 0), (w_o[l], 0), (w_ffn_in[l], 0)])
        cm = _conv_branch(u, w_cg, w_conv[l])
        gates, _ = _gates(u, w_cg, 3, [])
        merged, _, mod_late = _merge(attn, cm, w_oa, w_ob, gates, [],
                                     c_t, w_ada[l], b_ada_row, N_MOD_EARLY * d)
        x_cur, (w_fo,) = _mix_out(merged, w_om, x_cur, mod_late,
                                  ln1_g[l].reshape(1, -1), ln1_b[l].reshape(1, -1),
                                  [(w_ffn_out[l], 0)])

        x_cur = _ffn(x_cur, mod_late, w_fi, w_fo,
                     ln2_g[l].reshape(1, -1), ln2_b[l].reshape(1, -1))

    return x_cur.reshape(bsz, seq, d)
```

```python
import functools
import math

import jax
import jax.numpy as jnp
from jax import lax
from jax.experimental import pallas as pl
from jax.experimental.pallas import tpu as pltpu

F32 = jnp.float32
BF16 = jnp.bfloat16

D_MODEL = 2048
SEQ = 2048
CHUNK = 64
N_HEADS = 16
QK_NOPE = 128
QK_ROPE = 64
V_HEAD = 128
Q_LORA = 512
KV_LORA = 512
ROPE_THETA = 10000.0
ATTN_SCALE = (QK_NOPE + QK_ROPE) ** -0.5
CONV_K = 3
D_FF = 5632
DEPTH = 1
DEEPNORM_ALPHA = (2.0 * DEPTH) ** 0.25
LN_EPS = 1e-5
RMS_EPS = 1e-6

LANES = 128
SUBLANES = 8
HALF = QK_ROPE // 2
HEAD_PAD = N_HEADS * LANES
ATTN_TQ = 256
ATTN_LOOKAHEAD = 4
ATTN_HEADS_PER_STEP = 2
CAST_PIECE_ROWS = 32
UP_HEAD_GROUP = 4
ROW_CHUNK = 256
COL_CHUNK = 512
MXU_WIDTH = 256
Q_PRESCALE = ATTN_SCALE * math.log2(math.e)

VMEM_LIMIT = 58 * 1024 * 1024
FFN_VMEM_LIMIT = 62 * 1024 * 1024


def _params(n_axes, vmem=VMEM_LIMIT):
    return pltpu.CompilerParams(
        dimension_semantics=("arbitrary",) * n_axes, vmem_limit_bytes=vmem)


def _resident(shape, index_map):
    return pl.BlockSpec(shape, index_map, pipeline_mode=pl.Buffered(1))


def _slab_specs(w, n_steps, step_of, row0=0):
    n_rows = w.shape[0] - row0
    cols = w.shape[1]
    bf16_rows = 2 * SUBLANES
    if n_rows % (n_steps * bf16_rows) == 0:
        rb = n_rows // n_steps
        in_spec = pl.BlockSpec(
            (pl.Element(rb), pl.Element(cols)),
            lambda *g: (pl.multiple_of(row0 + step_of(*g) * rb, SUBLANES), 0))
        out_spec = pl.BlockSpec((rb, cols), lambda *g: (step_of(*g), 0))
    else:
        assert row0 == 0 and n_rows % (n_steps // 2 * bf16_rows) == 0
        rb = n_rows // (n_steps // 2)
        in_spec = out_spec = pl.BlockSpec(
            (rb, cols // 2), lambda *g: (step_of(*g) // 2, step_of(*g) % 2))
    return in_spec, out_spec, jax.ShapeDtypeStruct((n_rows, cols), BF16)


def _split_refs(refs, n_in, n_cast):
    return refs[:n_in], refs[n_in:n_in + n_cast], refs[n_in + n_cast], refs[n_in + n_cast + 1:]


def _cast_pieces(cast_src, cast_dst):
    pieces = []
    for src, dst in zip(cast_src, cast_dst, strict=True):
        rows = src.shape[0]
        n_piece = next(n for n in range(max(1, rows // CAST_PIECE_ROWS), 0, -1)
                       if rows % n == 0 and (rows // n) % (2 * SUBLANES) == 0)
        pieces += [(src, dst, slice(p * rows // n_piece, (p + 1) * rows // n_piece))
                   for p in range(n_piece)]
    return pieces


def _convert_pieces(pieces, k, n):
    for src, dst, rows in pieces[k::n]:
        dst[rows, :] = src[rows, :].astype(BF16)


def _col_chunks(n, width):
    chunks = [(c0, width) for c0 in range(0, n - width, width)]
    c0, size = n - width, width
    while size > MXU_WIDTH:
        size //= 2
        chunks.append((c0, size))
        c0 += size
    chunks.append((c0, size))
    return chunks


def _dot(a, b):
    return jnp.dot(a, b, preferred_element_type=F32)


def _dot_nt(a, b_t):
    return lax.dot_general(a, b_t, (((1,), (1,)), ((), ())), preferred_element_type=F32)


def _modulate(x, scale, shift):
    return (x * (1.0 + scale) + shift).astype(BF16)


MOD_SHIFT1, MOD_SCALE1, MOD_GATE1, MOD_SHIFT2, MOD_SCALE2, MOD_GATE2 = range(6)


def _mod_spec(mod, which):
    return pl.BlockSpec((mod.shape[0], D_MODEL), lambda *g: (0, which))


def _batch_row(ref, tile, tiles_per_seq):
    return ref[pl.ds(tile // tiles_per_seq, 1), :]


def _layer_norm(y, g, b):
    mu = jnp.mean(y, axis=-1, keepdims=True)
    yc = y - mu
    var = jnp.mean(yc * yc, axis=-1, keepdims=True)
    return yc * lax.rsqrt(var + LN_EPS) * g + b


def _rms_norm(y, g):
    return y * lax.rsqrt(jnp.mean(y * y, axis=-1, keepdims=True) + RMS_EPS) * g


def _ada_kernel(c_ref, w_ref, b_ref, o_ref):
    c = c_ref[...]
    c_act = (c * jax.nn.sigmoid(c)).astype(BF16)
    o_ref[...] = _dot(c_act, w_ref[...].astype(BF16)) + b_ref[...]


def _ada(c_pad, w_ada, b_ada, tn=1024):
    rows, d = c_pad.shape
    n = w_ada.shape[1]
    return pl.pallas_call(
        _ada_kernel,
        grid=(n // tn,),
        in_specs=[pl.BlockSpec((rows, d), lambda j: (0, 0)),
                  pl.BlockSpec((d, tn), lambda j: (0, j)),
                  pl.BlockSpec((1, tn), lambda j: (0, j))],
        out_specs=pl.BlockSpec((rows, tn), lambda j: (0, j)),
        out_shape=jax.ShapeDtypeStruct((rows, n), F32),
        compiler_params=_params(1),
        name="ada_mod",
    )(c_pad, w_ada, b_ada)


def _latent_kernel(x_ref, sh_ref, sc_ref, w_ref, gq_ref, gkv_ref,
                   qn_ref, kvn_ref, kr_ref, u_ref, wbf_ref, wkr_ref, *, tiles_per_seq):
    n_lat = Q_LORA + KV_LORA

    @pl.when(pl.program_id(0) == 0)
    def _():
        wbf_ref[...] = w_ref[:n_lat, :].astype(BF16)
        wkr_ref[...] = jnp.zeros_like(wkr_ref)
        wkr_ref[0:QK_ROPE, :] = w_ref[n_lat:n_lat + QK_ROPE, :].astype(BF16)
        wkr_ref[LANES:LANES + HALF, :] = (-w_ref[n_lat + HALF:n_lat + QK_ROPE, :]).astype(BF16)
        wkr_ref[LANES + HALF:LANES + QK_ROPE, :] = w_ref[n_lat:n_lat + HALF, :].astype(BF16)

    tile = pl.program_id(0)
    u = _modulate(x_ref[...], _batch_row(sc_ref, tile, tiles_per_seq),
                  _batch_row(sh_ref, tile, tiles_per_seq))
    u_ref[...] = u
    p = _dot_nt(u, wbf_ref[...])
    qn_ref[...] = _rms_norm(p[:, :Q_LORA], gq_ref[...]).astype(BF16)
    kvn_ref[...] = _rms_norm(p[:, Q_LORA:], gkv_ref[...]).astype(BF16)
    kr_ref[...] = _dot_nt(u, wkr_ref[...])


def _latent(x2, mod, w_t, g_q, g_kv, bm=1024):
    m, d = x2.shape
    per_seq = SEQ // bm
    n_lat = Q_LORA + KV_LORA
    return pl.pallas_call(
        functools.partial(_latent_kernel, tiles_per_seq=per_seq),
        grid=(m // bm,),
        in_specs=[pl.BlockSpec((bm, d), lambda i: (i, 0)),
                  _mod_spec(mod, MOD_SHIFT1), _mod_spec(mod, MOD_SCALE1),
                  _resident((n_lat + QK_ROPE, d), lambda i: (0, 0)),
                  pl.BlockSpec((1, Q_LORA), lambda i: (0, 0)),
                  pl.BlockSpec((1, KV_LORA), lambda i: (0, 0))],
        out_specs=[pl.BlockSpec((bm, Q_LORA), lambda i: (i, 0)),
                   pl.BlockSpec((bm, KV_LORA), lambda i: (i, 0)),
                   pl.BlockSpec((bm, 2 * LANES), lambda i: (i, 0)),
                   pl.BlockSpec((bm, d), lambda i: (i, 0))],
        out_shape=[jax.ShapeDtypeStruct((m, Q_LORA), BF16),
                   jax.ShapeDtypeStruct((m, KV_LORA), BF16),
                   jax.ShapeDtypeStruct((m, 2 * LANES), F32),
                   jax.ShapeDtypeStruct((m, d), BF16)],
        scratch_shapes=[pltpu.VMEM((n_lat, d), BF16), pltpu.VMEM((2 * LANES, d), BF16)],
        compiler_params=_params(1),
        name="latent_proj",
    )(x2, mod, mod, w_t, g_q, g_kv)


def _qkv_up_kernel(qn_ref, kvn_ref, krr_ref, pos_ref, inv_ref, wq_raw, wkv_raw,
                   qnope_ref, qrope_ref, k_ref, vt_ref, wq_ref, wk_ref, wvt_ref,
                   cos_ref, sin_ref):
    @pl.when(pl.program_id(0) == 0)
    def _():
        for h in range(N_HEADS):
            q0 = h * (QK_NOPE + QK_ROPE)
            blk = slice(h * LANES, (h + 1) * LANES)
            wq_ref[:, blk] = wq_raw[:, q0:q0 + QK_NOPE].astype(BF16)
            rope = wq_raw[:, q0 + QK_NOPE:q0 + QK_NOPE + QK_ROPE]
            packed = jnp.concatenate([rope, -rope[:, HALF:], rope[:, :HALF]], axis=1)
            wq_ref[:, HEAD_PAD + h * LANES:HEAD_PAD + (h + 1) * LANES] = packed.astype(BF16)
            k0 = h * (QK_NOPE + V_HEAD)
            wk_ref[:, blk] = wkv_raw[:, k0:k0 + QK_NOPE].astype(BF16)
            wvt_ref[blk, :] = wkv_raw[:, k0 + QK_NOPE:k0 + QK_NOPE + V_HEAD].T.astype(BF16)

    ang = pos_ref[...].astype(F32) * inv_ref[...]
    dense_lane_group = lax.broadcasted_iota(jnp.int32, ang.shape, 1) // HALF
    for name_ref, table in ((cos_ref, jnp.cos(ang)), (sin_ref, jnp.sin(ang))):
        for g in range(LANES // HALF):
            own = jnp.where(dense_lane_group == g, table, 0.0)
            spread = own
            for k in range(1, LANES // HALF):
                spread = spread + pltpu.roll(own, k * HALF, 1)
            name_ref[pl.ds(g, ang.shape[0], stride=LANES // HALF), :] = spread
    cos = cos_ref[...]
    sin = sin_ref[...]
    qn = qn_ref[...]
    kvn = kvn_ref[...]
    grp = UP_HEAD_GROUP * LANES
    lane = lax.broadcasted_iota(jnp.int32, cos.shape, 1)
    cs = jnp.where(lane < QK_ROPE, cos, sin) * Q_PRESCALE
    krr = krr_ref[...]
    k_rope = (krr[:, :LANES] * cos + krr[:, LANES:] * sin).astype(BF16)
    for g in range(N_HEADS // UP_HEAD_GROUP):
        lo, hi = g * grp, (g + 1) * grp
        qnope_ref[:, lo:hi] = (_dot(qn, wq_ref[:, lo:hi]) * Q_PRESCALE).astype(BF16)
        p = _dot(qn, wq_ref[:, HEAD_PAD + lo:HEAD_PAD + hi])
        for h in range(UP_HEAD_GROUP):
            t = p[:, h * LANES:(h + 1) * LANES] * cs
            qrope_ref[:, lo + h * LANES:lo + (h + 1) * LANES] = (
                t + pltpu.roll(t, QK_ROPE, 1)).astype(BF16)
        k_nope = _dot(kvn, wk_ref[:, lo:hi]).astype(BF16)
        for h in range(UP_HEAD_GROUP):
            k0 = (g * UP_HEAD_GROUP + h) * 2 * LANES
            k_ref[:, k0:k0 + LANES] = k_nope[:, h * LANES:(h + 1) * LANES]
            k_ref[:, k0 + LANES:k0 + 2 * LANES] = k_rope
        vt_ref[lo:hi, :] = _dot_nt(wvt_ref[lo:hi, :], kvn).astype(BF16)


def _qkv_up(qn, kvn, kr_raw, pos_dense, inv_row, w_q_b, w_kv_b, bm=512):
    m = qn.shape[0]
    row = lambda n: pl.BlockSpec((bm, n), lambda i: (i, 0))
    const = lambda a: _resident(a.shape, lambda i: (0, 0))
    return pl.pallas_call(
        _qkv_up_kernel,
        grid=(m // bm,),
        in_specs=[row(Q_LORA), row(KV_LORA), row(2 * LANES),
                  pl.BlockSpec((bm // (LANES // HALF), LANES), lambda i: (i, 0)),
                  const(inv_row), const(w_q_b), const(w_kv_b)],
        out_specs=[row(HEAD_PAD), row(HEAD_PAD), row(2 * HEAD_PAD),
                   pl.BlockSpec((HEAD_PAD, bm), lambda i: (0, i))],
        out_shape=[jax.ShapeDtypeStruct((m, HEAD_PAD), BF16)] * 2
        + [jax.ShapeDtypeStruct((m, 2 * HEAD_PAD), BF16), jax.ShapeDtypeStruct((HEAD_PAD, m), BF16)],
        scratch_shapes=[pltpu.VMEM((Q_LORA, 2 * HEAD_PAD), BF16),
                        pltpu.VMEM((KV_LORA, HEAD_PAD), BF16),
                        pltpu.VMEM((HEAD_PAD, KV_LORA), BF16),
                        pltpu.VMEM((bm, LANES), F32), pltpu.VMEM((bm, LANES), F32)],
        compiler_params=_params(1),
        name="qkv_up",
    )(qn, kvn, kr_raw, pos_dense, inv_row, w_q_b, w_kv_b)


def _conv_kernel(u_ref, wb_ref, wc_ref, wx_ref, wconv_ref, o_ref, halo_ref, *, tiles_per_seq):
    i = pl.program_id(0)
    j = pl.program_id(1)
    u = u_ref[...]
    bm, tn = o_ref.shape

    @pl.when(i % tiles_per_seq == 0)
    def _():
        halo_ref[j] = jnp.zeros(halo_ref.shape[1:], F32)

    for c0, width in _col_chunks(tn, MXU_WIDTH):
        cols = slice(c0, c0 + width)
        rows = lax.broadcasted_iota(jnp.int32, (bm, width), 0)
        cb = _dot_nt(u, wb_ref[cols, :])
        z = _dot_nt(u, wc_ref[cols, :]) * _dot_nt(u, wx_ref[cols, :])
        prev = halo_ref[j, :, cols]
        z1 = jnp.where(rows == 0, prev[SUBLANES - 1:SUBLANES, :], pltpu.roll(z, 1, 0))
        z2 = jnp.where(rows == 0, prev[SUBLANES - 2:SUBLANES - 1, :],
                       jnp.where(rows == 1, prev[SUBLANES - 1:SUBLANES, :], pltpu.roll(z, 2, 0)))
        halo_ref[j, :, cols] = z[bm - SUBLANES:, :]
        w = wconv_ref[:, cols]
        o_ref[:, cols] = (cb * (w[0:1, :] * z2 + w[1:2, :] * z1 + w[2:3, :] * z)).astype(BF16)


def _conv_branch(u, w_rows, w_conv, bm=1024, tn=1024):
    m, d = u.shape
    n = w_conv.shape[1]
    per_seq = SEQ // bm
    w_spec = lambda piece: pl.BlockSpec((tn, d), lambda i, j: (piece * (n // tn) + j, 0))
    return pl.pallas_call(
        functools.partial(_conv_kernel, tiles_per_seq=per_seq),
        grid=(m // bm, n // tn),
        in_specs=[pl.BlockSpec((bm, d), lambda i, j: (i, 0)),
                  w_spec(0), w_spec(1), w_spec(2),
                  pl.BlockSpec((CONV_K, tn), lambda i, j: (0, j))],
        out_specs=pl.BlockSpec((bm, tn), lambda i, j: (i, j)),
        out_shape=jax.ShapeDtypeStruct((m, n), BF16),
        scratch_shapes=[pltpu.VMEM((n // tn, SUBLANES, tn), F32)],
        compiler_params=_params(2),
        name="conv_branch",
    )(u, w_rows, w_rows, w_rows, w_conv)


def _gate_kernel(*refs, n_cast):
    (u_ref, wa_ref, wb_ref), cast_src, o_ref, cast_dst = _split_refs(refs, 3, n_cast)
    pieces = _cast_pieces(cast_src, cast_dst)
    u = u_ref[...]
    n = wa_ref.shape[0]
    passes = ([(wa_ref, 0, c0, 2 * COL_CHUNK) for c0 in range(0, n, 2 * COL_CHUNK)]
              + [(wb_ref, n, c0, width) for c0, width in _col_chunks(n, 2 * COL_CHUNK)])
    for k, (w_ref, out0, c0, width) in enumerate(passes):
        o_ref[:, out0 + c0:out0 + c0 + width] = jax.nn.sigmoid(
            _dot_nt(u, w_ref[c0:c0 + width, :])).astype(BF16)
        _convert_pieces(pieces, k, len(passes))


def _gates(u, w_rows, first_block, cast_jobs, bm=1024):
    m, d = u.shape
    specs = [_slab_specs(w, m // bm, lambda i: i, row0) for w, row0 in cast_jobs]
    outs = pl.pallas_call(
        functools.partial(_gate_kernel, n_cast=len(cast_jobs)),
        grid=(m // bm,),
        in_specs=[pl.BlockSpec((bm, d), lambda i: (i, 0)),
                  _resident((d, d), lambda i: (first_block, 0)),
                  _resident((d, d), lambda i: (first_block + 1, 0))] + [s[0] for s in specs],
        out_specs=[pl.BlockSpec((bm, 2 * d), lambda i: (i, 0))] + [s[1] for s in specs],
        out_shape=[jax.ShapeDtypeStruct((m, 2 * d), BF16)] + [s[2] for s in specs],
        compiler_params=_params(1),
        name="merge_gates",
    )(u, w_rows, w_rows, *[w for w, _ in cast_jobs])
    return outs[0], outs[1:]


def _attn_kernel(*refs, n_cast):
    (qn_ref, qr_ref, k_ref, vt_ref), cast_src, o_ref, cast_dst = _split_refs(refs, 4, n_cast)
    pieces = _cast_pieces(cast_src, cast_dst)
    key = lax.broadcasted_iota(jnp.int32, (ATTN_TQ, ATTN_TQ), 0)
    qry = lax.broadcasted_iota(jnp.int32, (ATTN_TQ, ATTN_TQ), 1)
    allowed = (key // CHUNK) <= (qry // CHUNK)
    n_blk = SEQ // ATTN_TQ

    def scores(task):
        a, i = task
        q0 = i * ATTN_TQ
        L = q0 + ATTN_TQ
        lanes = slice(a * LANES, (a + 1) * LANES)
        q = jnp.concatenate([qn_ref[q0:L, lanes], qr_ref[q0:L, lanes]], axis=1)
        return _dot_nt(k_ref[0:L, 2 * a * LANES:2 * (a + 1) * LANES], q)

    tasks = [(a, i) for i in range(n_blk - 1, -1, -1) for a in range(ATTN_HEADS_PER_STEP)]
    pending = [scores(t) for t in tasks[:ATTN_LOOKAHEAD]]
    for pos, (a, i) in enumerate(tasks):
        q0 = i * ATTN_TQ
        L = q0 + ATTN_TQ
        s_t = pending.pop(0)
        if pos + ATTN_LOOKAHEAD < len(tasks):
            pending.append(scores(tasks[pos + ATTN_LOOKAHEAD]))
        heads = slice(a * V_HEAD, (a + 1) * V_HEAD)
        s_diag = jnp.where(allowed, s_t[q0:, :], -1e30)
        m_col = jnp.max(s_diag, axis=0, keepdims=True)
        if i > 0:
            s_full = s_t[:q0, :]
            m_col = jnp.maximum(m_col, jnp.max(s_full, axis=0, keepdims=True))
        e_diag = jnp.exp2(s_diag - m_col)
        denom = jnp.sum(e_diag, axis=0, keepdims=True)
        o_t = _dot(vt_ref[heads, q0:L], e_diag.astype(BF16))
        if i > 0:
            e_full = jnp.exp2(s_full - m_col)
            denom = denom + jnp.sum(e_full, axis=0, keepdims=True)
            o_t = o_t + _dot(vt_ref[heads, :q0], e_full.astype(BF16))
        o_ref[q0:L, a * LANES:(a + 1) * LANES] = (o_t / denom).T.astype(BF16)
        _convert_pieces(pieces, pos, len(tasks))


def _attention(q_nope, q_rope, k_all, v_t, cast_jobs):
    m = q_nope.shape[0]
    hps = ATTN_HEADS_PER_STEP
    grid = (m // SEQ, N_HEADS // hps)
    head = pl.BlockSpec((SEQ, hps * LANES), lambda b, h: (b, h))
    specs = [_slab_specs(w, grid[0] * grid[1], lambda b, h: b * grid[1] + h, row0)
             for w, row0 in cast_jobs]
    outs = pl.pallas_call(
        functools.partial(_attn_kernel, n_cast=len(cast_jobs)),
        grid=grid,
        in_specs=[head, head, pl.BlockSpec((SEQ, hps * 2 * LANES), lambda b, h: (b, h)),
                  pl.BlockSpec((hps * V_HEAD, SEQ), lambda b, h: (h, b))] + [s[0] for s in specs],
        out_specs=[head] + [s[1] for s in specs],
        out_shape=[jax.ShapeDtypeStruct((m, HEAD_PAD), BF16)] + [s[2] for s in specs],
        compiler_params=_params(2),
        name="mla_attention",
    )(q_nope, q_rope, k_all, v_t, *[w for w, _ in cast_jobs])
    return outs[0], outs[1:]


def _merge_kernel(*refs, n_cast):
    (a_ref, c_ref, wa_ref, wb_ref, g_ref), cast_src, o_ref, cast_dst = _split_refs(refs, 5, n_cast)
    pieces = _cast_pieces(cast_src, cast_dst)
    a = a_ref[...]
    cm = c_ref[...]
    n = o_ref.shape[1]
    chunks = _col_chunks(n, MXU_WIDTH)
    for k, (c0, width) in enumerate(chunks):
        cols = slice(c0, c0 + width)
        y_a = _dot(a, wa_ref[:, cols])
        y_b = _dot(cm, wb_ref[:, cols])
        o_ref[:, cols] = (g_ref[:, cols].astype(F32) * y_a
                          + g_ref[:, n + c0:n + c0 + width].astype(F32) * y_b).astype(BF16)
        _convert_pieces(pieces, k, len(chunks))


def _merge(attn, cm, w_oa, w_ob, gates, cast_jobs, bm=512):
    m, d = attn.shape
    n = w_oa.shape[1]
    lhs = pl.BlockSpec((bm, d), lambda i: (i, 0))
    specs = [_slab_specs(w, m // bm, lambda i: i, row0) for w, row0 in cast_jobs]
    outs = pl.pallas_call(
        functools.partial(_merge_kernel, n_cast=len(cast_jobs)),
        grid=(m // bm,),
        in_specs=[lhs, lhs, _resident((d, n), lambda i: (0, 0)), _resident((d, n), lambda i: (0, 0)),
                  pl.BlockSpec((bm, 2 * n), lambda i: (i, 0))] + [s[0] for s in specs],
        out_specs=[pl.BlockSpec((bm, n), lambda i: (i, 0))] + [s[1] for s in specs],
        out_shape=[jax.ShapeDtypeStruct((m, n), BF16)] + [s[2] for s in specs],
        compiler_params=_params(1),
        name="branch_merge",
    )(attn, cm, w_oa, w_ob, gates, *[w for w, _ in cast_jobs])
    return outs[0], outs[1:]


def _mix_out_kernel(mg_ref, w_ref, x_ref, g1_ref, lg_ref, lb_ref, o_ref, *, tiles_per_seq):
    gate1 = _batch_row(g1_ref, pl.program_id(0), tiles_per_seq)
    n_chunks = mg_ref.shape[0] // ROW_CHUNK
    mix = _dot(mg_ref[0:ROW_CHUNK, :], w_ref[...])
    for r in range(n_chunks):
        rows = slice(r * ROW_CHUNK, (r + 1) * ROW_CHUNK)
        cur = mix
        if r + 1 < n_chunks:
            mix = _dot(mg_ref[(r + 1) * ROW_CHUNK:(r + 2) * ROW_CHUNK, :], w_ref[...])
        y = DEEPNORM_ALPHA * x_ref[rows, :] + gate1 * cur
        o_ref[rows, :] = _layer_norm(y, lg_ref[...], lb_ref[...])


def _mix_out(merged, w_o, x2, mod, ln_g, ln_b, bm=512):
    m, d = x2.shape
    per_seq = SEQ // bm
    row = pl.BlockSpec((bm, d), lambda i: (i, 0))
    vec = pl.BlockSpec((1, d), lambda i: (0, 0))
    return pl.pallas_call(
        functools.partial(_mix_out_kernel, tiles_per_seq=per_seq),
        grid=(m // bm,),
        in_specs=[row, _resident((d, d), lambda i: (0, 0)), row,
                  _mod_spec(mod, MOD_GATE1), vec, vec],
        out_specs=row,
        out_shape=jax.ShapeDtypeStruct((m, d), F32),
        compiler_params=_params(1),
        name="mix_out_ln",
    )(merged, w_o, x2, mod, ln_g, ln_b)


def _ffn_kernel(x_ref, sc_ref, sh_ref, g2_ref, wg_ref, wu_ref, wo_ref, lg_ref, lb_ref, o_ref,
                u_ref, *, tiles_per_seq):
    acc_ref = o_ref
    tile = pl.program_id(0)
    j = pl.program_id(1)

    @pl.when(j == 0)
    def _():
        u_ref[...] = _modulate(x_ref[...], _batch_row(sc_ref, tile, tiles_per_seq),
                               _batch_row(sh_ref, tile, tiles_per_seq))
        acc_ref[...] = jnp.zeros_like(acc_ref)

    u = u_ref[...]
    tf = wg_ref.shape[1]
    half = tf // 2
    hs = []
    for c0 in (0, half):
        hg = _dot(u, wg_ref[:, c0:c0 + half])
        hs.append((hg * jax.nn.sigmoid(hg) * _dot(u, wu_ref[:, c0:c0 + half])).astype(BF16))
    for c0 in range(0, acc_ref.shape[1], COL_CHUNK):
        cols = slice(c0, c0 + COL_CHUNK)
        acc_ref[:, cols] += _dot(hs[0], wo_ref[:half, cols]) + _dot(hs[1], wo_ref[half:, cols])

    @pl.when(j == pl.num_programs(1) - 1)
    def _():
        y = (DEEPNORM_ALPHA * x_ref[...]
             + _batch_row(g2_ref, tile, tiles_per_seq) * acc_ref[...])
        o_ref[...] = _layer_norm(y, lg_ref[...], lb_ref[...])


def _ffn(x1, mod, w_in, w_out, ln_g, ln_b, bm=1024, tf=512):
    m, d = x1.shape
    nf = D_FF // tf
    per_seq = SEQ // bm
    row = pl.BlockSpec((bm, d), lambda i, j: (i, 0))
    vec = pl.BlockSpec((1, d), lambda i, j: (0, 0))
    return pl.pallas_call(
        functools.partial(_ffn_kernel, tiles_per_seq=per_seq),
        grid=(m // bm, nf),
        in_specs=[row, _mod_spec(mod, MOD_SCALE2), _mod_spec(mod, MOD_SHIFT2),
                  _mod_spec(mod, MOD_GATE2),
                  pl.BlockSpec((d, tf), lambda i, j: (0, j)),
                  pl.BlockSpec((d, tf), lambda i, j: (0, j + nf)),
                  pl.BlockSpec((tf, d), lambda i, j: (j, 0)), vec, vec],
        out_specs=row,
        out_shape=jax.ShapeDtypeStruct((m, d), F32),
        scratch_shapes=[pltpu.VMEM((bm, d), BF16)],
        compiler_params=_params(2, vmem=FFN_VMEM_LIMIT),
        name="swiglu_ffn_ln",
    )(x1, mod, mod, mod, w_in, w_in, w_out, ln_g, ln_b)


def kernel(x, c, positions, w_ada, b_ada, w_in, g_q_a, w_q_b, g_kv_a, w_kv_b, w_o_a,
           w_conv, w_o_b, w_o, ln1_g, ln1_b, w_ffn_in, w_ffn_out, ln2_g, ln2_b):
    bsz, seq, d = x.shape
    assert (seq, d) == (SEQ, D_MODEL) and w_ada.shape[0] == DEPTH
    m = bsz * seq
    x_cur = x.reshape(m, d)

    inv_freq = 1.0 / (ROPE_THETA ** (jnp.arange(0, QK_ROPE, 2, dtype=F32) / QK_ROPE))
    inv_row = jnp.tile(inv_freq, LANES // HALF).reshape(1, LANES)
    groups = LANES // HALF
    pos_dense = jnp.repeat(positions.reshape(m // groups, groups), HALF, axis=1)
    c_pad = jnp.pad(c, ((0, SUBLANES - bsz), (0, 0)))

    for l in range(DEPTH):
        mod = _ada(c_pad, w_ada[l], b_ada[l].reshape(1, -1))

        o_cb = Q_LORA + KV_LORA + QK_ROPE
        w_tf = jnp.swapaxes(w_in[l], 0, 1)

        qn, kvn, kr_raw, u = _latent(x_cur, mod, w_tf,
                                     g_q_a[l].reshape(1, -1), g_kv_a[l].reshape(1, -1))
        q_nope, q_rope, k_all, v_t = _qkv_up(qn, kvn, kr_raw, pos_dense, inv_row,
                                             w_q_b[l], w_kv_b[l])
        attn, (w_cg, w_oa, w_ob, w_om, w_fi) = _attention(
            q_nope, q_rope, k_all, v_t,
            [(w_tf, o_cb), (w_o_a[l], 0), (w_o_b[l], 0), (w_o[l], 0), (w_ffn_in[l], 0)])
        cm = _conv_branch(u, w_cg, w_conv[l])
        gates, _ = _gates(u, w_cg, 3, [])
        merged, (w_fo,) = _merge(attn, cm, w_oa, w_ob, gates, [(w_ffn_out[l], 0)])
        x_cur = _mix_out(merged, w_om, x_cur, mod,
                         ln1_g[l].reshape(1, -1), ln1_b[l].reshape(1, -1))

        x_cur = _ffn(x_cur, mod, w_fi, w_fo,
                     ln2_g[l].reshape(1, -1), ln2_b[l].reshape(1, -1))

    return x_cur.reshape(bsz, seq, d)
```

```python
import functools
import math

import jax
import jax.numpy as jnp
from jax import lax
from jax.experimental import pallas as pl
from jax.experimental.pallas import tpu as pltpu

F32 = jnp.float32
BF16 = jnp.bfloat16

D_MODEL = 2048
SEQ = 2048
CHUNK = 64
N_HEADS = 16
QK_NOPE = 128
QK_ROPE = 64
V_HEAD = 128
Q_LORA = 512
KV_LORA = 512
ROPE_THETA = 10000.0
ATTN_SCALE = (QK_NOPE + QK_ROPE) ** -0.5
CONV_K = 3
D_FF = 5632
DEPTH = 1
DEEPNORM_ALPHA = (2.0 * DEPTH) ** 0.25
LN_EPS = 1e-5
RMS_EPS = 1e-6

LANES = 128
SUBLANES = 8
HALF = QK_ROPE // 2
HEAD_PAD = N_HEADS * LANES
ATTN_TQ = 256
ATTN_LOOKAHEAD = 4
ATTN_HEADS_PER_STEP = 2
CAST_PIECE_ROWS = 32
UP_HEAD_GROUP = 4
ROW_CHUNK = 256
COL_CHUNK = 512
MXU_WIDTH = 256
Q_PRESCALE = ATTN_SCALE * math.log2(math.e)

VMEM_LIMIT = 58 * 1024 * 1024
FFN_VMEM_LIMIT = 62 * 1024 * 1024


def _params(n_axes, vmem=VMEM_LIMIT):
    return pltpu.CompilerParams(
        dimension_semantics=("arbitrary",) * n_axes, vmem_limit_bytes=vmem)


def _resident(shape, index_map):
    return pl.BlockSpec(shape, index_map, pipeline_mode=pl.Buffered(1))


def _slab_specs(w, n_steps, step_of, row0=0):
    n_rows = w.shape[0] - row0
    cols = w.shape[1]
    bf16_rows = 2 * SUBLANES
    if n_rows % (n_steps * bf16_rows) == 0:
        rb = n_rows // n_steps
        in_spec = pl.BlockSpec(
            (pl.Element(rb), pl.Element(cols)),
            lambda *g: (pl.multiple_of(row0 + step_of(*g) * rb, SUBLANES), 0))
        out_spec = pl.BlockSpec((rb, cols), lambda *g: (step_of(*g), 0))
    else:
        assert row0 == 0 and n_rows % (n_steps // 2 * bf16_rows) == 0
        rb = n_rows // (n_steps // 2)
        in_spec = out_spec = pl.BlockSpec(
            (rb, cols // 2), lambda *g: (step_of(*g) // 2, step_of(*g) % 2))
    return in_spec, out_spec, jax.ShapeDtypeStruct((n_rows, cols), BF16)


def _split_refs(refs, n_in, n_cast):
    return refs[:n_in], refs[n_in:n_in + n_cast], refs[n_in + n_cast], refs[n_in + n_cast + 1:]


def _cast_pieces(cast_src, cast_dst):
    pieces = []
    for src, dst in zip(cast_src, cast_dst, strict=True):
        rows = src.shape[0]
        n_piece = next(n for n in range(max(1, rows // CAST_PIECE_ROWS), 0, -1)
                       if rows % n == 0 and (rows // n) % (2 * SUBLANES) == 0)
        pieces += [(src, dst, slice(p * rows // n_piece, (p + 1) * rows // n_piece))
                   for p in range(n_piece)]
    return pieces


def _convert_pieces(pieces, k, n):
    for src, dst, rows in pieces[k::n]:
        dst[rows, :] = src[rows, :].astype(BF16)


def _col_chunks(n, width):
    chunks = [(c0, width) for c0 in range(0, n - width, width)]
    c0, size = n - width, width
    while size > MXU_WIDTH:
        size //= 2
        chunks.append((c0, size))
        c0 += size
    chunks.append((c0, size))
    return chunks


def _dot(a, b):
    return jnp.dot(a, b, preferred_element_type=F32)


def _dot_nt(a, b_t):
    return lax.dot_general(a, b_t, (((1,), (1,)), ((), ())), preferred_element_type=F32)


def _modulate(x, scale, shift):
    return (x * (1.0 + scale) + shift).astype(BF16)


MOD_SHIFT1, MOD_SCALE1, MOD_GATE1, MOD_SHIFT2, MOD_SCALE2, MOD_GATE2 = range(6)


def _mod_spec(mod, which):
    return pl.BlockSpec((mod.shape[0], D_MODEL), lambda *g: (0, which))


def _batch_row(ref, tile, tiles_per_seq):
    return ref[pl.ds(tile // tiles_per_seq, 1), :]


def _deepnorm_ln(x, gate, branch, g, b):
    z = x + (gate * (1.0 / DEEPNORM_ALPHA)) * branch
    mu = jnp.mean(z, axis=-1, keepdims=True)
    zc = z - mu
    var = jnp.mean(zc * zc, axis=-1, keepdims=True)
    return zc * lax.rsqrt(var + LN_EPS / DEEPNORM_ALPHA ** 2) * g + b


def _rms_norm(y, g):
    return y * lax.rsqrt(jnp.mean(y * y, axis=-1, keepdims=True) + RMS_EPS) * g


def _ada_kernel(c_ref, w_ref, b_ref, o_ref):
    c = c_ref[...]
    c_act = (c * jax.nn.sigmoid(c)).astype(BF16)
    o_ref[...] = _dot(c_act, w_ref[...].astype(BF16)) + b_ref[...]


def _ada(c_pad, w_ada, b_ada, tn=1024):
    rows, d = c_pad.shape
    n = w_ada.shape[1]
    return pl.pallas_call(
        _ada_kernel,
        grid=(n // tn,),
        in_specs=[pl.BlockSpec((rows, d), lambda j: (0, 0)),
                  pl.BlockSpec((d, tn), lambda j: (0, j)),
                  pl.BlockSpec((1, tn), lambda j: (0, j))],
        out_specs=pl.BlockSpec((rows, tn), lambda j: (0, j)),
        out_shape=jax.ShapeDtypeStruct((rows, n), F32),
        compiler_params=_params(1),
        name="ada_mod",
    )(c_pad, w_ada, b_ada)


def _latent_kernel(x_ref, sh_ref, sc_ref, w_ref, gq_ref, gkv_ref,
                   qn_ref, kvn_ref, kr_ref, u_ref, wbf_ref, wkr_ref, *, tiles_per_seq):
    n_lat = Q_LORA + KV_LORA

    @pl.when(pl.program_id(0) == 0)
    def _():
        wbf_ref[...] = w_ref[:n_lat, :].astype(BF16)
        wkr_ref[...] = jnp.zeros_like(wkr_ref)
        wkr_ref[0:QK_ROPE, :] = w_ref[n_lat:n_lat + QK_ROPE, :].astype(BF16)
        wkr_ref[LANES:LANES + HALF, :] = (-w_ref[n_lat + HALF:n_lat + QK_ROPE, :]).astype(BF16)
        wkr_ref[LANES + HALF:LANES + QK_ROPE, :] = w_ref[n_lat:n_lat + HALF, :].astype(BF16)

    tile = pl.program_id(0)
    u = _modulate(x_ref[...], _batch_row(sc_ref, tile, tiles_per_seq),
                  _batch_row(sh_ref, tile, tiles_per_seq))
    u_ref[...] = u
    p = _dot_nt(u, wbf_ref[...])
    qn_ref[...] = _rms_norm(p[:, :Q_LORA], gq_ref[...]).astype(BF16)
    kvn_ref[...] = _rms_norm(p[:, Q_LORA:], gkv_ref[...]).astype(BF16)
    kr_ref[...] = _dot_nt(u, wkr_ref[...])


def _latent(x2, mod, w_t, g_q, g_kv, bm=1024):
    m, d = x2.shape
    per_seq = SEQ // bm
    n_lat = Q_LORA + KV_LORA
    return pl.pallas_call(
        functools.partial(_latent_kernel, tiles_per_seq=per_seq),
        grid=(m // bm,),
        in_specs=[pl.BlockSpec((bm, d), lambda i: (i, 0)),
                  _mod_spec(mod, MOD_SHIFT1), _mod_spec(mod, MOD_SCALE1),
                  _resident((n_lat + QK_ROPE, d), lambda i: (0, 0)),
                  pl.BlockSpec((1, Q_LORA), lambda i: (0, 0)),
                  pl.BlockSpec((1, KV_LORA), lambda i: (0, 0))],
        out_specs=[pl.BlockSpec((bm, Q_LORA), lambda i: (i, 0)),
                   pl.BlockSpec((bm, KV_LORA), lambda i: (i, 0)),
                   pl.BlockSpec((bm, 2 * LANES), lambda i: (i, 0)),
                   pl.BlockSpec((bm, d), lambda i: (i, 0))],
        out_shape=[jax.ShapeDtypeStruct((m, Q_LORA), BF16),
                   jax.ShapeDtypeStruct((m, KV_LORA), BF16),
                   jax.ShapeDtypeStruct((m, 2 * LANES), F32),
                   jax.ShapeDtypeStruct((m, d), BF16)],
        scratch_shapes=[pltpu.VMEM((n_lat, d), BF16), pltpu.VMEM((2 * LANES, d), BF16)],
        compiler_params=_params(1),
        name="latent_proj",
    )(x2, mod, mod, w_t, g_q, g_kv)


def _qkv_up_kernel(qn_ref, kvn_ref, krr_ref, pos_ref, inv_ref, wq_raw, wkv_raw,
                   qnope_ref, qrope_ref, knope_ref, vt_ref, kr_ref, wq_ref, wk_ref, wvt_ref,
                   cos_ref, sin_ref):
    @pl.when(pl.program_id(0) == 0)
    def _():
        for h in range(N_HEADS):
            q0 = h * (QK_NOPE + QK_ROPE)
            blk = slice(h * LANES, (h + 1) * LANES)
            wq_ref[:, blk] = wq_raw[:, q0:q0 + QK_NOPE].astype(BF16)
            rope = wq_raw[:, q0 + QK_NOPE:q0 + QK_NOPE + QK_ROPE]
            packed = jnp.concatenate([rope, -rope[:, HALF:], rope[:, :HALF]], axis=1)
            wq_ref[:, HEAD_PAD + h * LANES:HEAD_PAD + (h + 1) * LANES] = packed.astype(BF16)
            k0 = h * (QK_NOPE + V_HEAD)
            wk_ref[:, blk] = wkv_raw[:, k0:k0 + QK_NOPE].astype(BF16)
            wvt_ref[blk, :] = wkv_raw[:, k0 + QK_NOPE:k0 + QK_NOPE + V_HEAD].T.astype(BF16)

    ang = pos_ref[...].astype(F32) * inv_ref[...]
    dense_lane_group = lax.broadcasted_iota(jnp.int32, ang.shape, 1) // HALF
    for name_ref, table in ((cos_ref, jnp.cos(ang)), (sin_ref, jnp.sin(ang))):
        for g in range(LANES // HALF):
            own = jnp.where(dense_lane_group == g, table, 0.0)
            spread = own
            for k in range(1, LANES // HALF):
                spread = spread + pltpu.roll(own, k * HALF, 1)
            name_ref[pl.ds(g, ang.shape[0], stride=LANES // HALF), :] = spread
    cos = cos_ref[...]
    sin = sin_ref[...]
    qn = qn_ref[...]
    kvn = kvn_ref[...]
    grp = UP_HEAD_GROUP * LANES
    lane = lax.broadcasted_iota(jnp.int32, cos.shape, 1)
    cs = jnp.where(lane < QK_ROPE, cos, sin) * Q_PRESCALE
    for g in range(N_HEADS // UP_HEAD_GROUP):
        lo, hi = g * grp, (g + 1) * grp
        qnope_ref[:, lo:hi] = (_dot(qn, wq_ref[:, lo:hi]) * Q_PRESCALE).astype(BF16)
        p = _dot(qn, wq_ref[:, HEAD_PAD + lo:HEAD_PAD + hi])
        for h in range(UP_HEAD_GROUP):
            t = p[:, h * LANES:(h + 1) * LANES] * cs
            qrope_ref[:, lo + h * LANES:lo + (h + 1) * LANES] = (
                t + pltpu.roll(t, QK_ROPE, 1)).astype(BF16)
        knope_ref[:, lo:hi] = _dot(kvn, wk_ref[:, lo:hi]).astype(BF16)
        vt_ref[lo:hi, :] = _dot_nt(wvt_ref[lo:hi, :], kvn).astype(BF16)
    krr = krr_ref[...]
    kr_ref[...] = (krr[:, :LANES] * cos + krr[:, LANES:] * sin).astype(BF16)


def _qkv_up(qn, kvn, kr_raw, pos_dense, inv_row, w_q_b, w_kv_b, bm=512):
    m = qn.shape[0]
    row = lambda n: pl.BlockSpec((bm, n), lambda i: (i, 0))
    const = lambda a: _resident(a.shape, lambda i: (0, 0))
    return pl.pallas_call(
        _qkv_up_kernel,
        grid=(m // bm,),
        in_specs=[row(Q_LORA), row(KV_LORA), row(2 * LANES),
                  pl.BlockSpec((bm // (LANES // HALF), LANES), lambda i: (i, 0)),
                  const(inv_row), const(w_q_b), const(w_kv_b)],
        out_specs=[row(HEAD_PAD), row(HEAD_PAD), row(HEAD_PAD),
                   pl.BlockSpec((HEAD_PAD, bm), lambda i: (0, i)), row(LANES)],
        out_shape=[jax.ShapeDtypeStruct((m, HEAD_PAD), BF16)] * 3
        + [jax.ShapeDtypeStruct((HEAD_PAD, m), BF16), jax.ShapeDtypeStruct((m, LANES), BF16)],
        scratch_shapes=[pltpu.VMEM((Q_LORA, 2 * HEAD_PAD), BF16),
                        pltpu.VMEM((KV_LORA, HEAD_PAD), BF16),
                        pltpu.VMEM((HEAD_PAD, KV_LORA), BF16),
                        pltpu.VMEM((bm, LANES), F32), pltpu.VMEM((bm, LANES), F32)],
        compiler_params=_params(1),
        name="qkv_up",
    )(qn, kvn, kr_raw, pos_dense, inv_row, w_q_b, w_kv_b)


def _conv_kernel(u_ref, wb_ref, wc_ref, wx_ref, wconv_ref, o_ref, halo_ref, *, tiles_per_seq):
    i = pl.program_id(0)
    j = pl.program_id(1)
    u = u_ref[...]
    bm, tn = o_ref.shape

    @pl.when(i % tiles_per_seq == 0)
    def _():
        halo_ref[j] = jnp.zeros(halo_ref.shape[1:], F32)

    for c0, width in _col_chunks(tn, MXU_WIDTH):
        cols = slice(c0, c0 + width)
        rows = lax.broadcasted_iota(jnp.int32, (bm, width), 0)
        cb = _dot_nt(u, wb_ref[cols, :])
        z = _dot_nt(u, wc_ref[cols, :]) * _dot_nt(u, wx_ref[cols, :])
        prev = halo_ref[j, :, cols]
        z1 = jnp.where(rows == 0, prev[SUBLANES - 1:SUBLANES, :], pltpu.roll(z, 1, 0))
        z2 = jnp.where(rows == 0, prev[SUBLANES - 2:SUBLANES - 1, :],
                       jnp.where(rows == 1, prev[SUBLANES - 1:SUBLANES, :], pltpu.roll(z, 2, 0)))
        halo_ref[j, :, cols] = z[bm - SUBLANES:, :]
        w = wconv_ref[:, cols]
        o_ref[:, cols] = (cb * (w[0:1, :] * z2 + w[1:2, :] * z1 + w[2:3, :] * z)).astype(BF16)


def _conv_branch(u, w_rows, w_conv, bm=1024, tn=1024):
    m, d = u.shape
    n = w_conv.shape[1]
    per_seq = SEQ // bm
    w_spec = lambda piece: pl.BlockSpec((tn, d), lambda i, j: (piece * (n // tn) + j, 0))
    return pl.pallas_call(
        functools.partial(_conv_kernel, tiles_per_seq=per_seq),
        grid=(m // bm, n // tn),
        in_specs=[pl.BlockSpec((bm, d), lambda i, j: (i, 0)),
                  w_spec(0), w_spec(1), w_spec(2),
                  pl.BlockSpec((CONV_K, tn), lambda i, j: (0, j))],
        out_specs=pl.BlockSpec((bm, tn), lambda i, j: (i, j)),
        out_shape=jax.ShapeDtypeStruct((m, n), BF16),
        scratch_shapes=[pltpu.VMEM((n // tn, SUBLANES, tn), F32)],
        compiler_params=_params(2),
        name="conv_branch",
    )(u, w_rows, w_rows, w_rows, w_conv)


def _gate_kernel(*refs, n_cast):
    (u_ref, wa_ref, wb_ref), cast_src, o_ref, cast_dst = _split_refs(refs, 3, n_cast)
    pieces = _cast_pieces(cast_src, cast_dst)
    u = u_ref[...]
    n = wa_ref.shape[0]
    passes = ([(wa_ref, 0, c0, 2 * COL_CHUNK) for c0 in range(0, n, 2 * COL_CHUNK)]
              + [(wb_ref, n, c0, width) for c0, width in _col_chunks(n, 2 * COL_CHUNK)])
    for k, (w_ref, out0, c0, width) in enumerate(passes):
        o_ref[:, out0 + c0:out0 + c0 + width] = jax.nn.sigmoid(
            _dot_nt(u, w_ref[c0:c0 + width, :])).astype(BF16)
        _convert_pieces(pieces, k, len(passes))


def _gates(u, w_rows, first_block, cast_jobs, bm=1024):
    m, d = u.shape
    specs = [_slab_specs(w, m // bm, lambda i: i, row0) for w, row0 in cast_jobs]
    outs = pl.pallas_call(
        functools.partial(_gate_kernel, n_cast=len(cast_jobs)),
        grid=(m // bm,),
        in_specs=[pl.BlockSpec((bm, d), lambda i: (i, 0)),
                  _resident((d, d), lambda i: (first_block, 0)),
                  _resident((d, d), lambda i: (first_block + 1, 0))] + [s[0] for s in specs],
        out_specs=[pl.BlockSpec((bm, 2 * d), lambda i: (i, 0))] + [s[1] for s in specs],
        out_shape=[jax.ShapeDtypeStruct((m, 2 * d), BF16)] + [s[2] for s in specs],
        compiler_params=_params(1),
        name="merge_gates",
    )(u, w_rows, w_rows, *[w for w, _ in cast_jobs])
    return outs[0], outs[1:]


def _attn_kernel(*refs, n_cast):
    (qn_ref, qr_ref, kn_ref, kr_ref, vt_ref), cast_src, o_ref, cast_dst = _split_refs(refs, 5, n_cast)
    pieces = _cast_pieces(cast_src, cast_dst)
    kr = kr_ref[...]
    ks = [jnp.concatenate([kn_ref[:, a * LANES:(a + 1) * LANES], kr], axis=1)
          for a in range(ATTN_HEADS_PER_STEP)]
    key = lax.broadcasted_iota(jnp.int32, (ATTN_TQ, ATTN_TQ), 0)
    qry = lax.broadcasted_iota(jnp.int32, (ATTN_TQ, ATTN_TQ), 1)
    allowed = (key // CHUNK) <= (qry // CHUNK)
    n_blk = SEQ // ATTN_TQ

    def scores(task):
        a, i = task
        q0 = i * ATTN_TQ
        L = q0 + ATTN_TQ
        lanes = slice(a * LANES, (a + 1) * LANES)
        q = jnp.concatenate([qn_ref[q0:L, lanes], qr_ref[q0:L, lanes]], axis=1)
        return _dot_nt(ks[a][:L], q)

    tasks = [(a, i) for i in range(n_blk - 1, -1, -1) for a in range(ATTN_HEADS_PER_STEP)]
    pending = [scores(t) for t in tasks[:ATTN_LOOKAHEAD]]
    for pos, (a, i) in enumerate(tasks):
        q0 = i * ATTN_TQ
        L = q0 + ATTN_TQ
        s_t = pending.pop(0)
        if pos + ATTN_LOOKAHEAD < len(tasks):
            pending.append(scores(tasks[pos + ATTN_LOOKAHEAD]))
        heads = slice(a * V_HEAD, (a + 1) * V_HEAD)
        s_diag = jnp.where(allowed, s_t[q0:, :], -1e30)
        m_col = jnp.max(s_diag, axis=0, keepdims=True)
        if i > 0:
            s_full = s_t[:q0, :]
            m_col = jnp.maximum(m_col, jnp.max(s_full, axis=0, keepdims=True))
        e_diag = jnp.exp2(s_diag - m_col)
        denom = jnp.sum(e_diag, axis=0, keepdims=True)
        o_t = _dot(vt_ref[heads, q0:L], e_diag.astype(BF16))
        if i > 0:
            e_full = jnp.exp2(s_full - m_col)
            denom = denom + jnp.sum(e_full, axis=0, keepdims=True)
            o_t = o_t + _dot(vt_ref[heads, :q0], e_full.astype(BF16))
        o_ref[q0:L, a * LANES:(a + 1) * LANES] = (o_t / denom).T.astype(BF16)
        _convert_pieces(pieces, pos, len(tasks))


def _attention(q_nope, q_rope, k_nope, k_rope, v_t, cast_jobs):
    m = q_nope.shape[0]
    hps = ATTN_HEADS_PER_STEP
    grid = (m // SEQ, N_HEADS // hps)
    head = pl.BlockSpec((SEQ, hps * LANES), lambda b, h: (b, h))
    specs = [_slab_specs(w, grid[0] * grid[1], lambda b, h: b * grid[1] + h, row0)
             for w, row0 in cast_jobs]
    outs = pl.pallas_call(
        functools.partial(_attn_kernel, n_cast=len(cast_jobs)),
        grid=grid,
        in_specs=[head, head, head, pl.BlockSpec((SEQ, LANES), lambda b, h: (b, 0)),
                  pl.BlockSpec((hps * V_HEAD, SEQ), lambda b, h: (h, b))] + [s[0] for s in specs],
        out_specs=[head] + [s[1] for s in specs],
        out_shape=[jax.ShapeDtypeStruct((m, HEAD_PAD), BF16)] + [s[2] for s in specs],
        compiler_params=_params(2),
        name="mla_attention",
    )(q_nope, q_rope, k_nope, k_rope, v_t, *[w for w, _ in cast_jobs])
    return outs[0], outs[1:]


def _merge_kernel(*refs, n_cast):
    (a_ref, c_ref, wa_ref, wb_ref, g_ref), cast_src, o_ref, cast_dst = _split_refs(refs, 5, n_cast)
    pieces = _cast_pieces(cast_src, cast_dst)
    a = a_ref[...]
    cm = c_ref[...]
    n = o_ref.shape[1]
    chunks = _col_chunks(n, MXU_WIDTH)
    for k, (c0, width) in enumerate(chunks):
        cols = slice(c0, c0 + width)
        y_a = _dot(a, wa_ref[:, cols])
        y_b = _dot(cm, wb_ref[:, cols])
        o_ref[:, cols] = (g_ref[:, cols].astype(F32) * y_a
                          + g_ref[:, n + c0:n + c0 + width].astype(F32) * y_b).astype(BF16)
        _convert_pieces(pieces, k, len(chunks))


def _merge(attn, cm, w_oa, w_ob, gates, cast_jobs, bm=512):
    m, d = attn.shape
    n = w_oa.shape[1]
    lhs = pl.BlockSpec((bm, d), lambda i: (i, 0))
    specs = [_slab_specs(w, m // bm, lambda i: i, row0) for w, row0 in cast_jobs]
    outs = pl.pallas_call(
        functools.partial(_merge_kernel, n_cast=len(cast_jobs)),
        grid=(m // bm,),
        in_specs=[lhs, lhs, _resident((d, n), lambda i: (0, 0)), _resident((d, n), lambda i: (0, 0)),
                  pl.BlockSpec((bm, 2 * n), lambda i: (i, 0))] + [s[0] for s in specs],
        out_specs=[pl.BlockSpec((bm, n), lambda i: (i, 0))] + [s[1] for s in specs],
        out_shape=[jax.ShapeDtypeStruct((m, n), BF16)] + [s[2] for s in specs],
        compiler_params=_params(1),
        name="branch_merge",
    )(attn, cm, w_oa, w_ob, gates, *[w for w, _ in cast_jobs])
    return outs[0], outs[1:]


def _mix_out_kernel(mg_ref, w_ref, x_ref, g1_ref, lg_ref, lb_ref, o_ref, *, tiles_per_seq):
    gate1 = _batch_row(g1_ref, pl.program_id(0), tiles_per_seq)
    n_chunks = mg_ref.shape[0] // ROW_CHUNK
    mix = _dot(mg_ref[0:ROW_CHUNK, :], w_ref[...])
    for r in range(n_chunks):
        rows = slice(r * ROW_CHUNK, (r + 1) * ROW_CHUNK)
        cur = mix
        if r + 1 < n_chunks:
            mix = _dot(mg_ref[(r + 1) * ROW_CHUNK:(r + 2) * ROW_CHUNK, :], w_ref[...])
        o_ref[rows, :] = _deepnorm_ln(x_ref[rows, :], gate1, cur, lg_ref[...], lb_ref[...])


def _mix_out(merged, w_o, x2, mod, ln_g, ln_b, bm=512):
    m, d = x2.shape
    per_seq = SEQ // bm
    row = pl.BlockSpec((bm, d), lambda i: (i, 0))
    vec = pl.BlockSpec((1, d), lambda i: (0, 0))
    return pl.pallas_call(
        functools.partial(_mix_out_kernel, tiles_per_seq=per_seq),
        grid=(m // bm,),
        in_specs=[row, _resident((d, d), lambda i: (0, 0)), row,
                  _mod_spec(mod, MOD_GATE1), vec, vec],
        out_specs=row,
        out_shape=jax.ShapeDtypeStruct((m, d), F32),
        compiler_params=_params(1),
        name="mix_out_ln",
    )(merged, w_o, x2, mod, ln_g, ln_b)


def _ffn_kernel(x_ref, sc_ref, sh_ref, g2_ref, wg_ref, wu_ref, wo_ref, lg_ref, lb_ref, o_ref,
                u_ref, *, tiles_per_seq):
    acc_ref = o_ref
    tile = pl.program_id(0)
    j = pl.program_id(1)

    @pl.when(j == 0)
    def _():
        u_ref[...] = _modulate(x_ref[...], _batch_row(sc_ref, tile, tiles_per_seq),
                               _batch_row(sh_ref, tile, tiles_per_seq))
        acc_ref[...] = jnp.zeros_like(acc_ref)

    u = u_ref[...]
    tf = wg_ref.shape[1]
    half = tf // 2
    hs = []
    for c0 in (0, half):
        hg = _dot(u, wg_ref[:, c0:c0 + half])
        hs.append((hg * jax.nn.sigmoid(hg) * _dot(u, wu_ref[:, c0:c0 + half])).astype(BF16))
    for c0 in range(0, acc_ref.shape[1], COL_CHUNK):
        cols = slice(c0, c0 + COL_CHUNK)
        acc_ref[:, cols] += _dot(hs[0], wo_ref[:half, cols]) + _dot(hs[1], wo_ref[half:, cols])

    @pl.when(j == pl.num_programs(1) - 1)
    def _():
        o_ref[...] = _deepnorm_ln(x_ref[...], _batch_row(g2_ref, tile, tiles_per_seq), acc_ref[...],
                                  lg_ref[...], lb_ref[...])


def _ffn(x1, mod, w_in, w_out, ln_g, ln_b, bm=1024, tf=512):
    m, d = x1.shape
    nf = D_FF // tf
    per_seq = SEQ // bm
    row = pl.BlockSpec((bm, d), lambda i, j: (i, 0))
    vec = pl.BlockSpec((1, d), lambda i, j: (0, 0))
    return pl.pallas_call(
        functools.partial(_ffn_kernel, tiles_per_seq=per_seq),
        grid=(m // bm, nf),
        in_specs=[row, _mod_spec(mod, MOD_SCALE2), _mod_spec(mod, MOD_SHIFT2),
                  _mod_spec(mod, MOD_GATE2),
                  pl.BlockSpec((d, tf), lambda i, j: (0, j)),
                  pl.BlockSpec((d, tf), lambda i, j: (0, j + nf)),
                  pl.BlockSpec((tf, d), lambda i, j: (j, 0)), vec, vec],
        out_specs=row,
        out_shape=jax.ShapeDtypeStruct((m, d), F32),
        scratch_shapes=[pltpu.VMEM((bm, d), BF16)],
        compiler_params=_params(2, vmem=FFN_VMEM_LIMIT),
        name="swiglu_ffn_ln",
    )(x1, mod, mod, mod, w_in, w_in, w_out, ln_g, ln_b)


def kernel(x, c, positions, w_ada, b_ada, w_in, g_q_a, w_q_b, g_kv_a, w_kv_b, w_o_a,
           w_conv, w_o_b, w_o, ln1_g, ln1_b, w_ffn_in, w_ffn_out, ln2_g, ln2_b):
    bsz, seq, d = x.shape
    assert (seq, d) == (SEQ, D_MODEL) and w_ada.shape[0] == DEPTH
    m = bsz * seq
    x_cur = x.reshape(m, d)

    inv_freq = 1.0 / (ROPE_THETA ** (jnp.arange(0, QK_ROPE, 2, dtype=F32) / QK_ROPE))
    inv_row = jnp.tile(inv_freq, LANES // HALF).reshape(1, LANES)
    groups = LANES // HALF
    pos_dense = jnp.repeat(positions.reshape(m // groups, groups), HALF, axis=1)
    c_pad = jnp.pad(c, ((0, SUBLANES - bsz), (0, 0)))

    for l in range(DEPTH):
        mod = _ada(c_pad, w_ada[l], b_ada[l].reshape(1, -1))

        o_cb = Q_LORA + KV_LORA + QK_ROPE
        w_tf = jnp.swapaxes(w_in[l], 0, 1)

        qn, kvn, kr_raw, u = _latent(x_cur, mod, w_tf,
                                     g_q_a[l].reshape(1, -1), g_kv_a[l].reshape(1, -1))
        q_nope, q_rope, k_nope, v_t, k_rope = _qkv_up(qn, kvn, kr_raw, pos_dense, inv_row,
                                                      w_q_b[l], w_kv_b[l])
        attn, (w_cg, w_oa, w_ob, w_om, w_fi) = _attention(
            q_nope, q_rope, k_nope, k_rope, v_t,
            [(w_tf, o_cb), (w_o_a[l], 0), (w_o_b[l], 0), (w_o[l], 0), (w_ffn_in[l], 0)])
        cm = _conv_branch(u, w_cg, w_conv[l])
        gates, _ = _gates(u, w_cg, 3, [])
        merged, (w_fo,) = _merge(attn, cm, w_oa, w_ob, gates, [(w_ffn_out[l], 0)])
        x_cur = _mix_out(merged, w_om, x_cur, mod,
                         ln1_g[l].reshape(1, -1), ln1_b[l].reshape(1, -1))

        x_cur = _ffn(x_cur, mod, w_fi, w_fo,
                     ln2_g[l].reshape(1, -1), ln2_b[l].reshape(1, -1))

    return x_cur.reshape(bsz, seq, d)
```
